```python
import math
import jax, jax.numpy as jnp
from jax import lax
import numpy as np

D_MODEL = 2048
BATCH = 4
SEQ = 2048
DEPTH = 2

FFN_DIM = int(math.ceil(8 * D_MODEL / 3 / 256)) * 256
N_SUBLAYERS = 3
RMS_EPS = 1e-6
S5_WIDTH = D_MODEL // 2
S5_GROUP_CH = 16
S5_GROUPS = S5_WIDTH // S5_GROUP_CH
S5_STATE = 64
SSD_INNER = D_MODEL
SSD_HEAD_DIM = 64
SSD_HEADS = SSD_INNER // SSD_HEAD_DIM
SSD_GROUPS = 8
SSD_STATE = 128
SSD_CONV = 4
SSD_CHUNK = 128
SSD_CONV_DIM = SSD_INNER + 2 * SSD_GROUPS * SSD_STATE
MIX_WIDTH = S5_WIDTH + SSD_INNER
IN_PROJ_DIM = S5_WIDTH + SSD_INNER + SSD_CONV_DIM + SSD_HEADS
RWKV_HEAD_DIM = 64
RWKV_HEADS = D_MODEL // RWKV_HEAD_DIM
DECAY_LORA = max(32, int(round(1.8 * math.sqrt(D_MODEL) / 32)) * 32)
AAA_LORA = max(32, int(round(1.8 * math.sqrt(D_MODEL) / 32)) * 32)
GATE_LORA = max(32, int(round(0.6 * D_MODEL ** 0.8 / 32)) * 32)
RWKV_GN_EPS = 64e-5

kernel_name = 'hybrid_s5_ssd_rwkv7_macaron_adaln'


def rmsnorm(x, g):
    xf = x.astype(jnp.float32)
    xn = xf * lax.rsqrt(jnp.mean(xf * xf, axis=-1, keepdims=True) + RMS_EPS)
    return xn.astype(x.dtype) * g


def adaln(x, g, m):
    return rmsnorm(x, g) * (1 + m[:, 1][:, None, :]) + m[:, 0][:, None, :]


def swiglu(h, w1, w3, w2):
    return (jax.nn.silu(h @ w1) * (h @ w3)) @ w2


def segsum(v):
    t = v.shape[-1]
    cs = jnp.cumsum(v, axis=-1)
    seg = cs[..., :, None] - cs[..., None, :]
    mask = jnp.tril(jnp.ones((t, t), dtype=bool))
    return jnp.where(mask, seg, -jnp.inf)


def s5_branch(u, lam_re, lam_im, log_dt, b_re, b_im, c_re, c_im, d_skip, glu_w, glu_b):
    bsz, seqlen, _ = u.shape
    uf = u.astype(jnp.float32).reshape(bsz, seqlen, S5_GROUPS, S5_GROUP_CH)
    dt = jnp.exp(log_dt.astype(jnp.float32))[:, None]
    lr = jnp.minimum(lam_re.astype(jnp.float32), -1e-4)
    li = lam_im.astype(jnp.float32)
    mag = jnp.exp(lr * dt)
    ang = li * dt
    lb_re, lb_im = mag * jnp.cos(ang), mag * jnp.sin(ang)
    den = lr * lr + li * li
    nr, ni = lb_re - 1.0, lb_im
    f_re = (nr * lr + ni * li) / den
    f_im = (ni * lr - nr * li) / den
    br, bi = b_re.astype(jnp.float32), b_im.astype(jnp.float32)
    bb_re = f_re[..., None] * br - f_im[..., None] * bi
    bb_im = f_re[..., None] * bi + f_im[..., None] * br
    bu_re = jnp.einsum('blgc,gpc->blgp', uf, bb_re)
    bu_im = jnp.einsum('blgc,gpc->blgp', uf, bb_im)
    a_re = jnp.broadcast_to(lb_re, (1, seqlen, S5_GROUPS, S5_STATE))
    a_im = jnp.broadcast_to(lb_im, (1, seqlen, S5_GROUPS, S5_STATE))

    def combine(e1, e2):
        a1r, a1i, b1r, b1i = e1
        a2r, a2i, b2r, b2i = e2
        return (a2r * a1r - a2i * a1i,
                a2r * a1i + a2i * a1r,
                a2r * b1r - a2i * b1i + b2r,
                a2r * b1i + a2i * b1r + b2i)

    _, _, s_re, s_im = lax.associative_scan(combine, (a_re, a_im, bu_re, bu_im), axis=1)
    y = (jnp.einsum('blgp,gcp->blgc', s_re, c_re.astype(jnp.float32))
         - jnp.einsum('blgp,gcp->blgc', s_im, c_im.astype(jnp.float32)))
    y = y.reshape(bsz, seqlen, S5_WIDTH) + d_skip.astype(jnp.float32) * uf.reshape(bsz, seqlen, S5_WIDTH)
    y = jax.nn.gelu(y)
    y = y * jax.nn.sigmoid(y @ glu_w.astype(jnp.float32) + glu_b.astype(jnp.float32))
    return y.astype(u.dtype)


def ssd_chunked(xs, dt, a, bs, cs):
    bsz, seqlen = xs.shape[:2]
    nc = seqlen // SSD_CHUNK
    rpg = SSD_HEADS // SSD_GROUPS
    xq = (xs * dt[..., None]).reshape(bsz, nc, SSD_CHUNK, SSD_GROUPS, rpg, SSD_HEAD_DIM)
    adt = jnp.moveaxis((dt * a).reshape(bsz, nc, SSD_CHUNK, SSD_GROUPS, rpg), 2, -1)
    bq = bs.reshape(bsz, nc, SSD_CHUNK, SSD_GROUPS, SSD_STATE)
    cq = cs.reshape(bsz, nc, SSD_CHUNK, SSD_GROUPS, SSD_STATE)
    a_cum = jnp.cumsum(adt, axis=-1)
    decay_in = jnp.exp(segsum(adt))
    cb = jnp.einsum('bclgn,bcsgn->bcgls', cq, bq)
    y_diag = jnp.einsum('bcgls,bcgrls,bcsgrp->bclgrp', cb, decay_in, xq)
    decay_to_end = jnp.exp(a_cum[..., -1:] - a_cum)
    states = jnp.einsum('bcsgn,bcgrs,bcsgrp->bcgrpn', bq, decay_to_end, xq)
    states = jnp.concatenate([jnp.zeros_like(states[:, :1]), states], axis=1)
    chunk_sum = jnp.pad(a_cum[..., -1], ((0, 0), (1, 0), (0, 0), (0, 0)))
    decay_chunk = jnp.exp(segsum(jnp.moveaxis(chunk_sum, 1, -1)))
    carried = jnp.einsum('bgrzc,bcgrpn->bzgrpn', decay_chunk, states)[:, :-1]
    y_off = jnp.einsum('bclgn,bcgrpn,bcgrl->bclgrp', cq, carried, jnp.exp(a_cum))
    return (y_diag + y_off).reshape(bsz, seqlen, SSD_HEADS, SSD_HEAD_DIM)


def ssd_branch(z, xbc, dt_raw, conv_w, conv_b, dt_bias, a_log, d_skip, norm_g):
    bsz, seqlen, _ = z.shape
    xbc = lax.conv_general_dilated(xbc, conv_w[:, None, :], window_strides=(1,),
                                   padding=[(SSD_CONV - 1, 0)],
                                   dimension_numbers=('NWC', 'WIO', 'NWC'),
                                   feature_group_count=SSD_CONV_DIM) + conv_b
    xbc = jax.nn.silu(xbc).astype(jnp.float32)
    xs = xbc[..., :SSD_INNER].reshape(bsz, seqlen, SSD_HEADS, SSD_HEAD_DIM)
    bs = xbc[..., SSD_INNER:SSD_INNER + SSD_GROUPS * SSD_STATE].reshape(bsz, seqlen, SSD_GROUPS, SSD_STATE)
    cs = xbc[..., SSD_INNER + SSD_GROUPS * SSD_STATE:].reshape(bsz, seqlen, SSD_GROUPS, SSD_STATE)
    dt = jax.nn.softplus(dt_raw.astype(jnp.float32) + dt_bias.astype(jnp.float32))
    a = -jnp.exp(a_log.astype(jnp.float32))
    y = ssd_chunked(xs, dt, a, bs, cs) + d_skip.astype(jnp.float32)[:, None] * xs
    y = y.reshape(bsz, seqlen, SSD_INNER) * jax.nn.silu(z.astype(jnp.float32))
    y = rmsnorm(y, norm_g.astype(jnp.float32))
    return y.astype(z.dtype)


def s5_ssd_mixer(h, w_in, w_out, lam_re, lam_im, log_dt, b_re, b_im, c_re, c_im, s5_d, glu_w, glu_b,
                 conv_w, conv_b, dt_bias, a_log, ssd_d, ssd_norm_g):
    proj = h @ w_in
    o1 = S5_WIDTH
    o2 = o1 + SSD_INNER
    o3 = o2 + SSD_CONV_DIM
    u, z, xbc, dt_raw = proj[..., :o1], proj[..., o1:o2], proj[..., o2:o3], proj[..., o3:]
    y_s5 = s5_branch(u, lam_re, lam_im, log_dt, b_re, b_im, c_re, c_im, s5_d, glu_w, glu_b)
    y_ssd = ssd_branch(z, xbc, dt_raw, conv_w, conv_b, dt_bias, a_log, ssd_d, ssd_norm_g)
    return jnp.concatenate([y_s5, y_ssd], axis=-1) @ w_out


def rwkv7_scan(r, w, k, v, a, b):
    bsz, _, nh, n = r.shape

    def step(s, inp):
        r_t, w_t, k_t, v_t, a_t, b_t = inp
        sa = jnp.einsum('bhvk,bhk->bhv', s, a_t)
        s = s * w_t[:, :, None, :] + sa[..., None] * b_t[:, :, None, :] + v_t[..., None] * k_t[:, :, None, :]
        return s, jnp.einsum('bhvk,bhk->bhv', s, r_t)

    xs = tuple(jnp.moveaxis(t, 1, 0) for t in (r, w, k, v, a, b))
    s0 = jnp.zeros((bsz, nh, n, n), jnp.float32)
    _, y = lax.scan(step, s0, xs)
    return jnp.moveaxis(y, 0, 1)


def rwkv7_mixer(h, mu, w_r, w_k, w_v, w_o, w0, w1, w2, a0, a1, a2, g1, g2, k_k, k_a, r_k, ln_g, ln_b):
    bsz, seqlen, _ = h.shape
    hs = (bsz, seqlen, RWKV_HEADS, RWKV_HEAD_DIM)
    xx = jnp.pad(h, ((0, 0), (1, 0), (0, 0)))[:, :-1] - h
    xr, xw, xk, xv, xa, xg = (h + xx * mu[i] for i in range(6))
    r = xr @ w_r
    k = xk @ w_k
    v = xv @ w_v
    w = -jax.nn.softplus(-(w0 + jnp.tanh(xw @ w1) @ w2)) - 0.5
    a = jax.nn.sigmoid(a0 + (xa @ a1) @ a2)
    g = jax.nn.sigmoid(xg @ g1) @ g2
    kk = (k * k_k).astype(jnp.float32).reshape(hs)
    kk = kk / jnp.maximum(jnp.linalg.norm(kk, axis=-1, keepdims=True), 1e-12)
    k = k * (1 + (a - 1) * k_a)
    rf = r.astype(jnp.float32).reshape(hs)
    kf = k.astype(jnp.float32).reshape(hs)
    vf = v.astype(jnp.float32).reshape(hs)
    af = a.astype(jnp.float32).reshape(hs)
    decay = jnp.exp(-jnp.exp(w.astype(jnp.float32))).reshape(hs)
    y = rwkv7_scan(rf, decay, kf, vf, -kk, kk * af)
    mean = jnp.mean(y, axis=-1, keepdims=True)
    var = jnp.mean(jnp.square(y - mean), axis=-1, keepdims=True)
    yn = ((y - mean) * lax.rsqrt(var + RWKV_GN_EPS)).reshape(bsz, seqlen, D_MODEL)
    yn = yn * ln_g.astype(jnp.float32) + ln_b.astype(jnp.float32)
    bonus = jnp.sum(rf * kf * r_k.astype(jnp.float32), axis=-1, keepdims=True) * vf
    y = yn + bonus.reshape(bsz, seqlen, D_MODEL)
    return (y.astype(h.dtype) * g) @ w_o


def setup_inputs(seed: int = 0) -> dict:
    key = jax.random.key(seed)
    keys = iter(jax.random.split(key, 48))
    nrm = lambda shape, s: jax.random.normal(next(keys), shape, jnp.float32) * s
    uni = lambda shape, lo, hi: jax.random.uniform(next(keys), shape, jnp.float32, lo, hi)
    ne = (DEPTH + 1) // 2
    no = DEPTH // 2
    d = D_MODEL
    dt0 = jnp.exp(uni((ne, SSD_HEADS), math.log(1e-3), math.log(1e-1)))
    lam_im0 = jnp.pi * jnp.arange(S5_STATE, dtype=jnp.float32)
    return {
        'x': nrm((BATCH, SEQ, d), 1.0),
        'c': nrm((BATCH, d), 1.0),
        'w_mod': nrm((DEPTH, d, N_SUBLAYERS * 3 * d), d ** -0.5),
        'b_mod': nrm((DEPTH, N_SUBLAYERS * 3 * d), 0.01),
        'norm_g': 1.0 + nrm((DEPTH, N_SUBLAYERS, d), 0.02),
        'ffn_w1': nrm((DEPTH, 2, d, FFN_DIM), d ** -0.5),
        'ffn_w3': nrm((DEPTH, 2, d, FFN_DIM), d ** -0.5),
        'ffn_w2': nrm((DEPTH, 2, FFN_DIM, d), FFN_DIM ** -0.5),
        'hyb_w_in': nrm((ne, d, IN_PROJ_DIM), d ** -0.5),
        'hyb_w_out': nrm((ne, MIX_WIDTH, d), MIX_WIDTH ** -0.5),
        's5_lambda_re': -0.5 + nrm((ne, S5_GROUPS, S5_STATE), 0.01),
        's5_lambda_im': lam_im0 + nrm((ne, S5_GROUPS, S5_STATE), 0.01),
        's5_log_dt': uni((ne, S5_GROUPS), math.log(1e-3), math.log(1e-1)),
        's5_b_re': nrm((ne, S5_GROUPS, S5_STATE, S5_GROUP_CH), (2 * S5_GROUP_CH) ** -0.5),
        's5_b_im': nrm((ne, S5_GROUPS, S5_STATE, S5_GROUP_CH), (2 * S5_GROUP_CH) ** -0.5),
        's5_c_re': nrm((ne, S5_GROUPS, S5_GROUP_CH, S5_STATE), (2 * S5_STATE) ** -0.5),
        's5_c_im': nrm((ne, S5_GROUPS, S5_GROUP_CH, S5_STATE), (2 * S5_STATE) ** -0.5),
        's5_d': nrm((ne, S5_WIDTH), 1.0),
        's5_glu_w': nrm((ne, S5_WIDTH, S5_WIDTH), S5_WIDTH ** -0.5),
        's5_glu_b': nrm((ne, S5_WIDTH), 0.01),
        'ssd_conv_w': nrm((ne, SSD_CONV, SSD_CONV_DIM), SSD_CONV ** -0.5),
        'ssd_conv_b': nrm((ne, SSD_CONV_DIM), 0.01),
        'ssd_dt_bias': dt0 + jnp.log(-jnp.expm1(-dt0)),
        'ssd_a_log': jnp.log(uni((ne, SSD_HEADS), 1.0, 16.0)),
        'ssd_d': 1.0 + nrm((ne, SSD_HEADS), 0.1),
        'ssd_norm_g': 1.0 + nrm((ne, SSD_INNER), 0.02),
        'rwkv_mu': uni((no, 6, d), 0.0, 1.0),
        'rwkv_w_r': nrm((no, d, d), d ** -0.5),
        'rwkv_w_k': nrm((no, d, d), d ** -0.5),
        'rwkv_w_v': nrm((no, d, d), d ** -0.5),
        'rwkv_w_o': nrm((no, d, d), d ** -0.5),
        'rwkv_w0': uni((no, d), -6.0, -1.0),
        'rwkv_w1': nrm((no, d, DECAY_LORA), d ** -0.5),
        'rwkv_w2': nrm((no, DECAY_LORA, d), 0.1 * DECAY_LORA ** -0.5),
        'rwkv_a0': nrm((no, d), 0.1),
        'rwkv_a1': nrm((no, d, AAA_LORA), d ** -0.5),
        'rwkv_a2': nrm((no, AAA_LORA, d), 0.1 * AAA_LORA ** -0.5),
        'rwkv_g1': nrm((no, d, GATE_LORA), d ** -0.5),
        'rwkv_g2': nrm((no, GATE_LORA, d), GATE_LORA ** -0.5),
        'rwkv_k_k': 0.85 + nrm((no, d), 0.02),
        'rwkv_k_a': 1.0 + nrm((no, d), 0.02),
        'rwkv_r_k': nrm((no, RWKV_HEADS, RWKV_HEAD_DIM), 0.1),
        'rwkv_ln_g': 1.0 + nrm((no, d), 0.02),
        'rwkv_ln_b': nrm((no, d), 0.01),
        'final_g': 1.0 + nrm((d,), 0.02),
    }


def reference(x, c, w_mod, b_mod, norm_g, ffn_w1, ffn_w3, ffn_w2, hyb_w_in, hyb_w_out,
              s5_lambda_re, s5_lambda_im, s5_log_dt, s5_b_re, s5_b_im, s5_c_re, s5_c_im, s5_d,
              s5_glu_w, s5_glu_b, ssd_conv_w, ssd_conv_b, ssd_dt_bias, ssd_a_log, ssd_d, ssd_norm_g,
              rwkv_mu, rwkv_w_r, rwkv_w_k, rwkv_w_v, rwkv_w_o, rwkv_w0, rwkv_w1, rwkv_w2,
              rwkv_a0, rwkv_a1, rwkv_a2, rwkv_g1, rwkv_g2, rwkv_k_k, rwkv_k_a, rwkv_r_k,
              rwkv_ln_g, rwkv_ln_b, final_g):
    bsz = x.shape[0]
    c_act = jax.nn.silu(c)
    for layer in range(DEPTH):
        mod = (c_act @ w_mod[layer] + b_mod[layer]).reshape(bsz, N_SUBLAYERS, 3, D_MODEL)
        h = adaln(x, norm_g[layer, 0], mod[:, 0])
        x = x + 0.5 * mod[:, 0, 2][:, None, :] * swiglu(h, ffn_w1[layer, 0], ffn_w3[layer, 0], ffn_w2[layer, 0])
        h = adaln(x, norm_g[layer, 1], mod[:, 1])
        i = layer // 2
        if layer % 2 == 0:
            y = s5_ssd_mixer(h, hyb_w_in[i], hyb_w_out[i], s5_lambda_re[i], s5_lambda_im[i], s5_log_dt[i],
                             s5_b_re[i], s5_b_im[i], s5_c_re[i], s5_c_im[i], s5_d[i], s5_glu_w[i], s5_glu_b[i],
                             ssd_conv_w[i], ssd_conv_b[i], ssd_dt_bias[i], ssd_a_log[i], ssd_d[i], ssd_norm_g[i])
        else:
            y = rwkv7_mixer(h, rwkv_mu[i], rwkv_w_r[i], rwkv_w_k[i], rwkv_w_v[i], rwkv_w_o[i],
                            rwkv_w0[i], rwkv_w1[i], rwkv_w2[i], rwkv_a0[i], rwkv_a1[i], rwkv_a2[i],
                            rwkv_g1[i], rwkv_g2[i], rwkv_k_k[i], rwkv_k_a[i], rwkv_r_k[i],
                            rwkv_ln_g[i], rwkv_ln_b[i])
        x = x + mod[:, 1, 2][:, None, :] * y
        h = adaln(x, norm_g[layer, 2], mod[:, 2])
        x = x + 0.5 * mod[:, 2, 2][:, None, :] * swiglu(h, ffn_w1[layer, 1], ffn_w3[layer, 1], ffn_w2[layer, 1])
    return rmsnorm(x, final_g)
```

```python
import functools
import math

import jax
import jax.numpy as jnp
from jax import lax
from jax.experimental import pallas as pl
from jax.experimental.pallas import tpu as pltpu

F32 = jnp.float32
BF16 = jnp.bfloat16
HI = lax.Precision.HIGHEST

RMS_EPS = 1e-6
RWKV_GN_EPS = 64e-5

S5_GROUP_CH = 16
S5_STATE = 64
S5_CHUNK = 16
SSD_HEAD_DIM = 64
SSD_STATE = 128
SSD_GROUPS = 8
SSD_CHUNK = 128
SSD_CONV = 4
RWKV_HEAD_DIM = 64
RWKV_CHUNK = 64
RWKV_TOKEN_BLOCK = 256
LANES = 128
VMEM_LIMIT = 56 * 1024 * 1024


def _params(sem, vmem=VMEM_LIMIT):
    return pltpu.CompilerParams(dimension_semantics=sem, vmem_limit_bytes=vmem)


def _dot(a, b, hi=False):
    dn = (((1,), (0,)), ((), ()))
    if hi:
        return lax.dot_general(a, b, dn, precision=HI, preferred_element_type=F32)
    return lax.dot_general(a.astype(BF16), b.astype(BF16), dn, preferred_element_type=F32)


def _dot_nt(a, b, hi=False):
    dn = (((1,), (1,)), ((), ()))
    if hi:
        return lax.dot_general(a, b, dn, precision=HI, preferred_element_type=F32)
    return lax.dot_general(a.astype(BF16), b.astype(BF16), dn, preferred_element_type=F32)


def _dot_tn(a, b):
    dn = (((0,), (0,)), ((), ()))
    return lax.dot_general(a.astype(BF16), b.astype(BF16), dn, preferred_element_type=F32)


def _silu(x):
    return x * jax.nn.sigmoid(x)


def _softplus(x):
    return jnp.maximum(x, 0.0) + jnp.log1p(jnp.exp(-jnp.abs(x)))


def _rms(x, g):
    return x * lax.rsqrt(jnp.mean(x * x, axis=-1, keepdims=True) + RMS_EPS) * g


def _adaln(x, g, m):
    return _rms(x, g) * (1.0 + m[1:2]) + m[0:1]


def _mod_kernel(c_ref, w_ref, b_ref, o_ref):
    c = c_ref[...]
    o_ref[0] = _dot(_silu(c), w_ref[0]) + b_ref[0]


def _modulation(c, w_mod, b_mod):
    depth, d, n = w_mod.shape
    bsz = c.shape[0]
    rows = 8
    cp = jnp.pad(c, ((0, rows - bsz), (0, 0)))
    tn = 1024
    out = pl.pallas_call(
        _mod_kernel,
        grid=(depth, n // tn),
        in_specs=[pl.BlockSpec((rows, d), lambda l, j: (0, 0)),
                  pl.BlockSpec((1, d, tn), lambda l, j: (l, 0, j)),
                  pl.BlockSpec((1, 1, tn), lambda l, j: (l, 0, j))],
        out_specs=pl.BlockSpec((1, rows, tn), lambda l, j: (l, 0, j)),
        out_shape=jax.ShapeDtypeStruct((depth, rows, n), F32),
        compiler_params=_params(("arbitrary", "arbitrary")),
        name="modulation",
    )(cp, w_mod, b_mod.reshape(depth, 1, n))
    return out[:, :bsz]


def _ffn_kernel(x_ref, m_ref, g_ref, w1_ref, w3_ref, w2_ref, *rest, final):
    if final:
        fg_ref, o_ref, h_scr, acc_scr = rest
    else:
        o_ref, h_scr, acc_scr = rest
    j = pl.program_id(1)

    @pl.when(j == 0)
    def _():
        h_scr[...] = _adaln(x_ref[...], g_ref[...], m_ref[0]).astype(BF16)
        acc_scr[...] = jnp.zeros_like(acc_scr)

    h = h_scr[...]
    a = jnp.dot(h, w1_ref[...], preferred_element_type=F32)
    b = jnp.dot(h, w3_ref[...], preferred_element_type=F32)
    p = (_silu(a) * b).astype(BF16)
    acc_scr[...] += jnp.dot(p, w2_ref[...], preferred_element_type=F32)

    @pl.when(j == pl.num_programs(1) - 1)
    def _():
        o = x_ref[...] + (0.5 * m_ref[0][2:3]) * acc_scr[...]
        if final:
            o = _rms(o, fg_ref[...])
        o_ref[...] = o


def _ffn(x, m, g, w1, w3, w2, seqlen, final_g=None):
    t, d = x.shape
    f = w1.shape[1]
    tm, tf = min(512, seqlen), 512
    per = seqlen // tm
    final = final_g is not None
    in_specs = [pl.BlockSpec((tm, d), lambda i, j: (i, 0)),
                pl.BlockSpec((1, 3, d), lambda i, j: (i // per, 0, 0)),
                pl.BlockSpec((1, d), lambda i, j: (0, 0)),
                pl.BlockSpec((d, tf), lambda i, j: (0, j)),
                pl.BlockSpec((d, tf), lambda i, j: (0, j)),
                pl.BlockSpec((tf, d), lambda i, j: (j, 0))]
    args = [x, m, g.reshape(1, d), w1, w3, w2]
    if final:
        in_specs.append(pl.BlockSpec((1, d), lambda i, j: (0, 0)))
        args.append(final_g.reshape(1, d))
    return pl.pallas_call(
        functools.partial(_ffn_kernel, final=final),
        grid=(t // tm, f // tf),
        in_specs=in_specs,
        out_specs=pl.BlockSpec((tm, d), lambda i, j: (i, 0)),
        out_shape=jax.ShapeDtypeStruct((t, d), F32),
        scratch_shapes=[pltpu.VMEM((tm, d), BF16), pltpu.VMEM((tm, d), F32)],
        compiler_params=_params(("parallel", "arbitrary")),
        name="ffn",
    )(*args)


def _adaln_kernel(x_ref, m_ref, g_ref, o_ref):
    o_ref[...] = _adaln(x_ref[...], g_ref[...], m_ref[0]).astype(o_ref.dtype)


def _adaln_call(x, m, g, seqlen, out_dtype):
    t, d = x.shape
    tm = min(512, seqlen)
    per = seqlen // tm
    return pl.pallas_call(
        _adaln_kernel,
        grid=(t // tm,),
        in_specs=[pl.BlockSpec((tm, d), lambda i: (i, 0)),
                  pl.BlockSpec((1, 3, d), lambda i: (i // per, 0, 0)),
                  pl.BlockSpec((1, d), lambda i: (0, 0))],
        out_specs=pl.BlockSpec((tm, d), lambda i: (i, 0)),
        out_shape=jax.ShapeDtypeStruct((t, d), out_dtype),
        compiler_params=_params(("parallel",)),
        name="adaln",
    )(x, m, g.reshape(1, d))


def _matmul_kernel(a_ref, w_ref, o_ref):
    o_ref[...] = _dot(a_ref[...], w_ref[...]).astype(o_ref.dtype)


def _matmul(a, w, out_dtype=F32):
    m, k = a.shape
    n = w.shape[1]
    tm = min(1024, m)
    tn = min(1024, n)
    return pl.pallas_call(
        _matmul_kernel,
        grid=(m // tm, n // tn),
        in_specs=[pl.BlockSpec((tm, k), lambda i, j: (i, 0)),
                  pl.BlockSpec((k, tn), lambda i, j: (0, j))],
        out_specs=pl.BlockSpec((tm, tn), lambda i, j: (i, j)),
        out_shape=jax.ShapeDtypeStruct((m, n), out_dtype),
        compiler_params=_params(("parallel", "arbitrary")),
        name="matmul",
    )(a, w)


def _matmul_resid_kernel(*refs, n_in):
    a_refs = refs[:n_in]
    w_refs = refs[n_in:2 * n_in]
    x_ref, m_ref, o_ref = refs[2 * n_in:]
    acc = _dot(a_refs[0][...], w_refs[0][...])
    for a_ref, w_ref in zip(a_refs[1:], w_refs[1:]):
        acc = acc + _dot(a_ref[...], w_ref[...])
    o_ref[...] = x_ref[...] + m_ref[0][2:3] * acc


def _matmul_resid(a_list, w_list, x, m, seqlen):
    t, d = x.shape
    tm, tn = min(512, seqlen), 1024
    per = seqlen // tm
    n_in = len(a_list)
    in_specs = [pl.BlockSpec((tm, a.shape[1]), lambda i, j: (i, 0)) for a in a_list]
    in_specs += [pl.BlockSpec((w.shape[0], tn), lambda i, j: (0, j)) for w in w_list]
    in_specs += [pl.BlockSpec((tm, tn), lambda i, j: (i, j)),
                 pl.BlockSpec((1, 3, tn), lambda i, j: (i // per, 0, j))]
    return pl.pallas_call(
        functools.partial(_matmul_resid_kernel, n_in=n_in),
        grid=(t // tm, d // tn),
        in_specs=in_specs,
        out_specs=pl.BlockSpec((tm, tn), lambda i, j: (i, j)),
        out_shape=jax.ShapeDtypeStruct((t, d), F32),
        compiler_params=_params(("parallel", "arbitrary")),
        name="matmul_resid",
    )(*a_list, *w_list, x, m)


def _s5_prep_kernel(lre_ref, lim_ref, ldt_ref, btr_ref, bti_ref, cr_ref, ci_ref,
                    mt_ref, wsr_ref, wsi_ref, vtr_ref, vti_ref, ar_ref, ai_ref):
    q, gc, p = S5_CHUNK, S5_GROUP_CH, S5_STATE
    lr = jnp.minimum(lre_ref[0], -1e-4)
    li = lim_ref[0]
    dt = jnp.exp(ldt_ref[0])

    def lam_pow(mult):
        mag = jnp.exp(lr * dt * mult)
        ang = li * dt * mult
        return mag * jnp.cos(ang), mag * jnp.sin(ang)

    one = jnp.ones((1, 1), F32)
    lb_re, lb_im = lam_pow(one)
    den = lr * lr + li * li
    nr, ni = lb_re - 1.0, lb_im
    f_re = (nr * lr + ni * li) / den
    f_im = (ni * lr - nr * li) / den
    btr, bti = btr_ref[0], bti_ref[0]
    bb_re = f_re * btr - f_im * bti
    bb_im = f_re * bti + f_im * btr
    cr, ci = cr_ref[0], ci_ref[0]

    qc = q * gc
    row_i = lax.broadcasted_iota(jnp.int32, (qc, gc), 0)
    col_i = lax.broadcasted_iota(jnp.int32, (qc, gc), 1)
    rep = lambda a: _dot((row_i // gc == col_i).astype(F32), a, hi=True)
    tile = lambda a: _dot((row_i % gc == col_i).astype(F32), a, hi=True)
    tau = lax.broadcasted_iota(jnp.int32, (q, 1), 0).astype(F32)
    pr, pi = lam_pow(tau)
    pr, pi = rep(pr), rep(pi)
    cr_t, ci_t, br_t, bi_t = tile(cr), tile(ci), tile(bb_re), tile(bb_im)
    kw = (_dot_nt(pr * cr_t - pi * ci_t, br_t, hi=True)
          - _dot_nt(pr * ci_t + pi * cr_t, bi_t, hi=True))
    lane_blk = lax.broadcasted_iota(jnp.int32, (qc, qc), 1) // gc
    mt = jnp.where(lane_blk == 0, kw, 0.0)
    for j in range(1, q):
        sh = jnp.concatenate([jnp.zeros((j * gc, qc), F32), kw[:(q - j) * gc]], axis=0)
        mt = jnp.where(lane_blk == j, sh, mt)
    mt_ref[0] = mt.astype(BF16)

    pr_r, pi_r = lam_pow((q - 1) - tau)
    pr_r, pi_r = rep(pr_r), rep(pi_r)
    wsr_ref[0] = (pr_r * br_t - pi_r * bi_t).astype(BF16)
    wsi_ref[0] = (pr_r * bi_t + pi_r * br_t).astype(BF16)

    pr1, pi1 = lam_pow(tau + 1.0)
    pr1, pi1 = rep(pr1), rep(pi1)
    vtr_ref[0] = (pr1 * cr_t - pi1 * ci_t).astype(BF16)
    vti_ref[0] = (-(pr1 * ci_t + pi1 * cr_t)).astype(BF16)

    lvl = lax.broadcasted_iota(jnp.int32, (ar_ref.shape[1], 1), 0)
    step = lax.shift_left(jnp.full_like(lvl, q), lvl).astype(F32)
    a_re, a_im = lam_pow(step)
    ar_ref[0] = a_re
    ai_ref[0] = a_im


def _s5_core_kernel(x_ref, mt_ref, wsr_ref, wsi_ref, vtr_ref, vti_ref, ar_ref, ai_ref, o_ref, *, nc, levels):
    x = x_ref[0].astype(BF16)
    y = _dot_nt(x, mt_ref[0])
    sr = _dot(x, wsr_ref[0])
    si = _dot(x, wsi_ref[0])
    rows = x.shape[0]
    cidx = lax.broadcasted_iota(jnp.int32, (rows, 1), 0) % nc
    a_re, a_im = ar_ref[0], ai_ref[0]
    for k in range(levels):
        sh = 1 << k
        xr = pltpu.roll(sr, sh, axis=0)
        xi = pltpu.roll(si, sh, axis=0)
        ok = cidx >= sh
        ar, ai = a_re[k:k + 1], a_im[k:k + 1]
        sr, si = (sr + jnp.where(ok, ar * xr - ai * xi, 0.0),
                  si + jnp.where(ok, ar * xi + ai * xr, 0.0))
    ok = cidx >= 1
    pr = jnp.where(ok, pltpu.roll(sr, 1, axis=0), 0.0)
    pi = jnp.where(ok, pltpu.roll(si, 1, axis=0), 0.0)
    o_ref[0] = y + _dot_nt(pr, vtr_ref[0]) + _dot_nt(pi, vti_ref[0])


def _s5_post_kernel(y_ref, u_ref, d_ref, w_ref, b_ref, o_ref):
    y = y_ref[...] + d_ref[...] * u_ref[...]
    y = jax.nn.gelu(y)
    o_ref[...] = (y * jax.nn.sigmoid(_dot(y, w_ref[...]) + b_ref[...])).astype(o_ref.dtype)


def _s5_branch(u, bsz, seqlen, lam_re, lam_im, log_dt, b_re, b_im, c_re, c_im, d_skip, glu_w, glu_b):
    t, width = u.shape
    q, gc, p = S5_CHUNK, S5_GROUP_CH, S5_STATE
    groups = width // gc
    nc = seqlen // q
    rows = bsz * nc
    levels = int(math.log2(nc))
    assert (1 << levels) == nc
    lv_rows = 8
    qc = q * gc
    g3 = lambda a, b: pl.BlockSpec((1, a, b), lambda g: (g, 0, 0))
    prep = pl.pallas_call(
        _s5_prep_kernel,
        grid=(groups,),
        in_specs=[g3(1, p), g3(1, p), g3(1, p), g3(gc, p), g3(gc, p), g3(gc, p), g3(gc, p)],
        out_specs=[g3(qc, qc), g3(qc, p), g3(qc, p), g3(qc, p), g3(qc, p), g3(lv_rows, p), g3(lv_rows, p)],
        out_shape=[jax.ShapeDtypeStruct((groups, qc, qc), BF16)]
        + [jax.ShapeDtypeStruct((groups, qc, p), BF16)] * 4
        + [jax.ShapeDtypeStruct((groups, lv_rows, p), F32)] * 2,
        compiler_params=_params(("parallel",)),
        name="s5_prep",
    )(lam_re.reshape(groups, 1, p), lam_im.reshape(groups, 1, p),
      jnp.broadcast_to(log_dt[:, None, None], (groups, 1, p)),
      jnp.swapaxes(b_re, 1, 2), jnp.swapaxes(b_im, 1, 2), c_re, c_im)
    mt, wsr, wsi, vtr, vti, a_re, a_im = prep

    xg = u.reshape(bsz, nc, q, groups, gc).transpose(3, 0, 1, 2, 4).reshape(groups, rows, qc)
    yt = pl.pallas_call(
        functools.partial(_s5_core_kernel, nc=nc, levels=levels),
        grid=(groups,),
        in_specs=[g3(rows, qc), g3(qc, qc), g3(qc, p), g3(qc, p), g3(qc, p), g3(qc, p),
                  g3(lv_rows, p), g3(lv_rows, p)],
        out_specs=g3(rows, qc),
        out_shape=jax.ShapeDtypeStruct((groups, rows, qc), F32),
        compiler_params=_params(("parallel",)),
        name="s5_core",
    )(xg, mt, wsr, wsi, vtr, vti, a_re, a_im)
    y = yt.reshape(groups, bsz, nc, q, gc).transpose(1, 2, 3, 0, 4).reshape(t, width)

    tm = min(512, t)
    return pl.pallas_call(
        _s5_post_kernel,
        grid=(t // tm,),
        in_specs=[pl.BlockSpec((tm, width), lambda i: (i, 0)),
                  pl.BlockSpec((tm, width), lambda i: (i, 0)),
                  pl.BlockSpec((1, width), lambda i: (0, 0)),
                  pl.BlockSpec((width, width), lambda i: (0, 0)),
                  pl.BlockSpec((1, width), lambda i: (0, 0))],
        out_specs=pl.BlockSpec((tm, width), lambda i: (i, 0)),
        out_shape=jax.ShapeDtypeStruct((t, width), BF16),
        compiler_params=_params(("parallel",)),
        name="s5_post",
    )(y, u, d_skip.reshape(1, width), glu_w.astype(BF16), glu_b.reshape(1, width))


def _ssd_kernel(z_ref, xbc_ref, dt_ref, cw_ref, cb_ref, dtb_ref, alog_ref, dsk_ref, ng_ref, exp_ref,
                o_ref, ext_scr, st_scr, *, heads, inner):
    lc = SSD_CHUNK
    hd, ns = SSD_HEAD_DIM, SSD_STATE
    rpg = heads // SSD_GROUPS
    gw = SSD_GROUPS * ns
    tail = SSD_CONV - 1

    @pl.when(pl.program_id(1) == 0)
    def _():
        ext_scr[0:8, :] = jnp.zeros((8, ext_scr.shape[1]), F32)
        st_scr[...] = jnp.zeros_like(st_scr)

    ext_scr[8:8 + lc, :] = xbc_ref[...]
    cw = cw_ref[...]
    conv = cb_ref[...] + cw[0:1] * ext_scr[pl.ds(8 - tail, lc), :]
    for k in range(1, SSD_CONV):
        conv = conv + cw[k:k + 1] * ext_scr[pl.ds(8 - tail + k, lc), :]
    ext_scr[0:8, :] = ext_scr[lc:lc + 8, :]
    act = _silu(conv)
    xs = act[:, :inner]
    bs = act[:, inner:inner + gw]
    cs = act[:, inner + gw:]

    dt = _softplus(dt_ref[...] + dtb_ref[...])
    adt = dt * (-jnp.exp(alog_ref[...]))
    ri = lax.broadcasted_iota(jnp.int32, (lc, lc), 0)
    ci = lax.broadcasted_iota(jnp.int32, (lc, lc), 1)
    causal = ri >= ci
    a_cum = _dot(causal.astype(F32), adt, hi=True)
    a_cum_t = a_cum.T
    expand = exp_ref[...]
    xq = xs * _dot(dt, expand, hi=True)
    ea = _dot(jnp.exp(a_cum), expand, hi=True)

    y_parts = []
    for g in range(SSD_GROUPS):
        cs_g = cs[:, g * ns:(g + 1) * ns]
        bs_g = bs[:, g * ns:(g + 1) * ns]
        cb = _dot_nt(cs_g, bs_g)
        bs_gt = bs_g.T
        for r in range(rpg):
            h = g * rpg + r
            col = a_cum[:, h:h + 1]
            row = a_cum_t[h:h + 1, :]
            decay = jnp.exp(jnp.where(causal, col - row, -jnp.inf))
            xq_h = xq[:, h * hd:(h + 1) * hd]
            st = st_scr[h]
            y_h = _dot(cb * decay, xq_h)
            y_parts.append((y_h, _dot(cs_g, st)))
            last = row[:, lc - 1:lc]
            to_end = jnp.exp(last - row)
            st_scr[h] = jnp.exp(last) * st + _dot(bs_gt * to_end, xq_h)
    y_diag = jnp.concatenate([a for a, _ in y_parts], axis=1)
    y_off = jnp.concatenate([b for _, b in y_parts], axis=1)
    y = y_diag + y_off * ea + dsk_ref[...] * xs
    y = y * _silu(z_ref[...])
    o_ref[...] = _rms(y, ng_ref[...]).astype(o_ref.dtype)


def _ssd_branch(z, xbc, dt_raw, bsz, seqlen, conv_w, conv_b, dt_bias, a_log, d_skip, norm_g):
    t, inner = z.shape
    cd = xbc.shape[1]
    heads = inner // SSD_HEAD_DIM
    lc = SSD_CHUNK
    nc = seqlen // lc
    pad = LANES - heads
    padv = lambda v: jnp.pad(v.reshape(1, heads), ((0, 0), (0, pad)))
    expand = jnp.repeat(jnp.eye(LANES, heads, dtype=F32), SSD_HEAD_DIM, axis=1)
    row = lambda n: pl.BlockSpec((1, n), lambda b, c: (0, 0))
    tok = lambda n: pl.BlockSpec((lc, n), lambda b, c: (b * nc + c, 0))
    return pl.pallas_call(
        functools.partial(_ssd_kernel, heads=heads, inner=inner),
        grid=(bsz, nc),
        in_specs=[tok(inner), tok(cd), tok(LANES),
                  pl.BlockSpec((SSD_CONV, cd), lambda b, c: (0, 0)), row(cd),
                  row(LANES), row(LANES), row(inner), row(inner),
                  pl.BlockSpec((LANES, inner), lambda b, c: (0, 0))],
        out_specs=tok(inner),
        out_shape=jax.ShapeDtypeStruct((t, inner), BF16),
        scratch_shapes=[pltpu.VMEM((8 + lc + 8, cd), F32),
                        pltpu.VMEM((heads, SSD_STATE, SSD_HEAD_DIM), F32)],
        compiler_params=_params(("parallel", "arbitrary")),
        name="ssd",
    )(z, xbc, dt_raw, conv_w, conv_b.reshape(1, cd), padv(dt_bias), padv(a_log),
      jnp.repeat(d_skip, SSD_HEAD_DIM).reshape(1, inner), norm_g.reshape(1, inner), expand)


def _hybrid_mixer(x, m, g, seqlen, w_in, w_out, lam_re, lam_im, log_dt, b_re, b_im, c_re, c_im, s5_d,
                  glu_w, glu_b, conv_w, conv_b, dt_bias, a_log, ssd_d, ssd_norm_g):
    t, d = x.shape
    bsz = t // seqlen
    s5w = s5_d.shape[0]
    inner = ssd_norm_g.shape[0]
    cd = conv_w.shape[1]
    heads = dt_bias.shape[0]
    o1, o2, o3 = s5w, s5w + inner, s5w + inner + cd
    h = _adaln_call(x, m, g, seqlen, BF16)
    wb = w_in.astype(BF16)
    u = _matmul(h, wb[:, :o1])
    z = _matmul(h, wb[:, o1:o2])
    xbc = _matmul(h, wb[:, o2:o3])
    dt_raw = _matmul(h, jnp.pad(wb[:, o3:], ((0, 0), (0, LANES - heads))))
    y_s5 = _s5_branch(u, bsz, seqlen, lam_re, lam_im, log_dt, b_re, b_im, c_re, c_im, s5_d, glu_w, glu_b)
    y_ssd = _ssd_branch(z, xbc, dt_raw, bsz, seqlen, conv_w, conv_b, dt_bias, a_log, ssd_d, ssd_norm_g)
    wo = w_out.astype(BF16)
    return _matmul_resid([y_s5, y_ssd], [wo[:s5w], wo[s5w:]], x, m, seqlen)


def _rwkv_mix_kernel(x_ref, m_ref, g_ref, mu_ref, *rest):
    o_refs, prev_scr = rest[:6], rest[6]
    h = _adaln(x_ref[...], g_ref[...], m_ref[0])
    tm = h.shape[0]

    @pl.when(pl.program_id(1) == 0)
    def _():
        prev_scr[...] = jnp.zeros_like(prev_scr)

    first = lax.broadcasted_iota(jnp.int32, (tm, 1), 0) == 0
    shifted = jnp.where(first, prev_scr[7:8, :], pltpu.roll(h, 1, axis=0))
    prev_scr[...] = h[tm - 8:tm]
    xx = shifted - h
    mu = mu_ref[...]
    for i, o_ref in enumerate(o_refs):
        o_ref[...] = (h + xx * mu[i:i + 1]).astype(o_ref.dtype)


def _rwkv_mix(x, m, g, mu, bsz, seqlen):
    t, d = x.shape
    tm = min(256, seqlen)
    per = seqlen // tm
    tok = pl.BlockSpec((tm, d), lambda b, i: (b * per + i, 0))
    return pl.pallas_call(
        _rwkv_mix_kernel,
        grid=(bsz, per),
        in_specs=[tok, pl.BlockSpec((1, 3, d), lambda b, i: (b, 0, 0)),
                  pl.BlockSpec((1, d), lambda b, i: (0, 0)),
                  pl.BlockSpec((8, d), lambda b, i: (0, 0))],
        out_specs=[tok] * 6,
        out_shape=[jax.ShapeDtypeStruct((t, d), BF16)] * 6,
        scratch_shapes=[pltpu.VMEM((8, d), F32)],
        compiler_params=_params(("parallel", "arbitrary")),
        name="rwkv_mix",
    )(x, m, g.reshape(1, d), jnp.pad(mu, ((0, 2), (0, 0))))


def _lora_kernel(x_ref, a_ref, b_ref, bias_ref, o_ref, *, mode):
    mid = _dot(x_ref[...], a_ref[...])
    if mode == "decay":
        mid = jnp.tanh(mid)
    elif mode == "gate":
        mid = jax.nn.sigmoid(mid)
    out = _dot(mid, b_ref[...])
    if mode == "decay":
        w = -_softplus(-(bias_ref[...] + out)) - 0.5
        out = -jnp.exp(w)
    elif mode == "icl":
        out = jax.nn.sigmoid(bias_ref[...] + out)
    o_ref[...] = out


def _lora(x, a, b, bias, mode):
    t, d = x.shape
    rank = a.shape[1]
    rp = -(-rank // LANES) * LANES
    ap = jnp.pad(a, ((0, 0), (0, rp - rank))).astype(BF16)
    bp = jnp.pad(b, ((0, rp - rank), (0, 0))).astype(BF16)
    tm = min(512, t)
    return pl.pallas_call(
        functools.partial(_lora_kernel, mode=mode),
        grid=(t // tm,),
        in_specs=[pl.BlockSpec((tm, d), lambda i: (i, 0)),
                  pl.BlockSpec((d, rp), lambda i: (0, 0)),
                  pl.BlockSpec((rp, d), lambda i: (0, 0)),
                  pl.BlockSpec((1, d), lambda i: (0, 0))],
        out_specs=pl.BlockSpec((tm, d), lambda i: (i, 0)),
        out_shape=jax.ShapeDtypeStruct((t, d), F32),
        compiler_params=_params(("parallel",)),
        name="rwkv_lora_" + mode,
    )(x, ap, bp, bias.reshape(1, d))


def _rwkv_scan_kernel(r_ref, k_ref, v_ref, lw_ref, a_ref, g_ref, kk_ref, ka_ref, rk_ref, lg_ref, lb_ref,
                      o_ref, s_scr):
    c = RWKV_CHUNK
    hd = RWKV_HEAD_DIM
    c2 = 2 * c
    n_chunks = r_ref.shape[0] // c

    @pl.when(pl.program_id(2) == 0)
    def _():
        s_scr[...] = jnp.zeros_like(s_scr)

    lane_head = lax.broadcasted_iota(jnp.int32, (1, LANES), 1) // hd
    head0 = lane_head == 0
    same_head = ((lax.broadcasted_iota(jnp.int32, (LANES, LANES), 0) // hd)
                 == (lax.broadcasted_iota(jnp.int32, (LANES, LANES), 1) // hd))
    ones_bd = same_head.astype(F32)
    ri = lax.broadcasted_iota(jnp.int32, (c, c), 0)
    ci = lax.broadcasted_iota(jnp.int32, (c, c), 1)
    tri_incl = (ri >= ci).astype(F32)
    r2 = lax.broadcasted_iota(jnp.int32, (c2, c2), 0)
    q2 = lax.broadcasted_iota(jnp.int32, (c2, c2), 1)
    same_blk = (r2 // c) == (q2 // c)
    strict = same_blk & ((r2 % c) > (q2 % c))
    incl = same_blk & ((r2 % c) >= (q2 % c))
    eye2 = (r2 == q2).astype(F32)
    k_k, k_a, r_k = kk_ref[...], ka_ref[...], rk_ref[...]
    ln_g, ln_b = lg_ref[...], lb_ref[...]

    def stack(x):
        return jnp.concatenate([jnp.where(head0, x, 0.0), jnp.where(head0, 0.0, x)], axis=0)

    def chunk(i, carry):
        rows = pl.ds(pl.multiple_of(i * c, c), c)
        r, k, v, lw, a = r_ref[rows, :], k_ref[rows, :], v_ref[rows, :], lw_ref[rows, :], a_ref[rows, :]
        kk = k * k_k
        nrm = jnp.sqrt(_dot(kk * kk, ones_bd, hi=True))
        kk = kk / jnp.maximum(nrm, 1e-12)
        k2 = k * (1.0 + (a - 1.0) * k_a)
        bv = kk * a
        cum = _dot(tri_incl, lw, hi=True)
        tot = cum[c - 1:c, :]
        g_inv = jnp.exp(-cum)
        at = stack(-kk * jnp.exp(cum - lw))
        rt = stack(r * jnp.exp(cum))
        bt = stack(bv * g_inv)
        kt = stack(k2 * g_inv)
        to_end = jnp.exp(tot - cum)
        vs = stack(v)

        n_ab = jnp.where(strict, _dot_nt(at, bt), 0.0)
        a_ak = jnp.where(strict, _dot_nt(at, kt), 0.0)
        a_rb = jnp.where(incl, _dot_nt(rt, bt), 0.0)
        a_rk = jnp.where(incl, _dot_nt(rt, kt), 0.0)
        tinv = eye2 + n_ab
        pw = n_ab
        for _ in range(int(math.log2(c)) - 1):
            pw = _dot(pw, pw, hi=True)
            tinv = tinv + _dot(tinv, pw, hi=True)

        s = s_scr[...]
        sh = _dot_nt(jnp.concatenate([at[:c] + at[c:], rt[:c] + rt[c:]], axis=0), s)
        u = _dot(tinv, stack(sh[:c]) + _dot(a_ak, vs))
        y2 = _dot(a_rb, u) + _dot(a_rk, vs)
        y = sh[c:] + y2[:c] + y2[c:]
        uu = u[:c] + u[c:]
        upd = _dot_tn(uu, bv * to_end) + _dot_tn(v, k2 * to_end)
        s_scr[...] = s * jnp.exp(tot) + jnp.where(same_head, upd, 0.0)

        mean = _dot(y, ones_bd, hi=True) * (1.0 / hd)
        dy = y - mean
        var = _dot(dy * dy, ones_bd, hi=True) * (1.0 / hd)
        yn = dy * lax.rsqrt(var + RWKV_GN_EPS) * ln_g + ln_b
        bonus = _dot(r * k2 * r_k, ones_bd, hi=True) * v
        o_ref[rows, :] = ((yn + bonus) * g_ref[rows, :]).astype(o_ref.dtype)
        return carry

    lax.fori_loop(0, n_chunks, chunk, 0)


def _rwkv_scan(r, k, v, lw, a, g, k_k, k_a, r_k, ln_g, ln_b, bsz, seqlen):
    t, d = r.shape
    tb = min(RWKV_TOKEN_BLOCK, seqlen)
    per = seqlen // tb
    tok = pl.BlockSpec((tb, LANES), lambda b, h, i: (b * per + i, h))
    row = pl.BlockSpec((1, LANES), lambda b, h, i: (0, h))
    vec = lambda p: p.reshape(1, d)
    return pl.pallas_call(
        _rwkv_scan_kernel,
        grid=(bsz, d // LANES, per),
        in_specs=[tok] * 6 + [row] * 5,
        out_specs=tok,
        out_shape=jax.ShapeDtypeStruct((t, d), BF16),
        scratch_shapes=[pltpu.VMEM((LANES, LANES), F32)],
        compiler_params=_params(("parallel", "parallel", "arbitrary")),
        name="rwkv_scan",
    )(r, k, v, lw, a, g, vec(k_k), vec(k_a), vec(r_k), vec(ln_g), vec(ln_b))


def _rwkv_mixer(x, m, g, seqlen, mu, w_r, w_k, w_v, w_o, w0, w1, w2, a0, a1, a2, g1, g2,
                k_k, k_a, r_k, ln_g, ln_b):
    t, d = x.shape
    bsz = t // seqlen
    xr, xw, xk, xv, xa, xg = _rwkv_mix(x, m, g, mu, bsz, seqlen)
    r = _matmul(xr, w_r.astype(BF16))
    k = _matmul(xk, w_k.astype(BF16))
    v = _matmul(xv, w_v.astype(BF16))
    lw = _lora(xw, w1, w2, w0, "decay")
    a = _lora(xa, a1, a2, a0, "icl")
    gate = _lora(xg, g1, g2, jnp.zeros_like(w0), "gate")
    y = _rwkv_scan(r, k, v, lw, a, gate, k_k, k_a, r_k.reshape(-1), ln_g, ln_b, bsz, seqlen)
    return _matmul_resid([y], [w_o.astype(BF16)], x, m, seqlen)


def kernel(x, c, w_mod, b_mod, norm_g, ffn_w1, ffn_w3, ffn_w2, hyb_w_in, hyb_w_out, s5_lambda_re, s5_lambda_im, s5_log_dt, s5_b_re, s5_b_im, s5_c_re, s5_c_im, s5_d, s5_glu_w, s5_glu_b, ssd_conv_w, ssd_conv_b, ssd_dt_bias, ssd_a_log, ssd_d, ssd_norm_g, rwkv_mu, rwkv_w_r, rwkv_w_k, rwkv_w_v, rwkv_w_o, rwkv_w0, rwkv_w1, rwkv_w2, rwkv_a0, rwkv_a1, rwkv_a2, rwkv_g1, rwkv_g2, rwkv_k_k, rwkv_k_a, rwkv_r_k, rwkv_ln_g, rwkv_ln_b, final_g):
    bsz, seqlen, d = x.shape
    depth = w_mod.shape[0]
    xf = x.reshape(bsz * seqlen, d)
    mod = _modulation(c, w_mod, b_mod).reshape(depth, bsz, 3, 3, d)
    for layer in range(depth):
        i = layer // 2
        m0, m1, m2 = mod[layer, :, 0], mod[layer, :, 1], mod[layer, :, 2]
        xf = _ffn(xf, m0, norm_g[layer, 0], ffn_w1[layer, 0].astype(BF16), ffn_w3[layer, 0].astype(BF16),
                  ffn_w2[layer, 0].astype(BF16), seqlen)
        if layer % 2 == 0:
            xf = _hybrid_mixer(xf, m1, norm_g[layer, 1], seqlen, hyb_w_in[i], hyb_w_out[i], s5_lambda_re[i],
                               s5_lambda_im[i], s5_log_dt[i], s5_b_re[i], s5_b_im[i], s5_c_re[i], s5_c_im[i],
                               s5_d[i], s5_glu_w[i], s5_glu_b[i], ssd_conv_w[i], ssd_conv_b[i], ssd_dt_bias[i],
                               ssd_a_log[i], ssd_d[i], ssd_norm_g[i])
        else:
            xf = _rwkv_mixer(xf, m1, norm_g[layer, 1], seqlen, rwkv_mu[i], rwkv_w_r[i], rwkv_w_k[i], rwkv_w_v[i],
                             rwkv_w_o[i], rwkv_w0[i], rwkv_w1[i], rwkv_w2[i], rwkv_a0[i], rwkv_a1[i], rwkv_a2[i],
                             rwkv_g1[i], rwkv_g2[i], rwkv_k_k[i], rwkv_k_a[i], rwkv_r_k[i], rwkv_ln_g[i],
                             rwkv_ln_b[i])
        xf = _ffn(xf, m2, norm_g[layer, 2], ffn_w1[layer, 1].astype(BF16), ffn_w3[layer, 1].astype(BF16),
                  ffn_w2[layer, 1].astype(BF16), seqlen, final_g=final_g if layer == depth - 1 else None)
    return xf.reshape(bsz, seqlen, d)
```

```python
import functools
import math

import jax
import jax.numpy as jnp
from jax import lax
from jax.experimental import pallas as pl
from jax.experimental.pallas import tpu as pltpu

F32 = jnp.float32
BF16 = jnp.bfloat16
HI = lax.Precision.HIGHEST

RMS_EPS = 1e-6
RWKV_GN_EPS = 64e-5

S5_GROUP_CH = 16
S5_STATE = 64
S5_CHUNK = 16
SSD_HEAD_DIM = 64
SSD_STATE = 128
SSD_GROUPS = 8
SSD_CHUNK = 128
SSD_CONV = 4
RWKV_HEAD_DIM = 64
RWKV_CHUNK = 64
RWKV_TOKEN_BLOCK = 512
LANES = 128
VMEM_LIMIT = 56 * 1024 * 1024


def _params(sem, vmem=VMEM_LIMIT):
    return pltpu.CompilerParams(dimension_semantics=sem, vmem_limit_bytes=vmem)


def _dot(a, b, hi=False):
    dn = (((1,), (0,)), ((), ()))
    if hi:
        return lax.dot_general(a, b, dn, precision=HI, preferred_element_type=F32)
    return lax.dot_general(a.astype(BF16), b.astype(BF16), dn, preferred_element_type=F32)


def _dot_nt(a, b, hi=False):
    dn = (((1,), (1,)), ((), ()))
    if hi:
        return lax.dot_general(a, b, dn, precision=HI, preferred_element_type=F32)
    return lax.dot_general(a.astype(BF16), b.astype(BF16), dn, preferred_element_type=F32)


def _dot_tn(a, b):
    dn = (((0,), (0,)), ((), ()))
    return lax.dot_general(a.astype(BF16), b.astype(BF16), dn, preferred_element_type=F32)


def _silu(x):
    return x * jax.nn.sigmoid(x)


def _softplus(x):
    return jnp.maximum(x, 0.0) + jnp.log1p(jnp.exp(-jnp.abs(x)))


def _rms(x, g):
    return x * lax.rsqrt(jnp.mean(x * x, axis=-1, keepdims=True) + RMS_EPS) * g


def _adaln(x, g, m):
    return _rms(x, g) * (1.0 + m[1:2]) + m[0:1]


def _mod_kernel(c_ref, w_ref, b_ref, o_ref):
    c = c_ref[...]
    o_ref[0] = _dot(_silu(c), w_ref[0]) + b_ref[0]


def _modulation(c, w_mod, b_mod):
    depth, d, n = w_mod.shape
    bsz = c.shape[0]
    rows = 8
    cp = jnp.pad(c, ((0, rows - bsz), (0, 0)))
    tn = 1024
    out = pl.pallas_call(
        _mod_kernel,
        grid=(depth, n // tn),
        in_specs=[pl.BlockSpec((rows, d), lambda l, j: (0, 0)),
                  pl.BlockSpec((1, d, tn), lambda l, j: (l, 0, j)),
                  pl.BlockSpec((1, 1, tn), lambda l, j: (l, 0, j))],
        out_specs=pl.BlockSpec((1, rows, tn), lambda l, j: (l, 0, j)),
        out_shape=jax.ShapeDtypeStruct((depth, rows, n), F32),
        compiler_params=_params(("arbitrary", "arbitrary")),
        name="modulation",
    )(cp, w_mod, b_mod.reshape(depth, 1, n))
    return out[:, :bsz]


def _ffn_kernel(x_ref, m_ref, g_ref, w1_ref, w3_ref, w2_ref, *rest, final):
    if final:
        fg_ref, o_ref, h_scr = rest
    else:
        o_ref, h_scr = rest
    j = pl.program_id(1)
    tm = x_ref.shape[0]
    rc = min(256, tm)

    @pl.when(j == 0)
    def _():
        for r0 in range(0, tm, rc):
            h_scr[r0:r0 + rc, :] = _adaln(x_ref[r0:r0 + rc, :], g_ref[...], m_ref[0]).astype(BF16)
        o_ref[...] = jnp.zeros_like(o_ref)

    h = h_scr[...]
    a = _dot(h, w1_ref[...])
    b = _dot(h, w3_ref[...])
    o_ref[...] += _dot(_silu(a) * b, w2_ref[...])

    @pl.when(j == pl.num_programs(1) - 1)
    def _():
        for r0 in range(0, tm, rc):
            o = x_ref[r0:r0 + rc, :] + (0.5 * m_ref[0][2:3]) * o_ref[r0:r0 + rc, :]
            if final:
                o = _rms(o, fg_ref[...])
            o_ref[r0:r0 + rc, :] = o


def _ffn(x, m, g, w1, w3, w2, layer, which, seqlen, final_g=None):
    t, d = x.shape
    f = w1.shape[-1]
    tm, tf = min(1024, seqlen), 256
    per = seqlen // tm
    final = final_g is not None
    in_specs = [pl.BlockSpec((tm, d), lambda i, j: (i, 0), pipeline_mode=pl.Buffered(1)),
                pl.BlockSpec((1, 3, d), lambda i, j: (i // per, 0, 0)),
                pl.BlockSpec((1, d), lambda i, j: (0, 0)),
                pl.BlockSpec((None, None, d, tf), lambda i, j: (layer, which, 0, j)),
                pl.BlockSpec((None, None, d, tf), lambda i, j: (layer, which, 0, j)),
                pl.BlockSpec((None, None, tf, d), lambda i, j: (layer, which, j, 0))]
    args = [x, m, g.reshape(1, d), w1, w3, w2]
    if final:
        in_specs.append(pl.BlockSpec((1, d), lambda i, j: (0, 0)))
        args.append(final_g.reshape(1, d))
    return pl.pallas_call(
        functools.partial(_ffn_kernel, final=final),
        grid=(t // tm, f // tf),
        in_specs=in_specs,
        out_specs=pl.BlockSpec((tm, d), lambda i, j: (i, 0)),
        out_shape=jax.ShapeDtypeStruct((t, d), F32),
        scratch_shapes=[pltpu.VMEM((tm, d), BF16)],
        compiler_params=_params(("parallel", "arbitrary")),
        name="ffn",
    )(*args)


def _adaln_kernel(x_ref, m_ref, g_ref, o_ref):
    o_ref[...] = _adaln(x_ref[...], g_ref[...], m_ref[0]).astype(o_ref.dtype)


def _adaln_call(x, m, g, seqlen, out_dtype):
    t, d = x.shape
    tm = min(512, seqlen)
    per = seqlen // tm
    return pl.pallas_call(
        _adaln_kernel,
        grid=(t // tm,),
        in_specs=[pl.BlockSpec((tm, d), lambda i: (i, 0)),
                  pl.BlockSpec((1, 3, d), lambda i: (i // per, 0, 0)),
                  pl.BlockSpec((1, d), lambda i: (0, 0))],
        out_specs=pl.BlockSpec((tm, d), lambda i: (i, 0)),
        out_shape=jax.ShapeDtypeStruct((t, d), out_dtype),
        compiler_params=_params(("parallel",)),
        name="adaln",
    )(x, m, g.reshape(1, d))


def _matmul_kernel(a_ref, w_ref, o_ref, w_scr, *, valid):
    @pl.when(pl.program_id(1) == 0)
    def _():
        w = w_ref[...]
        if valid < w.shape[1]:
            w = jnp.where(lax.broadcasted_iota(jnp.int32, w.shape, 1) < valid, w, 0.0)
        w_scr[...] = w.astype(BF16)

    o_ref[...] = _dot(a_ref[...], w_scr[...]).astype(o_ref.dtype)


def _matmul(a, w, col0=0, n=None, out_dtype=F32):
    m, k = a.shape
    n = w.shape[1] - col0 if n is None else n
    n_pad = -(-n // LANES) * LANES
    tm = min(1024, m)
    tn = min(1024, n_pad)
    assert col0 % tn == 0 and n_pad % tn == 0
    cb = col0 // tn
    return pl.pallas_call(
        functools.partial(_matmul_kernel, valid=min(tn, n)),
        grid=(n_pad // tn, m // tm),
        in_specs=[pl.BlockSpec((tm, k), lambda j, i: (i, 0)),
                  pl.BlockSpec((k, tn), lambda j, i: (0, cb + j))],
        out_specs=pl.BlockSpec((tm, tn), lambda j, i: (i, j)),
        out_shape=jax.ShapeDtypeStruct((m, n_pad), out_dtype),
        scratch_shapes=[pltpu.VMEM((k, tn), BF16)],
        compiler_params=_params(("parallel", "arbitrary")),
        name="matmul",
    )(a, w)


def _matmul_resid_kernel(*refs, n_in):
    a_refs = refs[:n_in]
    w_refs = refs[n_in:2 * n_in]
    x_ref, m_ref, o_ref = refs[2 * n_in:2 * n_in + 3]
    w_scrs = refs[2 * n_in + 3:]

    @pl.when(pl.program_id(1) == 0)
    def _():
        for w_ref, w_scr in zip(w_refs, w_scrs):
            w_scr[...] = w_ref[...].astype(BF16)

    acc = _dot(a_refs[0][...], w_scrs[0][...])
    for a_ref, w_scr in zip(a_refs[1:], w_scrs[1:]):
        acc = acc + _dot(a_ref[...], w_scr[...])
    o_ref[...] = x_ref[...] + m_ref[0][2:3] * acc


def _matmul_resid(a_list, w, x, m, seqlen):
    t, d = x.shape
    tm, tn = min(512, seqlen), 1024
    per = seqlen // tm
    n_in = len(a_list)
    kb = w.shape[0] // n_in
    arrays, in_specs = [], []
    for a in a_list:
        arr, blk = a if isinstance(a, tuple) else (a, 0)
        arrays.append(arr)
        in_specs.append(pl.BlockSpec((tm, kb), lambda j, i, blk=blk: (i, blk)))
    in_specs += [pl.BlockSpec((kb, tn), lambda j, i, r=r: (r, j)) for r in range(n_in)]
    in_specs += [pl.BlockSpec((tm, tn), lambda j, i: (i, j)),
                 pl.BlockSpec((1, 3, tn), lambda j, i: (i // per, 0, j))]
    return pl.pallas_call(
        functools.partial(_matmul_resid_kernel, n_in=n_in),
        grid=(d // tn, t // tm),
        in_specs=in_specs,
        out_specs=pl.BlockSpec((tm, tn), lambda j, i: (i, j)),
        out_shape=jax.ShapeDtypeStruct((t, d), F32),
        scratch_shapes=[pltpu.VMEM((kb, tn), BF16)] * n_in,
        compiler_params=_params(("parallel", "arbitrary")),
        name="matmul_resid",
    )(*arrays, *([w] * n_in), x, m)


def _s5_prep_kernel(lre_ref, lim_ref, ldt_ref, btr_ref, bti_ref, cr_ref, ci_ref,
                    mt_ref, wsr_ref, wsi_ref, vtr_ref, vti_ref, ar_ref, ai_ref):
    q, gc, p = S5_CHUNK, S5_GROUP_CH, S5_STATE
    lr = jnp.minimum(lre_ref[0], -1e-4)
    li = lim_ref[0]
    dt = jnp.exp(ldt_ref[0])

    def lam_pow(mult):
        mag = jnp.exp(lr * dt * mult)
        ang = li * dt * mult
        return mag * jnp.cos(ang), mag * jnp.sin(ang)

    one = jnp.ones((1, 1), F32)
    lb_re, lb_im = lam_pow(one)
    den = lr * lr + li * li
    nr, ni = lb_re - 1.0, lb_im
    f_re = (nr * lr + ni * li) / den
    f_im = (ni * lr - nr * li) / den
    btr, bti = btr_ref[0], bti_ref[0]
    bb_re = f_re * btr - f_im * bti
    bb_im = f_re * bti + f_im * btr
    cr, ci = cr_ref[0], ci_ref[0]

    qc = q * gc
    row_i = lax.broadcasted_iota(jnp.int32, (qc, gc), 0)
    col_i = lax.broadcasted_iota(jnp.int32, (qc, gc), 1)
    rep = lambda a: _dot((row_i // gc == col_i).astype(F32), a, hi=True)
    tile = lambda a: _dot((row_i % gc == col_i).astype(F32), a, hi=True)
    tau = lax.broadcasted_iota(jnp.int32, (q, 1), 0).astype(F32)
    pr, pi = lam_pow(tau)
    pr, pi = rep(pr), rep(pi)
    cr_t, ci_t, br_t, bi_t = tile(cr), tile(ci), tile(bb_re), tile(bb_im)
    kw = (_dot_nt(pr * cr_t - pi * ci_t, br_t, hi=True)
          - _dot_nt(pr * ci_t + pi * cr_t, bi_t, hi=True))
    lane_blk = lax.broadcasted_iota(jnp.int32, (qc, qc), 1) // gc
    mt = jnp.where(lane_blk == 0, kw, 0.0)
    for j in range(1, q):
        sh = jnp.concatenate([jnp.zeros((j * gc, qc), F32), kw[:(q - j) * gc]], axis=0)
        mt = jnp.where(lane_blk == j, sh, mt)
    mt_ref[0] = mt.astype(BF16)

    pr_r, pi_r = lam_pow((q - 1) - tau)
    pr_r, pi_r = rep(pr_r), rep(pi_r)
    wsr_ref[0] = (pr_r * br_t - pi_r * bi_t).astype(BF16)
    wsi_ref[0] = (pr_r * bi_t + pi_r * br_t).astype(BF16)

    pr1, pi1 = lam_pow(tau + 1.0)
    pr1, pi1 = rep(pr1), rep(pi1)
    vtr_ref[0] = (pr1 * cr_t - pi1 * ci_t).astype(BF16)
    vti_ref[0] = (-(pr1 * ci_t + pi1 * cr_t)).astype(BF16)

    lvl = lax.broadcasted_iota(jnp.int32, (ar_ref.shape[1], 1), 0)
    step = lax.shift_left(jnp.full_like(lvl, q), lvl).astype(F32)
    a_re, a_im = lam_pow(step)
    ar_ref[0] = a_re
    ai_ref[0] = a_im


def _s5_core_kernel(x_ref, mt_ref, wsr_ref, wsi_ref, vtr_ref, vti_ref, ar_ref, ai_ref, o_ref, *, nc, levels):
    x = x_ref[0].astype(BF16)
    y = _dot_nt(x, mt_ref[0])
    sr = _dot(x, wsr_ref[0])
    si = _dot(x, wsi_ref[0])
    rows = x.shape[0]
    cidx = lax.broadcasted_iota(jnp.int32, (rows, 1), 0) % nc
    a_re, a_im = ar_ref[0], ai_ref[0]
    for k in range(levels):
        sh = 1 << k
        xr = pltpu.roll(sr, sh, axis=0)
        xi = pltpu.roll(si, sh, axis=0)
        ok = cidx >= sh
        ar, ai = a_re[k:k + 1], a_im[k:k + 1]
        sr, si = (sr + jnp.where(ok, ar * xr - ai * xi, 0.0),
                  si + jnp.where(ok, ar * xi + ai * xr, 0.0))
    ok = cidx >= 1
    pr = jnp.where(ok, pltpu.roll(sr, 1, axis=0), 0.0)
    pi = jnp.where(ok, pltpu.roll(si, 1, axis=0), 0.0)
    o_ref[0] = y + _dot_nt(pr, vtr_ref[0]) + _dot_nt(pi, vti_ref[0])


def _s5_post_kernel(y_ref, u_ref, d_ref, w_ref, b_ref, o_ref):
    y = y_ref[...] + d_ref[...] * u_ref[...]
    y = jax.nn.gelu(y)
    o_ref[...] = (y * jax.nn.sigmoid(_dot(y, w_ref[...]) + b_ref[...])).astype(o_ref.dtype)


def _s5_branch(u, bsz, seqlen, lam_re, lam_im, log_dt, b_re, b_im, c_re, c_im, d_skip, glu_w, glu_b):
    t, width = u.shape
    q, gc, p = S5_CHUNK, S5_GROUP_CH, S5_STATE
    groups = width // gc
    nc = seqlen // q
    rows = bsz * nc
    levels = int(math.log2(nc))
    assert (1 << levels) == nc
    lv_rows = 8
    qc = q * gc
    g3 = lambda a, b: pl.BlockSpec((1, a, b), lambda g: (g, 0, 0))
    prep = pl.pallas_call(
        _s5_prep_kernel,
        grid=(groups,),
        in_specs=[g3(1, p), g3(1, p), g3(1, p), g3(gc, p), g3(gc, p), g3(gc, p), g3(gc, p)],
        out_specs=[g3(qc, qc), g3(qc, p), g3(qc, p), g3(qc, p), g3(qc, p), g3(lv_rows, p), g3(lv_rows, p)],
        out_shape=[jax.ShapeDtypeStruct((groups, qc, qc), BF16)]
        + [jax.ShapeDtypeStruct((groups, qc, p), BF16)] * 4
        + [jax.ShapeDtypeStruct((groups, lv_rows, p), F32)] * 2,
        compiler_params=_params(("parallel",)),
        name="s5_prep",
    )(lam_re.reshape(groups, 1, p), lam_im.reshape(groups, 1, p),
      jnp.broadcast_to(log_dt[:, None, None], (groups, 1, p)),
      jnp.swapaxes(b_re, 1, 2), jnp.swapaxes(b_im, 1, 2), c_re, c_im)
    mt, wsr, wsi, vtr, vti, a_re, a_im = prep

    xg = u.reshape(bsz, nc, q, groups, gc).transpose(3, 0, 1, 2, 4).reshape(groups, rows, qc)
    yt = pl.pallas_call(
        functools.partial(_s5_core_kernel, nc=nc, levels=levels),
        grid=(groups,),
        in_specs=[g3(rows, qc), g3(qc, qc), g3(qc, p), g3(qc, p), g3(qc, p), g3(qc, p),
                  g3(lv_rows, p), g3(lv_rows, p)],
        out_specs=g3(rows, qc),
        out_shape=jax.ShapeDtypeStruct((groups, rows, qc), F32),
        compiler_params=_params(("parallel",)),
        name="s5_core",
    )(xg, mt, wsr, wsi, vtr, vti, a_re, a_im)
    y = yt.reshape(groups, bsz, nc, q, gc).transpose(1, 2, 3, 0, 4).reshape(t, width)

    tm = min(512, t)
    return pl.pallas_call(
        _s5_post_kernel,
        grid=(t // tm,),
        in_specs=[pl.BlockSpec((tm, width), lambda i: (i, 0)),
                  pl.BlockSpec((tm, width), lambda i: (i, 0)),
                  pl.BlockSpec((1, width), lambda i: (0, 0)),
                  pl.BlockSpec((width, width), lambda i: (0, 0)),
                  pl.BlockSpec((1, width), lambda i: (0, 0))],
        out_specs=pl.BlockSpec((tm, width), lambda i: (i, 0)),
        out_shape=jax.ShapeDtypeStruct((t, width), BF16),
        compiler_params=_params(("parallel",)),
        name="s5_post",
    )(y, u, d_skip.reshape(1, width), glu_w.astype(BF16), glu_b.reshape(1, width))


def _ssd_kernel(z_ref, xbc_ref, dt_ref, cw_ref, cb_ref, dtb_ref, alog_ref, dsk_ref, ng_ref, exp_ref,
                o_ref, ext_scr, st_scr, *, heads, inner):
    lc = SSD_CHUNK
    hd, ns = SSD_HEAD_DIM, SSD_STATE
    rpg = heads // SSD_GROUPS
    gw = SSD_GROUPS * ns
    tail = SSD_CONV - 1

    @pl.when(pl.program_id(1) == 0)
    def _():
        ext_scr[0:8, :] = jnp.zeros((8, ext_scr.shape[1]), F32)
        st_scr[...] = jnp.zeros_like(st_scr)

    ext_scr[8:8 + lc, :] = xbc_ref[...]
    cw = cw_ref[...]
    conv = cb_ref[...] + cw[0:1] * ext_scr[pl.ds(8 - tail, lc), :]
    for k in range(1, SSD_CONV):
        conv = conv + cw[k:k + 1] * ext_scr[pl.ds(8 - tail + k, lc), :]
    ext_scr[0:8, :] = ext_scr[lc:lc + 8, :]
    act = _silu(conv)
    xs = act[:, :inner]
    bs = act[:, inner:inner + gw]
    cs = act[:, inner + gw:]

    dt = _softplus(dt_ref[...] + dtb_ref[...])
    adt = dt * (-jnp.exp(alog_ref[...]))
    ri = lax.broadcasted_iota(jnp.int32, (lc, lc), 0)
    ci = lax.broadcasted_iota(jnp.int32, (lc, lc), 1)
    causal = ri >= ci
    a_cum = _dot(causal.astype(F32), adt, hi=True)
    a_cum_t = a_cum.T
    expand = exp_ref[...]
    xq = xs * _dot(dt, expand, hi=True)
    ea = _dot(jnp.exp(a_cum), expand, hi=True)

    y_parts = []
    for g in range(SSD_GROUPS):
        cs_g = cs[:, g * ns:(g + 1) * ns]
        bs_g = bs[:, g * ns:(g + 1) * ns]
        cb = _dot_nt(cs_g, bs_g)
        bs_gt = bs_g.T
        for r in range(rpg):
            h = g * rpg + r
            col = a_cum[:, h:h + 1]
            row = a_cum_t[h:h + 1, :]
            decay = jnp.exp(jnp.where(causal, col - row, -jnp.inf))
            xq_h = xq[:, h * hd:(h + 1) * hd]
            st = st_scr[h]
            y_h = _dot(cb * decay, xq_h)
            y_parts.append((y_h, _dot(cs_g, st)))
            last = row[:, lc - 1:lc]
            to_end = jnp.exp(last - row)
            st_scr[h] = jnp.exp(last) * st + _dot(bs_gt * to_end, xq_h)
    y_diag = jnp.concatenate([a for a, _ in y_parts], axis=1)
    y_off = jnp.concatenate([b for _, b in y_parts], axis=1)
    y = y_diag + y_off * ea + dsk_ref[...] * xs
    y = y * _silu(z_ref[...])
    o_ref[...] = _rms(y, ng_ref[...]).astype(o_ref.dtype)


def _ssd_branch(z, xbc, dt_raw, bsz, seqlen, conv_w, conv_b, dt_bias, a_log, d_skip, norm_g):
    t, inner = z.shape
    cd = xbc.shape[1]
    heads = inner // SSD_HEAD_DIM
    lc = SSD_CHUNK
    nc = seqlen // lc
    pad = LANES - heads
    padv = lambda v: jnp.pad(v.reshape(1, heads), ((0, 0), (0, pad)))
    expand = jnp.repeat(jnp.eye(LANES, heads, dtype=F32), SSD_HEAD_DIM, axis=1)
    row = lambda n: pl.BlockSpec((1, n), lambda b, c: (0, 0))
    tok = lambda n: pl.BlockSpec((lc, n), lambda b, c: (b * nc + c, 0))
    return pl.pallas_call(
        functools.partial(_ssd_kernel, heads=heads, inner=inner),
        grid=(bsz, nc),
        in_specs=[tok(inner), tok(cd), tok(LANES),
                  pl.BlockSpec((SSD_CONV, cd), lambda b, c: (0, 0)), row(cd),
                  row(LANES), row(LANES), row(inner), row(inner),
                  pl.BlockSpec((LANES, inner), lambda b, c: (0, 0))],
        out_specs=tok(inner),
        out_shape=jax.ShapeDtypeStruct((t, inner), BF16),
        scratch_shapes=[pltpu.VMEM((8 + lc + 8, cd), F32),
                        pltpu.VMEM((heads, SSD_STATE, SSD_HEAD_DIM), F32)],
        compiler_params=_params(("parallel", "arbitrary")),
        name="ssd",
    )(z, xbc, dt_raw, conv_w, conv_b.reshape(1, cd), padv(dt_bias), padv(a_log),
      jnp.repeat(d_skip, SSD_HEAD_DIM).reshape(1, inner), norm_g.reshape(1, inner), expand)


def _hybrid_mixer(x, m, g, seqlen, w_in, w_out, lam_re, lam_im, log_dt, b_re, b_im, c_re, c_im, s5_d,
                  glu_w, glu_b, conv_w, conv_b, dt_bias, a_log, ssd_d, ssd_norm_g):
    t, d = x.shape
    bsz = t // seqlen
    s5w = s5_d.shape[0]
    inner = ssd_norm_g.shape[0]
    cd = conv_w.shape[1]
    heads = dt_bias.shape[0]
    o1, o2, o3 = s5w, s5w + inner, s5w + inner + cd
    h = _adaln_call(x, m, g, seqlen, BF16)
    u = _matmul(h, w_in, 0, o1)
    z = _matmul(h, w_in, o1, inner)
    xbc = _matmul(h, w_in, o2, cd)
    dt_raw = _matmul(h, w_in, o3, heads)
    y_s5 = _s5_branch(u, bsz, seqlen, lam_re, lam_im, log_dt, b_re, b_im, c_re, c_im, s5_d, glu_w, glu_b)
    y_ssd = _ssd_branch(z, xbc, dt_raw, bsz, seqlen, conv_w, conv_b, dt_bias, a_log, ssd_d, ssd_norm_g)
    assert inner % s5w == 0
    return _matmul_resid([y_s5] + [(y_ssd, i) for i in range(inner // s5w)], w_out, x, m, seqlen)


def _rwkv_mix_kernel(x_ref, m_ref, g_ref, mu_ref, *rest):
    o_refs, prev_scr = rest[:6], rest[6]
    h = _adaln(x_ref[...], g_ref[...], m_ref[0])
    tm = h.shape[0]

    @pl.when(pl.program_id(1) == 0)
    def _():
        prev_scr[...] = jnp.zeros_like(prev_scr)

    first = lax.broadcasted_iota(jnp.int32, (tm, 1), 0) == 0
    shifted = jnp.where(first, prev_scr[7:8, :], pltpu.roll(h, 1, axis=0))
    prev_scr[...] = h[tm - 8:tm]
    xx = shifted - h
    mu = mu_ref[...]
    for i, o_ref in enumerate(o_refs):
        o_ref[...] = (h + xx * mu[i:i + 1]).astype(o_ref.dtype)


def _rwkv_mix(x, m, g, mu, bsz, seqlen):
    t, d = x.shape
    tm = min(256, seqlen)
    per = seqlen // tm
    tok = pl.BlockSpec((tm, d), lambda b, i: (b * per + i, 0))
    return pl.pallas_call(
        _rwkv_mix_kernel,
        grid=(bsz, per),
        in_specs=[tok, pl.BlockSpec((1, 3, d), lambda b, i: (b, 0, 0)),
                  pl.BlockSpec((1, d), lambda b, i: (0, 0)),
                  pl.BlockSpec((8, d), lambda b, i: (0, 0))],
        out_specs=[tok] * 6,
        out_shape=[jax.ShapeDtypeStruct((t, d), BF16)] * 6,
        scratch_shapes=[pltpu.VMEM((8, d), F32)],
        compiler_params=_params(("parallel", "arbitrary")),
        name="rwkv_mix",
    )(x, m, g.reshape(1, d), jnp.pad(mu, ((0, 2), (0, 0))))


def _lora_kernel(x_ref, a_ref, b_ref, bias_ref, o_ref, *, mode):
    mid = _dot(x_ref[...], a_ref[...])
    if mode == "decay":
        mid = jnp.tanh(mid)
    elif mode == "gate":
        mid = jax.nn.sigmoid(mid)
    out = _dot(mid, b_ref[...])
    if mode == "decay":
        w = -_softplus(-(bias_ref[...] + out)) - 0.5
        out = -jnp.exp(w)
    elif mode == "icl":
        out = jax.nn.sigmoid(bias_ref[...] + out)
    o_ref[...] = out


def _lora(x, a, b, bias, mode):
    t, d = x.shape
    rank = a.shape[1]
    rp = -(-rank // LANES) * LANES
    ap = jnp.pad(a, ((0, 0), (0, rp - rank))).astype(BF16)
    bp = jnp.pad(b, ((0, rp - rank), (0, 0))).astype(BF16)
    tm = min(512, t)
    return pl.pallas_call(
        functools.partial(_lora_kernel, mode=mode),
        grid=(t // tm,),
        in_specs=[pl.BlockSpec((tm, d), lambda i: (i, 0)),
                  pl.BlockSpec((d, rp), lambda i: (0, 0)),
                  pl.BlockSpec((rp, d), lambda i: (0, 0)),
                  pl.BlockSpec((1, d), lambda i: (0, 0))],
        out_specs=pl.BlockSpec((tm, d), lambda i: (i, 0)),
        out_shape=jax.ShapeDtypeStruct((t, d), F32),
        compiler_params=_params(("parallel",)),
        name="rwkv_lora_" + mode,
    )(x, ap, bp, bias.reshape(1, d))


def _rwkv_scan_kernel(r_ref, k_ref, v_ref, lw_ref, a_ref, g_ref, kk_ref, ka_ref, rk_ref, lg_ref, lb_ref,
                      o_ref, s_scr):
    c = RWKV_CHUNK
    hd = RWKV_HEAD_DIM
    c2 = 2 * c
    n_chunks = r_ref.shape[0] // c

    @pl.when(pl.program_id(2) == 0)
    def _():
        s_scr[...] = jnp.zeros_like(s_scr)

    lane_head = lax.broadcasted_iota(jnp.int32, (1, LANES), 1) // hd
    head0 = lane_head == 0
    same_head = ((lax.broadcasted_iota(jnp.int32, (LANES, LANES), 0) // hd)
                 == (lax.broadcasted_iota(jnp.int32, (LANES, LANES), 1) // hd))
    ones_bd = same_head.astype(F32)
    tb = r_ref.shape[0]
    tg = min(tb, 2 * LANES)
    ri = lax.broadcasted_iota(jnp.int32, (tg, tg), 0)
    ci = lax.broadcasted_iota(jnp.int32, (tg, tg), 1)
    tri_b16 = ((ri // c == ci // c) & (ri >= ci)).astype(BF16)
    r2 = lax.broadcasted_iota(jnp.int32, (1, c2, c2), 1)
    q2 = lax.broadcasted_iota(jnp.int32, (1, c2, c2), 2)
    same_blk = (r2 // c) == (q2 // c)
    strict = same_blk & ((r2 % c) > (q2 % c))
    incl = same_blk & ((r2 % c) >= (q2 % c))
    eye2 = (r2 == q2).astype(F32)
    k_k, k_a, r_k = kk_ref[...], ka_ref[...], rk_ref[...]
    ln_g, ln_b = lg_ref[...], lb_ref[...]
    ones_b16 = ones_bd.astype(BF16)

    def stack(x):
        x = x.reshape(n_chunks, c, LANES)
        return jnp.concatenate([jnp.where(head0, x, 0.0), jnp.where(head0, 0.0, x)], axis=1)

    def unstack(x):
        return (x[:, :c] + x[:, c:]).reshape(tb, LANES)

    def bdot(p, q, dims):
        return lax.dot_general(p.astype(BF16), q.astype(BF16), (dims, ((0,), (0,))), preferred_element_type=F32)

    bmm = lambda p, q: bdot(p, q, ((2,), (1,)))
    bmm_nt = lambda p, q: bdot(p, q, ((2,), (2,)))
    bmm_tn = lambda p, q: bdot(p, q, ((1,), (1,)))

    def split(x):
        hi_part = x.astype(BF16)
        return hi_part, (x - hi_part.astype(F32)).astype(BF16)

    ones2 = jnp.concatenate([ones_b16, ones_b16], axis=0)

    def head_sum(x):
        return jnp.dot(jnp.concatenate(split(x), axis=1), ones2, preferred_element_type=F32)

    def cumsum_rows(x):
        outs = []
        for g0 in range(0, tb, tg):
            xg = x[g0:g0 + tg]
            p1, p2 = split(xg)
            p3 =(xg - p1.astype(F32) - p2.astype(F32)).astype(BF16)
            wide = jnp.dot(tri_b16, jnp.concatenate([p1, p2], axis=1), preferred_element_type=F32)
            outs.append(wide[:, :LANES] + wide[:, LANES:] + jnp.dot(tri_b16, p3, preferred_element_type=F32))
        return outs[0] if len(outs) == 1 else jnp.concatenate(outs, axis=0)

    def lhs3(hi_part, lo_part):
        return jnp.concatenate([hi_part, lo_part], axis=2)

    def rhs3(hi_part, lo_part):
        top = jnp.concatenate([hi_part, lo_part], axis=2)
        bot = jnp.concatenate([hi_part, jnp.zeros_like(lo_part)], axis=2)
        return jnp.concatenate([top, bot], axis=1)

    def fold(x):
        return x[:, :, :c2] + x[:, :, c2:]

    r, k, v, lw, a = r_ref[...], k_ref[...], v_ref[...], lw_ref[...], a_ref[...]
    kk = k * k_k
    kk = kk / jnp.maximum(jnp.sqrt(head_sum(kk * kk)), 1e-12)
    k2 = k * (1.0 + (a - 1.0) * k_a)
    bv = kk * a
    cum = cumsum_rows(lw)
    cum3 = cum.reshape(n_chunks, c, LANES)
    tot = jnp.broadcast_to(cum3[:, c - 1:c, :], cum3.shape).reshape(tb, LANES)
    g_inv = jnp.exp(-cum)
    rt_u = r * jnp.exp(cum)
    at = stack(-kk * jnp.exp(cum - lw))
    rt = stack(rt_u)
    bt = stack(bv * g_inv)
    kt = stack(k2 * g_inv)
    to_end = jnp.exp(tot - cum)
    b_end = (bv * to_end).reshape(n_chunks, c, LANES)
    k_end = (k2 * to_end).reshape(n_chunks, c, LANES)
    vs = stack(v)

    gram = bmm_nt(jnp.concatenate([at, rt], axis=1), jnp.concatenate([bt, kt], axis=1))
    n_ab = jnp.where(strict, gram[:, :c2, :c2], 0.0)
    a_ak = jnp.where(strict, gram[:, :c2, c2:], 0.0)
    a_rb = jnp.where(incl, gram[:, c2:, :c2], 0.0)
    a_rk = jnp.where(incl, gram[:, c2:, c2:], 0.0)
    pw_h, pw_l = split(n_ab)
    pw = fold(bmm(lhs3(pw_h, pw_l), rhs3(pw_h, pw_l)))
    tinv = eye2 + n_ab
    for _ in range(int(math.log2(c)) - 2):
        pw_h, pw_l = split(pw)
        t_h, t_l = split(tinv)
        both = fold(bmm(jnp.concatenate([lhs3(pw_h, pw_l), lhs3(t_h, t_l)], axis=1), rhs3(pw_h, pw_l)))
        pw = both[:, :c2]
        tinv = tinv + both[:, c2:]
    pw_h, pw_l = split(pw)
    t_h, t_l = split(tinv)
    tinv = tinv + fold(bmm(lhs3(t_h, t_l), rhs3(pw_h, pw_l)))
    pq = bmm(tinv, jnp.concatenate([at, bmm(a_ak, vs)], axis=2))
    ry = bmm(jnp.concatenate([a_rb, a_rk], axis=2),
             jnp.concatenate([pq, jnp.concatenate([jnp.zeros_like(vs), vs], axis=2)], axis=1))
    rq = rt_u + unstack(ry[:, :, :LANES])
    y0 = unstack(ry[:, :, LANES:])
    p_u = unstack(pq[:, :, :LANES]).reshape(n_chunks, c, LANES)
    q_u = unstack(pq[:, :, LANES:]).reshape(n_chunks, c, LANES)
    gm = jnp.where(same_head, bmm_tn(p_u, b_end), 0.0)
    dm = jnp.where(same_head, bmm_tn(jnp.concatenate([q_u, v.reshape(n_chunks, c, LANES)], axis=1),
                                     jnp.concatenate([b_end, k_end], axis=1)), 0.0)
    g_tot = jnp.exp(cum3[:, c - 1:c, :])
    bonus = head_sum(r * k2 * r_k) * v

    s = s_scr[...]
    ys = []
    for i in range(n_chunks):
        ys.append(_dot_nt(rq[i * c:(i + 1) * c], s))
        s = s * g_tot[i] + _dot(s, gm[i]) + dm[i]
    s_scr[...] = s
    y = jnp.concatenate(ys, axis=0) + y0
    mean = head_sum(y) * (1.0 / hd)
    dy = y - mean
    var = head_sum(dy * dy) * (1.0 / hd)
    yn = dy * lax.rsqrt(var + RWKV_GN_EPS) * ln_g + ln_b
    o_ref[...] = ((yn + bonus) * g_ref[...]).astype(o_ref.dtype)


def _rwkv_scan(r, k, v, lw, a, g, k_k, k_a, r_k, ln_g, ln_b, bsz, seqlen):
    t, d = r.shape
    tb = min(RWKV_TOKEN_BLOCK, seqlen)
    per = seqlen // tb
    tok = pl.BlockSpec((tb, LANES), lambda b, h, i: (b * per + i, h))
    row = pl.BlockSpec((1, LANES), lambda b, h, i: (0, h))
    vec = lambda p: p.reshape(1, d)
    return pl.pallas_call(
        _rwkv_scan_kernel,
        grid=(bsz, d // LANES, per),
        in_specs=[tok] * 6 + [row] * 5,
        out_specs=tok,
        out_shape=jax.ShapeDtypeStruct((t, d), BF16),
        scratch_shapes=[pltpu.VMEM((LANES, LANES), F32)],
        compiler_params=_params(("parallel", "parallel", "arbitrary")),
        name="rwkv_scan",
    )(r, k, v, lw, a, g, vec(k_k), vec(k_a), vec(r_k), vec(ln_g), vec(ln_b))


def _rwkv_mixer(x, m, g, seqlen, mu, w_r, w_k, w_v, w_o, w0, w1, w2, a0, a1, a2, g1, g2,
                k_k, k_a, r_k, ln_g, ln_b):
    t, d = x.shape
    bsz = t // seqlen
    xr, xw, xk, xv, xa, xg = _rwkv_mix(x, m, g, mu, bsz, seqlen)
    r = _matmul(xr, w_r)
    k = _matmul(xk, w_k)
    v = _matmul(xv, w_v)
    lw = _lora(xw, w1, w2, w0, "decay")
    a = _lora(xa, a1, a2, a0, "icl")
    gate = _lora(xg, g1, g2, jnp.zeros_like(w0), "gate")
    y = _rwkv_scan(r, k, v, lw, a, gate, k_k, k_a, r_k.reshape(-1), ln_g, ln_b, bsz, seqlen)
    return _matmul_resid([y], w_o, x, m, seqlen)


def kernel(x, c, w_mod, b_mod, norm_g, ffn_w1, ffn_w3, ffn_w2, hyb_w_in, hyb_w_out, s5_lambda_re, s5_lambda_im, s5_log_dt, s5_b_re, s5_b_im, s5_c_re, s5_c_im, s5_d, s5_glu_w, s5_glu_b, ssd_conv_w, ssd_conv_b, ssd_dt_bias, ssd_a_log, ssd_d, ssd_norm_g, rwkv_mu, rwkv_w_r, rwkv_w_k, rwkv_w_v, rwkv_w_o, rwkv_w0, rwkv_w1, rwkv_w2, rwkv_a0, rwkv_a1, rwkv_a2, rwkv_g1, rwkv_g2, rwkv_k_k, rwkv_k_a, rwkv_r_k, rwkv_ln_g, rwkv_ln_b, final_g):
    bsz, seqlen, d = x.shape
    depth = w_mod.shape[0]
    xf = x.reshape(bsz * seqlen, d)
    mod = _modulation(c, w_mod, b_mod).reshape(depth, bsz, 3, 3, d)
    for layer in range(depth):
        i = layer // 2
        m0, m1, m2 = mod[layer, :, 0], mod[layer, :, 1], mod[layer, :, 2]
        xf = _ffn(xf, m0, norm_g[layer, 0], ffn_w1, ffn_w3, ffn_w2, layer, 0, seqlen)
        if layer % 2 == 0:
            xf = _hybrid_mixer(xf, m1, norm_g[layer, 1], seqlen, hyb_w_in[i], hyb_w_out[i], s5_lambda_re[i],
                               s5_lambda_im[i], s5_log_dt[i], s5_b_re[i], s5_b_im[i], s5_c_re[i], s5_c_im[i],
                               s5_d[i], s5_glu_w[i], s5_glu_b[i], ssd_conv_w[i], ssd_conv_b[i], ssd_dt_bias[i],
                               ssd_a_log[i], ssd_d[i], ssd_norm_g[i])
        else:
            xf = _rwkv_mixer(xf, m1, norm_g[layer, 1], seqlen, rwkv_mu[i], rwkv_w_r[i], rwkv_w_k[i], rwkv_w_v[i],
                             rwkv_w_o[i], rwkv_w0[i], rwkv_w1[i], rwkv_w2[i], rwkv_a0[i], rwkv_a1[i], rwkv_a2[i],
                             rwkv_g1[i], rwkv_g2[i], rwkv_k_k[i], rwkv_k_a[i], rwkv_r_k[i], rwkv_ln_g[i],
                             rwkv_ln_b[i])
        xf = _ffn(xf, m2, norm_g[layer, 2], ffn_w1, ffn_w3, ffn_w2, layer, 1, seqlen,
                  final_g=final_g if layer == depth - 1 else None)
    return xf.reshape(bsz, seqlen, d)
```

```python
import functools
import math

import jax
import jax.numpy as jnp
from jax import lax
from jax.experimental import pallas as pl
from jax.experimental.pallas import tpu as pltpu

F32 = jnp.float32
BF16 = jnp.bfloat16
HI = lax.Precision.HIGHEST

RMS_EPS = 1e-6
RWKV_GN_EPS = 64e-5

S5_GROUP_CH = 16
S5_STATE = 64
S5_CHUNK = 16
SSD_HEAD_DIM = 64
SSD_STATE = 128
SSD_GROUPS = 8
SSD_CHUNK = 128
SSD_CONV = 4
RWKV_HEAD_DIM = 64
RWKV_CHUNK = 64
RWKV_TOKEN_BLOCK = 512
LANES = 128
VMEM_LIMIT = 56 * 1024 * 1024

def _params(sem, vmem=VMEM_LIMIT):
    return pltpu.CompilerParams(dimension_semantics=sem, vmem_limit_bytes=vmem)


def _dot(a, b, hi=False):
    dn = (((1,), (0,)), ((), ()))
    if hi:
        return lax.dot_general(a, b, dn, precision=HI, preferred_element_type=F32)
    return lax.dot_general(a.astype(BF16), b.astype(BF16), dn, preferred_element_type=F32)


def _dot_nt(a, b, hi=False):
    dn = (((1,), (1,)), ((), ()))
    if hi:
        return lax.dot_general(a, b, dn, precision=HI, preferred_element_type=F32)
    return lax.dot_general(a.astype(BF16), b.astype(BF16), dn, preferred_element_type=F32)


def _dot_tn(a, b):
    dn = (((0,), (0,)), ((), ()))
    return lax.dot_general(a.astype(BF16), b.astype(BF16), dn, preferred_element_type=F32)


def _silu(x):
    return x * jax.nn.sigmoid(x)


def _softplus(x):
    return jnp.maximum(x, 0.0) + jnp.log1p(jnp.exp(-jnp.abs(x)))


def _rms(x, g):
    return x * lax.rsqrt(jnp.mean(x * x, axis=-1, keepdims=True) + RMS_EPS) * g


def _adaln(x, g, m):
    return _rms(x, g) * (1.0 + m[1:2]) + m[0:1]


def _mod_kernel(c_ref, w_ref, b_ref, o_ref):
    c = c_ref[...]
    o_ref[0] = _dot(_silu(c), w_ref[0]) + b_ref[0]


def _modulation(c, w_mod, b_mod):
    depth, d, n = w_mod.shape
    bsz = c.shape[0]
    rows = 8
    cp = jnp.pad(c, ((0, rows - bsz), (0, 0)))
    tn = 1024
    out = pl.pallas_call(
        _mod_kernel,
        grid=(depth, n // tn),
        in_specs=[pl.BlockSpec((rows, d), lambda l, j: (0, 0)),
                  pl.BlockSpec((1, d, tn), lambda l, j: (l, 0, j)),
                  pl.BlockSpec((1, 1, tn), lambda l, j: (l, 0, j))],
        out_specs=pl.BlockSpec((1, rows, tn), lambda l, j: (l, 0, j)),
        out_shape=jax.ShapeDtypeStruct((depth, rows, n), F32),
        compiler_params=_params(("arbitrary", "arbitrary")),
        name="modulation",
    )(cp, w_mod, b_mod.reshape(depth, 1, n))
    return out[:, :bsz]


def _ffn_kernel(x_ref, m_ref, g_ref, w1_ref, w3_ref, w2_ref, *rest, final):
    if final:
        fg_ref, o_ref, h_scr = rest
    else:
        o_ref, h_scr = rest
    j = pl.program_id(1)
    tm = x_ref.shape[0]
    rc = min(256, tm)

    @pl.when(j == 0)
    def _():
        for r0 in range(0, tm, rc):
            h_scr[r0:r0 + rc, :] = _adaln(x_ref[r0:r0 + rc, :], g_ref[...], m_ref[0]).astype(BF16)
        o_ref[...] = jnp.zeros_like(o_ref)

    h = h_scr[...]
    a = _dot(h, w1_ref[...])
    b = _dot(h, w3_ref[...])
    o_ref[...] += _dot(_silu(a) * b, w2_ref[...])

    @pl.when(j == pl.num_programs(1) - 1)
    def _():
        for r0 in range(0, tm, rc):
            o = x_ref[r0:r0 + rc, :] + (0.5 * m_ref[0][2:3]) * o_ref[r0:r0 + rc, :]
            if final:
                o = _rms(o, fg_ref[...])
            o_ref[r0:r0 + rc, :] = o


def _ffn(x, m, g, w1, w3, w2, layer, which, seqlen, final_g=None):
    t, d = x.shape
    f = w1.shape[-1]
    tm, tf = min(1024, seqlen), 256
    per = seqlen // tm
    final = final_g is not None
    in_specs = [pl.BlockSpec((tm, d), lambda i, j: (i, 0), pipeline_mode=pl.Buffered(1)),
                pl.BlockSpec((1, 3, d), lambda i, j: (i // per, 0, 0)),
                pl.BlockSpec((1, d), lambda i, j: (0, 0)),
                pl.BlockSpec((None, None, d, tf), lambda i, j: (layer, which, 0, j)),
                pl.BlockSpec((None, None, d, tf), lambda i, j: (layer, which, 0, j)),
                pl.BlockSpec((None, None, tf, d), lambda i, j: (layer, which, j, 0))]
    args = [x, m, g.reshape(1, d), w1, w3, w2]
    if final:
        in_specs.append(pl.BlockSpec((1, d), lambda i, j: (0, 0)))
        args.append(final_g.reshape(1, d))
    return pl.pallas_call(
        functools.partial(_ffn_kernel, final=final),
        grid=(t // tm, f // tf),
        in_specs=in_specs,
        out_specs=pl.BlockSpec((tm, d), lambda i, j: (i, 0)),
        out_shape=jax.ShapeDtypeStruct((t, d), F32),
        scratch_shapes=[pltpu.VMEM((tm, d), BF16)],
        compiler_params=_params(("parallel", "arbitrary")),
        name="ffn",
    )(*args)


def _adaln_kernel(x_ref, m_ref, g_ref, o_ref):
    o_ref[...] = _adaln(x_ref[...], g_ref[...], m_ref[0]).astype(o_ref.dtype)


def _adaln_call(x, m, g, seqlen, out_dtype):
    t, d = x.shape
    tm = min(512, seqlen)
    per = seqlen // tm
    return pl.pallas_call(
        _adaln_kernel,
        grid=(t // tm,),
        in_specs=[pl.BlockSpec((tm, d), lambda i: (i, 0)),
                  pl.BlockSpec((1, 3, d), lambda i: (i // per, 0, 0)),
                  pl.BlockSpec((1, d), lambda i: (0, 0))],
        out_specs=pl.BlockSpec((tm, d), lambda i: (i, 0)),
        out_shape=jax.ShapeDtypeStruct((t, d), out_dtype),
        compiler_params=_params(("parallel",)),
        name="adaln",
    )(x, m, g.reshape(1, d))


def _matmul_kernel(a_ref, w_ref, o_ref, w_scr, *, valid):
    @pl.when(pl.program_id(1) == 0)
    def _():
        w = w_ref[...]
        if valid < w.shape[1]:
            w = jnp.where(lax.broadcasted_iota(jnp.int32, w.shape, 1) < valid, w, 0.0)
        w_scr[...] = w.astype(BF16)

    o_ref[...] = _dot(a_ref[...], w_scr[...]).astype(o_ref.dtype)


def _matmul(a, w, col0=0, n=None, out_dtype=F32):
    m, k = a.shape
    n = w.shape[1] - col0 if n is None else n
    n_pad = -(-n // LANES) * LANES
    tm = min(1024, m)
    tn = min(1024, n_pad)
    assert col0 % tn == 0 and n_pad % tn == 0
    cb = col0 // tn
    return pl.pallas_call(
        functools.partial(_matmul_kernel, valid=min(tn, n)),
        grid=(n_pad // tn, m // tm),
        in_specs=[pl.BlockSpec((tm, k), lambda j, i: (i, 0)),
                  pl.BlockSpec((k, tn), lambda j, i: (0, cb + j))],
        out_specs=pl.BlockSpec((tm, tn), lambda j, i: (i, j)),
        out_shape=jax.ShapeDtypeStruct((m, n_pad), out_dtype),
        scratch_shapes=[pltpu.VMEM((k, tn), BF16)],
        compiler_params=_params(("parallel", "arbitrary")),
        name="matmul",
    )(a, w)


def _matmul_resid_kernel(*refs, n_in):
    a_refs = refs[:n_in]
    w_refs = refs[n_in:2 * n_in]
    x_ref, m_ref, o_ref = refs[2 * n_in:2 * n_in + 3]
    w_scrs = refs[2 * n_in + 3:]

    @pl.when(pl.program_id(1) == 0)
    def _():
        for w_ref, w_scr in zip(w_refs, w_scrs):
            w_scr[...] = w_ref[...].astype(BF16)

    acc = _dot(a_refs[0][...], w_scrs[0][...])
    for a_ref, w_scr in zip(a_refs[1:], w_scrs[1:]):
        acc = acc + _dot(a_ref[...], w_scr[...])
    o_ref[...] = x_ref[...] + m_ref[0][2:3] * acc


def _matmul_resid(a_list, w, x, m, seqlen):
    t, d = x.shape
    tm, tn = min(512, seqlen), 1024
    per = seqlen // tm
    n_in = len(a_list)
    kb = w.shape[0] // n_in
    arrays, in_specs = [], []
    for a in a_list:
        arr, blk = a if isinstance(a, tuple) else (a, 0)
        arrays.append(arr)
        in_specs.append(pl.BlockSpec((tm, kb), lambda j, i, blk=blk: (i, blk)))
    in_specs += [pl.BlockSpec((kb, tn), lambda j, i, r=r: (r, j)) for r in range(n_in)]
    in_specs += [pl.BlockSpec((tm, tn), lambda j, i: (i, j)),
                 pl.BlockSpec((1, 3, tn), lambda j, i: (i // per, 0, j))]
    return pl.pallas_call(
        functools.partial(_matmul_resid_kernel, n_in=n_in),
        grid=(d // tn, t // tm),
        in_specs=in_specs,
        out_specs=pl.BlockSpec((tm, tn), lambda j, i: (i, j)),
        out_shape=jax.ShapeDtypeStruct((t, d), F32),
        scratch_shapes=[pltpu.VMEM((kb, tn), BF16)] * n_in,
        compiler_params=_params(("parallel", "arbitrary")),
        name="matmul_resid",
    )(*arrays, *([w] * n_in), x, m)


def _s5_kernel(u_ref, lrg_ref, lig_ref, ldg_ref, lrl_ref, lil_ref, ldl_ref, btr_ref, bti_ref, cr_ref, ci_ref,
               o_ref, *, nc, levels):
    q, gc, p = S5_CHUNK, S5_GROUP_CH, S5_STATE
    ng = LANES // gc
    gp = ng * p
    rows = u_ref.shape[0] // q
    half = q // 2

    def make_pow(lre, lim, ldt):
        lr = jnp.minimum(lre, -1e-4)
        dt = jnp.exp(ldt)

        def lam_pow(mult):
            mag = jnp.exp(lr * dt * mult)
            ang = lim * dt * mult
            return mag * jnp.cos(ang), mag * jnp.sin(ang)

        lb_re, lb_im = lam_pow(1.0)
        den = lr * lr + lim * lim
        nr, ni = lb_re - 1.0, lb_im
        return lam_pow, (nr * lr + ni * lim) / den, (ni * lr - nr * lim) / den

    pow_g, f_re_g, f_im_g = make_pow(lrg_ref[...], lig_ref[...], ldg_ref[...])
    pow_l, f_re_l, f_im_l = make_pow(lrl_ref[0], lil_ref[0], ldl_ref[0])

    ri = lax.broadcasted_iota(jnp.int32, (LANES, ng), 0)
    ci_ = lax.broadcasted_iota(jnp.int32, (LANES, ng), 1)
    to_rows = (ri // gc == ci_).astype(F32)
    expand = lambda a: _dot(to_rows, a, hi=True)
    pi_ = lax.broadcasted_iota(jnp.int32, (p, gp), 0)
    pj_ = lax.broadcasted_iota(jnp.int32, (p, gp), 1)
    to_lanes = (pi_ == pj_ % p).astype(F32)
    tile = lambda a: _dot(a, to_lanes, hi=True)
    same_rc = (lax.broadcasted_iota(jnp.int32, (LANES, LANES), 0) // gc
               == lax.broadcasted_iota(jnp.int32, (LANES, LANES), 1) // gc)
    same_rl = (lax.broadcasted_iota(jnp.int32, (LANES, gp), 0) // gc
               == lax.broadcasted_iota(jnp.int32, (LANES, gp), 1) // p)

    btr, bti = btr_ref[...], bti_ref[...]
    cr, ci = cr_ref[...], ci_ref[...]
    f_re_r, f_im_r = expand(f_re_g), expand(f_im_g)
    bb_re = f_re_r * btr - f_im_r * bti
    bb_im = f_re_r * bti + f_im_r * btr

    w = []
    for tau in range(q):
        pr, pi = pow_g(float(tau))
        pr, pi = expand(pr), expand(pi)
        k_t = (_dot_nt(bb_re, cr * pr - ci * pi, hi=True) - _dot_nt(bb_im, cr * pi + ci * pr, hi=True))
        w.append(jnp.where(same_rc, k_t, 0.0).astype(BF16))
    zero = jnp.zeros((LANES, LANES), BF16)

    def w_pair(delta):
        lo = w[2 * delta - 1] if delta > 0 else zero
        return jnp.concatenate([jnp.concatenate([w[2 * delta], w[2 * delta + 1]], axis=1),
                                jnp.concatenate([lo, w[2 * delta]], axis=1)], axis=0)

    u_pair = []
    for jp in range(half):
        u_pair.append(jnp.concatenate(
            [u_ref[pl.ds(2 * jp + jj, rows, stride=q), :].astype(BF16) for jj in range(2)], axis=1))

    bt_re, bt_im = tile(btr), tile(bti)
    bbl_re = jnp.where(same_rl, f_re_l * bt_re - f_im_l * bt_im, 0.0)
    bbl_im = jnp.where(same_rl, f_re_l * bt_im + f_im_l * bt_re, 0.0)
    st = None
    for jp in range(half):
        blocks = []
        for j in (2 * jp, 2 * jp + 1):
            pr, pi = pow_l(float(q - 1 - j))
            blocks.append(jnp.concatenate([bbl_re * pr - bbl_im * pi, bbl_re * pi + bbl_im * pr], axis=1))
        term = _dot(u_pair[jp], jnp.concatenate(blocks, axis=0))
        st = term if st is None else st + term
    sr, si = st[:, :gp], st[:, gp:]

    cidx = lax.broadcasted_iota(jnp.int32, (rows, 1), 0) % nc
    for k in range(levels):
        sh = 1 << k
        ar, ai = pow_l(float(q * sh))
        xr = pltpu.roll(sr, sh, axis=0)
        xi = pltpu.roll(si, sh, axis=0)
        ok = cidx >= sh
        sr, si = (sr + jnp.where(ok, ar * xr - ai * xi, 0.0),
                  si + jnp.where(ok, ar * xi + ai * xr, 0.0))
    ok = cidx >= 1
    s_prev = jnp.concatenate([jnp.where(ok, pltpu.roll(sr, 1, axis=0), 0.0),
                              jnp.where(ok, pltpu.roll(si, 1, axis=0), 0.0)], axis=1).astype(BF16)

    ct_re, ct_im = tile(cr), tile(ci)
    ct_re = jnp.where(same_rl, ct_re, 0.0)
    ct_im = jnp.where(same_rl, ct_im, 0.0)
    for ip in range(half):
        acc = None
        for jp in range(ip + 1):
            term = _dot(u_pair[jp], w_pair(ip - jp))
            acc = term if acc is None else acc + term
        blocks = []
        for i in (2 * ip, 2 * ip + 1):
            pr, pi = pow_l(float(i + 1))
            blocks.append(jnp.concatenate([ct_re * pr - ct_im * pi, -(ct_re * pi + ct_im * pr)], axis=1))
        acc = acc + _dot_nt(s_prev, jnp.concatenate(blocks, axis=0))
        o_ref[pl.ds(2 * ip, rows, stride=q), :] = acc[:, :LANES]
        o_ref[pl.ds(2 * ip + 1, rows, stride=q), :] = acc[:, LANES:]


def _s5_post_kernel(y_ref, u_ref, d_ref, w_ref, b_ref, o_ref):
    y = y_ref[...] + d_ref[...] * u_ref[...]
    y = jax.nn.gelu(y)
    o_ref[...] = (y * jax.nn.sigmoid(_dot(y, w_ref[...]) + b_ref[...])).astype(o_ref.dtype)


def _s5_branch(u, bsz, seqlen, lam_re, lam_im, log_dt, b_re, b_im, c_re, c_im, d_skip, glu_w, glu_b):
    t, width = u.shape
    q, gc, p = S5_CHUNK, S5_GROUP_CH, S5_STATE
    groups = width // gc
    nc = seqlen // q
    levels = int(math.log2(nc))
    assert (1 << levels) == nc
    ng = LANES // gc
    nblk = width // LANES
    gp = ng * p
    ldt = jnp.broadcast_to(log_dt[:, None], (groups, p))
    by_group = pl.BlockSpec((ng, p), lambda k: (k, 0))
    by_lane = pl.BlockSpec((1, 1, gp), lambda k: (k, 0, 0))
    by_chan = pl.BlockSpec((LANES, p), lambda k: (k, 0))
    tok = pl.BlockSpec((t, LANES), lambda k: (0, k))
    lanes3 = lambda a: a.reshape(nblk, 1, gp)
    chan2 = lambda a: a.reshape(width, p)
    y = pl.pallas_call(
        functools.partial(_s5_kernel, nc=nc, levels=levels),
        grid=(nblk,),
        in_specs=[tok, by_group, by_group, by_group, by_lane, by_lane, by_lane,
                  by_chan, by_chan, by_chan, by_chan],
        out_specs=tok,
        out_shape=jax.ShapeDtypeStruct((t, width), F32),
        compiler_params=_params(("parallel",)),
        name="s5_conv",
    )(u, lam_re, lam_im, ldt, lanes3(lam_re), lanes3(lam_im), lanes3(ldt),
      chan2(jnp.swapaxes(b_re, 1, 2)), chan2(jnp.swapaxes(b_im, 1, 2)), chan2(c_re), chan2(c_im))

    tm = min(512, t)
    return pl.pallas_call(
        _s5_post_kernel,
        grid=(t // tm,),
        in_specs=[pl.BlockSpec((tm, width), lambda i: (i, 0)),
                  pl.BlockSpec((tm, width), lambda i: (i, 0)),
                  pl.BlockSpec((1, width), lambda i: (0, 0)),
                  pl.BlockSpec((width, width), lambda i: (0, 0)),
                  pl.BlockSpec((1, width), lambda i: (0, 0))],
        out_specs=pl.BlockSpec((tm, width), lambda i: (i, 0)),
        out_shape=jax.ShapeDtypeStruct((t, width), BF16),
        compiler_params=_params(("parallel",)),
        name="s5_post",
    )(y, u, d_skip.reshape(1, width), glu_w.astype(BF16), glu_b.reshape(1, width))


def _ssd_kernel(z_ref, xbc_ref, dt_ref, cw_ref, cb_ref, dtb_ref, alog_ref, dsk_ref, ng_ref,
                o_ref, ext_scr, st_scr, *, heads, inner):
    lc = SSD_CHUNK
    hd, ns = SSD_HEAD_DIM, SSD_STATE
    rpg = heads // SSD_GROUPS
    gw = SSD_GROUPS * ns
    tail = SSD_CONV - 1

    @pl.when(pl.program_id(1) == 0)
    def _():
        ext_scr[0:8, :] = jnp.zeros((8, ext_scr.shape[1]), F32)
        st_scr[...] = jnp.zeros_like(st_scr)

    ext_scr[8:8 + lc, :] = xbc_ref[...]
    cw = cw_ref[...]
    conv = cb_ref[...] + cw[0:1] * ext_scr[pl.ds(8 - tail, lc), :]
    for k in range(1, SSD_CONV):
        conv = conv + cw[k:k + 1] * ext_scr[pl.ds(8 - tail + k, lc), :]
    ext_scr[0:8, :] = ext_scr[lc:lc + 8, :]
    act = _silu(conv)
    xs = act[:, :inner]
    bs = act[:, inner:inner + gw]
    cs = act[:, inner + gw:]

    dt = _softplus(dt_ref[...] + dtb_ref[...])
    adt = dt * (-jnp.exp(alog_ref[...]))
    ri = lax.broadcasted_iota(jnp.int32, (lc, lc), 0)
    ci = lax.broadcasted_iota(jnp.int32, (lc, lc), 1)
    causal = ri >= ci
    a_cum = _dot(causal.astype(F32), adt, hi=True)
    a_cum_t = a_cum.T
    dt_t = dt.T

    y_parts = []
    for g in range(SSD_GROUPS):
        cs_g = cs[:, g * ns:(g + 1) * ns]
        bs_g = bs[:, g * ns:(g + 1) * ns]
        cb = _dot_nt(cs_g, bs_g)
        bs_gt = bs_g.T
        for r in range(rpg):
            h = g * rpg + r
            col = a_cum[:, h:h + 1]
            row = a_cum_t[h:h + 1, :]
            dt_row = dt_t[h:h + 1, :]
            decay = jnp.exp(jnp.where(causal, col - row, -jnp.inf))
            xs_h = xs[:, h * hd:(h + 1) * hd]
            st = st_scr[h]
            lhs = jnp.concatenate([cb * decay * dt_row, cs_g * jnp.exp(col)], axis=1)
            y_parts.append(_dot(lhs, jnp.concatenate([xs_h, st], axis=0)))
            last = row[:, lc - 1:lc]
            to_end = jnp.exp(last - row)
            st_scr[h] = jnp.exp(last) * st + _dot(bs_gt * (to_end * dt_row), xs_h)
    y = jnp.concatenate(y_parts, axis=1) + dsk_ref[...] * xs
    y = y * _silu(z_ref[...])
    o_ref[...] = _rms(y, ng_ref[...]).astype(o_ref.dtype)


def _ssd_branch(z, xbc, dt_raw, bsz, seqlen, conv_w, conv_b, dt_bias, a_log, d_skip, norm_g):
    t, inner = z.shape
    cd = xbc.shape[1]
    heads = inner // SSD_HEAD_DIM
    lc = SSD_CHUNK
    nc = seqlen // lc
    pad = LANES - heads
    padv = lambda v: jnp.pad(v.reshape(1, heads), ((0, 0), (0, pad)))
    row = lambda n: pl.BlockSpec((1, n), lambda b, c: (0, 0))
    tok = lambda n: pl.BlockSpec((lc, n), lambda b, c: (b * nc + c, 0))
    return pl.pallas_call(
        functools.partial(_ssd_kernel, heads=heads, inner=inner),
        grid=(bsz, nc),
        in_specs=[tok(inner), tok(cd), tok(LANES),
                  pl.BlockSpec((SSD_CONV, cd), lambda b, c: (0, 0)), row(cd),
                  row(LANES), row(LANES), row(inner), row(inner)],
        out_specs=tok(inner),
        out_shape=jax.ShapeDtypeStruct((t, inner), BF16),
        scratch_shapes=[pltpu.VMEM((8 + lc + 8, cd), F32),
                        pltpu.VMEM((heads, SSD_STATE, SSD_HEAD_DIM), F32)],
        compiler_params=_params(("parallel", "arbitrary")),
        name="ssd",
    )(z, xbc, dt_raw, conv_w, conv_b.reshape(1, cd), padv(dt_bias), padv(a_log),
      jnp.repeat(d_skip, SSD_HEAD_DIM).reshape(1, inner), norm_g.reshape(1, inner))


def _hybrid_mixer(x, m, g, seqlen, w_in, w_out, lam_re, lam_im, log_dt, b_re, b_im, c_re, c_im, s5_d,
                  glu_w, glu_b, conv_w, conv_b, dt_bias, a_log, ssd_d, ssd_norm_g):
    t, d = x.shape
    bsz = t // seqlen
    s5w = s5_d.shape[0]
    inner = ssd_norm_g.shape[0]
    cd = conv_w.shape[1]
    heads = dt_bias.shape[0]
    o1, o2, o3 = s5w, s5w + inner, s5w + inner + cd
    h = _adaln_call(x, m, g, seqlen, BF16)
    u = _matmul(h, w_in, 0, o1)
    z = _matmul(h, w_in, o1, inner)
    xbc = _matmul(h, w_in, o2, cd)
    dt_raw = _matmul(h, w_in, o3, heads)
    y_s5 = _s5_branch(u, bsz, seqlen, lam_re, lam_im, log_dt, b_re, b_im, c_re, c_im, s5_d, glu_w, glu_b)
    y_ssd = _ssd_branch(z, xbc, dt_raw, bsz, seqlen, conv_w, conv_b, dt_bias, a_log, ssd_d, ssd_norm_g)
    assert inner % s5w == 0
    return _matmul_resid([y_s5] + [(y_ssd, i) for i in range(inner // s5w)], w_out, x, m, seqlen)


def _rwkv_mix_kernel(x_ref, m_ref, g_ref, mu_ref, *rest):
    o_refs, prev_scr = rest[:6], rest[6]
    h = _adaln(x_ref[...], g_ref[...], m_ref[0])
    tm = h.shape[0]

    @pl.when(pl.program_id(1) == 0)
    def _():
        prev_scr[...] = jnp.zeros_like(prev_scr)

    first = lax.broadcasted_iota(jnp.int32, (tm, 1), 0) == 0
    shifted = jnp.where(first, prev_scr[7:8, :], pltpu.roll(h, 1, axis=0))
    prev_scr[...] = h[tm - 8:tm]
    xx = shifted - h
    mu = mu_ref[...]
    for i, o_ref in enumerate(o_refs):
        o_ref[...] = (h + xx * mu[i:i + 1]).astype(o_ref.dtype)


def _rwkv_mix(x, m, g, mu, bsz, seqlen):
    t, d = x.shape
    tm = min(256, seqlen)
    per = seqlen // tm
    tok = pl.BlockSpec((tm, d), lambda b, i: (b * per + i, 0))
    return pl.pallas_call(
        _rwkv_mix_kernel,
        grid=(bsz, per),
        in_specs=[tok, pl.BlockSpec((1, 3, d), lambda b, i: (b, 0, 0)),
                  pl.BlockSpec((1, d), lambda b, i: (0, 0)),
                  pl.BlockSpec((8, d), lambda b, i: (0, 0))],
        out_specs=[tok] * 6,
        out_shape=[jax.ShapeDtypeStruct((t, d), BF16)] * 6,
        scratch_shapes=[pltpu.VMEM((8, d), F32)],
        compiler_params=_params(("parallel", "arbitrary")),
        name="rwkv_mix",
    )(x, m, g.reshape(1, d), jnp.pad(mu, ((0, 2), (0, 0))))


def _lora_kernel(x_ref, a_ref, b_ref, bias_ref, o_ref, *, mode):
    mid = _dot(x_ref[...], a_ref[...])
    if mode == "decay":
        mid = jnp.tanh(mid)
    elif mode == "gate":
        mid = jax.nn.sigmoid(mid)
    out = _dot(mid, b_ref[...])
    if mode == "decay":
        w = -_softplus(-(bias_ref[...] + out)) - 0.5
        out = -jnp.exp(w)
    elif mode == "icl":
        out = jax.nn.sigmoid(bias_ref[...] + out)
    o_ref[...] = out


def _lora(x, a, b, bias, mode):
    t, d = x.shape
    rank = a.shape[1]
    rp = -(-rank // LANES) * LANES
    ap = jnp.pad(a, ((0, 0), (0, rp - rank))).astype(BF16)
    bp = jnp.pad(b, ((0, rp - rank), (0, 0))).astype(BF16)
    tm = min(512, t)
    return pl.pallas_call(
        functools.partial(_lora_kernel, mode=mode),
        grid=(t // tm,),
        in_specs=[pl.BlockSpec((tm, d), lambda i: (i, 0)),
                  pl.BlockSpec((d, rp), lambda i: (0, 0)),
                  pl.BlockSpec((rp, d), lambda i: (0, 0)),
                  pl.BlockSpec((1, d), lambda i: (0, 0))],
        out_specs=pl.BlockSpec((tm, d), lambda i: (i, 0)),
        out_shape=jax.ShapeDtypeStruct((t, d), F32),
        compiler_params=_params(("parallel",)),
        name="rwkv_lora_" + mode,
    )(x, ap, bp, bias.reshape(1, d))


def _rwkv_scan_kernel(r_ref, k_ref, v_ref, lw_ref, a_ref, g_ref, kk_ref, ka_ref, rk_ref, lg_ref, lb_ref,
                      o_ref, s_scr):
    c = RWKV_CHUNK
    hd = RWKV_HEAD_DIM
    c2 = 2 * c
    n_chunks = r_ref.shape[0] // c

    @pl.when(pl.program_id(2) == 0)
    def _():
        s_scr[...] = jnp.zeros_like(s_scr)

    lane_head = lax.broadcasted_iota(jnp.int32, (1, LANES), 1) // hd
    head0 = lane_head == 0
    same_head = ((lax.broadcasted_iota(jnp.int32, (LANES, LANES), 0) // hd)
                 == (lax.broadcasted_iota(jnp.int32, (LANES, LANES), 1) // hd))
    ones_bd = same_head.astype(F32)
    tb = r_ref.shape[0]
    tg = min(tb, 2 * LANES)
    ri = lax.broadcasted_iota(jnp.int32, (tg, tg), 0)
    ci = lax.broadcasted_iota(jnp.int32, (tg, tg), 1)
    tri_b16 = ((ri // c == ci // c) & (ri >= ci)).astype(BF16)
    r2 = lax.broadcasted_iota(jnp.int32, (1, c2, c2), 1)
    q2 = lax.broadcasted_iota(jnp.int32, (1, c2, c2), 2)
    same_blk = (r2 // c) == (q2 // c)
    strict = same_blk & ((r2 % c) > (q2 % c))
    incl = same_blk & ((r2 % c) >= (q2 % c))
    eye2 = (r2 == q2).astype(F32)
    k_k, k_a, r_k = kk_ref[...], ka_ref[...], rk_ref[...]
    ln_g, ln_b = lg_ref[...], lb_ref[...]
    ones_b16 = ones_bd.astype(BF16)

    def stack(x):
        x = x.reshape(n_chunks, c, LANES)
        return jnp.concatenate([jnp.where(head0, x, 0.0), jnp.where(head0, 0.0, x)], axis=1)

    def unstack(x):
        return (x[:, :c] + x[:, c:]).reshape(tb, LANES)

    def bdot(p, q, dims):
        return lax.dot_general(p.astype(BF16), q.astype(BF16), (dims, ((0,), (0,))), preferred_element_type=F32)

    bmm = lambda p, q: bdot(p, q, ((2,), (1,)))
    bmm_nt = lambda p, q: bdot(p, q, ((2,), (2,)))
    bmm_tn = lambda p, q: bdot(p, q, ((1,), (1,)))

    def split(x):
        hi_part = x.astype(BF16)
        return hi_part, (x - hi_part.astype(F32)).astype(BF16)

    ones2 = jnp.concatenate([ones_b16, ones_b16], axis=0)

    def head_sum(x):
        return jnp.dot(jnp.concatenate(split(x), axis=1), ones2, preferred_element_type=F32)

    def cumsum_rows(x):
        outs = []
        for g0 in range(0, tb, tg):
            xg = x[g0:g0 + tg]
            p1, p2 = split(xg)
            p3 =(xg - p1.astype(F32) - p2.astype(F32)).astype(BF16)
            wide = jnp.dot(tri_b16, jnp.concatenate([p1, p2], axis=1), preferred_element_type=F32)
            outs.append(wide[:, :LANES] + wide[:, LANES:] + jnp.dot(tri_b16, p3, preferred_element_type=F32))
        return outs[0] if len(outs) == 1 else jnp.concatenate(outs, axis=0)

    def lhs3(hi_part, lo_part):
        return jnp.concatenate([hi_part, lo_part], axis=2)

    def rhs3(hi_part, lo_part):
        top = jnp.concatenate([hi_part, lo_part], axis=2)
        bot = jnp.concatenate([hi_part, jnp.zeros_like(lo_part)], axis=2)
        return jnp.concatenate([top, bot], axis=1)

    def fold(x):
        return x[:, :, :c2] + x[:, :, c2:]

    r, k, v, lw, a = r_ref[...], k_ref[...], v_ref[...], lw_ref[...], a_ref[...]
    kk = k * k_k
    kk = kk / jnp.maximum(jnp.sqrt(head_sum(kk * kk)), 1e-12)
    k2 = k * (1.0 + (a - 1.0) * k_a)
    bv = kk * a
    cum = cumsum_rows(lw)
    cum3 = cum.reshape(n_chunks, c, LANES)
    tot = jnp.broadcast_to(cum3[:, c - 1:c, :], cum3.shape).reshape(tb, LANES)
    g_inv = jnp.exp(-cum)
    rt_u = r * jnp.exp(cum)
    at = stack(-kk * jnp.exp(cum - lw))
    rt = stack(rt_u)
    bt = stack(bv * g_inv)
    kt = stack(k2 * g_inv)
    to_end = jnp.exp(tot - cum)
    b_end = (bv * to_end).reshape(n_chunks, c, LANES)
    k_end = (k2 * to_end).reshape(n_chunks, c, LANES)
    vs = stack(v)

    gram = bmm_nt(jnp.concatenate([at, rt], axis=1), jnp.concatenate([bt, kt], axis=1))
    n_ab = jnp.where(strict, gram[:, :c2, :c2], 0.0)
    a_ak = jnp.where(strict, gram[:, :c2, c2:], 0.0)
    a_rb = jnp.where(incl, gram[:, c2:, :c2], 0.0)
    a_rk = jnp.where(incl, gram[:, c2:, c2:], 0.0)
    pw_h, pw_l = split(n_ab)
    pw = fold(bmm(lhs3(pw_h, pw_l), rhs3(pw_h, pw_l)))
    tinv = eye2 + n_ab
    for _ in range(int(math.log2(c)) - 2):
        pw_h, pw_l = split(pw)
        t_h, t_l = split(tinv)
        both = fold(bmm(jnp.concatenate([lhs3(pw_h, pw_l), lhs3(t_h, t_l)], axis=1), rhs3(pw_h, pw_l)))
        pw = both[:, :c2]
        tinv = tinv + both[:, c2:]
    pw_h, pw_l = split(pw)
    t_h, t_l = split(tinv)
    tinv = tinv + fold(bmm(lhs3(t_h, t_l), rhs3(pw_h, pw_l)))
    pq = bmm(tinv, jnp.concatenate([at, bmm(a_ak, vs)], axis=2))
    ry = bmm(jnp.concatenate([a_rb, a_rk], axis=2),
             jnp.concatenate([pq, jnp.concatenate([jnp.zeros_like(vs), vs], axis=2)], axis=1))
    rq = rt_u + unstack(ry[:, :, :LANES])
    y0 = unstack(ry[:, :, LANES:])
    p_u = unstack(pq[:, :, :LANES]).reshape(n_chunks, c, LANES)
    q_u = unstack(pq[:, :, LANES:]).reshape(n_chunks, c, LANES)
    gm = jnp.where(same_head, bmm_tn(p_u, b_end), 0.0)
    dm = jnp.where(same_head, bmm_tn(jnp.concatenate([q_u, v.reshape(n_chunks, c, LANES)], axis=1),
                                     jnp.concatenate([b_end, k_end], axis=1)), 0.0)
    g_tot = jnp.exp(cum3[:, c - 1:c, :])
    bonus = head_sum(r * k2 * r_k) * v

    s = s_scr[...]
    ys = []
    for i in range(n_chunks):
        ys.append(_dot_nt(rq[i * c:(i + 1) * c], s))
        s = s * g_tot[i] + _dot(s, gm[i]) + dm[i]
    s_scr[...] = s
    y = jnp.concatenate(ys, axis=0) + y0
    mean = head_sum(y) * (1.0 / hd)
    dy = y - mean
    var = head_sum(dy * dy) * (1.0 / hd)
    yn = dy * lax.rsqrt(var + RWKV_GN_EPS) * ln_g + ln_b
    o_ref[...] = ((yn + bonus) * g_ref[...]).astype(o_ref.dtype)


def _rwkv_scan(r, k, v, lw, a, g, k_k, k_a, r_k, ln_g, ln_b, bsz, seqlen):
    t, d = r.shape
    tb = min(RWKV_TOKEN_BLOCK, seqlen)
    per = seqlen // tb
    tok = pl.BlockSpec((tb, LANES), lambda b, h, i: (b * per + i, h))
    row = pl.BlockSpec((1, LANES), lambda b, h, i: (0, h))
    vec = lambda p: p.reshape(1, d)
    return pl.pallas_call(
        _rwkv_scan_kernel,
        grid=(bsz, d // LANES, per),
        in_specs=[tok] * 6 + [row] * 5,
        out_specs=tok,
        out_shape=jax.ShapeDtypeStruct((t, d), BF16),
        scratch_shapes=[pltpu.VMEM((LANES, LANES), F32)],
        compiler_params=_params(("parallel", "parallel", "arbitrary")),
        name="rwkv_scan",
    )(r, k, v, lw, a, g, vec(k_k), vec(k_a), vec(r_k), vec(ln_g), vec(ln_b))


def _rwkv_mixer(x, m, g, seqlen, mu, w_r, w_k, w_v, w_o, w0, w1, w2, a0, a1, a2, g1, g2,
                k_k, k_a, r_k, ln_g, ln_b):
    t, d = x.shape
    bsz = t // seqlen
    xr, xw, xk, xv, xa, xg = _rwkv_mix(x, m, g, mu, bsz, seqlen)
    r = _matmul(xr, w_r)
    k = _matmul(xk, w_k)
    v = _matmul(xv, w_v)
    lw = _lora(xw, w1, w2, w0, "decay")
    a = _lora(xa, a1, a2, a0, "icl")
    gate = _lora(xg, g1, g2, jnp.zeros_like(w0), "gate")
    y = _rwkv_scan(r, k, v, lw, a, gate, k_k, k_a, r_k.reshape(-1), ln_g, ln_b, bsz, seqlen)
    return _matmul_resid([y], w_o, x, m, seqlen)


def kernel(x, c, w_mod, b_mod, norm_g, ffn_w1, ffn_w3, ffn_w2, hyb_w_in, hyb_w_out, s5_lambda_re, s5_lambda_im, s5_log_dt, s5_b_re, s5_b_im, s5_c_re, s5_c_im, s5_d, s5_glu_w, s5_glu_b, ssd_conv_w, ssd_conv_b, ssd_dt_bias, ssd_a_log, ssd_d, ssd_norm_g, rwkv_mu, rwkv_w_r, rwkv_w_k, rwkv_w_v, rwkv_w_o, rwkv_w0, rwkv_w1, rwkv_w2, rwkv_a0, rwkv_a1, rwkv_a2, rwkv_g1, rwkv_g2, rwkv_k_k, rwkv_k_a, rwkv_r_k, rwkv_ln_g, rwkv_ln_b, final_g):
    bsz, seqlen, d = x.shape
    depth = w_mod.shape[0]
    xf = x.reshape(bsz * seqlen, d)
    mod = _modulation(c, w_mod, b_mod).reshape(depth, bsz, 3, 3, d)
    for layer in range(depth):
        i = layer // 2
        m0, m1, m2 = mod[layer, :, 0], mod[layer, :, 1], mod[layer, :, 2]
        xf = _ffn(xf, m0, norm_g[layer, 0], ffn_w1, ffn_w3, ffn_w2, layer, 0, seqlen)
        if layer % 2 == 0:
            xf = _hybrid_mixer(xf, m1, norm_g[layer, 1], seqlen, hyb_w_in[i], hyb_w_out[i], s5_lambda_re[i],
                               s5_lambda_im[i], s5_log_dt[i], s5_b_re[i], s5_b_im[i], s5_c_re[i], s5_c_im[i],
                               s5_d[i], s5_glu_w[i], s5_glu_b[i], ssd_conv_w[i], ssd_conv_b[i], ssd_dt_bias[i],
                               ssd_a_log[i], ssd_d[i], ssd_norm_g[i])
        else:
            xf = _rwkv_mixer(xf, m1, norm_g[layer, 1], seqlen, rwkv_mu[i], rwkv_w_r[i], rwkv_w_k[i], rwkv_w_v[i],
                             rwkv_w_o[i], rwkv_w0[i], rwkv_w1[i], rwkv_w2[i], rwkv_a0[i], rwkv_a1[i], rwkv_a2[i],
                             rwkv_g1[i], rwkv_g2[i], rwkv_k_k[i], rwkv_k_a[i], rwkv_r_k[i], rwkv_ln_g[i],
                             rwkv_ln_b[i])
        xf = _ffn(xf, m2, norm_g[layer, 2], ffn_w1, ffn_w3, ffn_w2, layer, 1, seqlen,
                  final_g=final_g if layer == depth - 1 else None)
    return xf.reshape(bsz, seqlen, d)
```

```python
import functools
import math

import jax
import jax.numpy as jnp
from jax import lax
from jax.experimental import pallas as pl
from jax.experimental.pallas import tpu as pltpu

F32 = jnp.float32
BF16 = jnp.bfloat16
HI = lax.Precision.HIGHEST

RMS_EPS = 1e-6
RWKV_GN_EPS = 64e-5

S5_GROUP_CH = 16
S5_STATE = 64
S5_CHUNK = 16
SSD_HEAD_DIM = 64
SSD_STATE = 128
SSD_GROUPS = 8
SSD_CHUNK = 128
SSD_CONV = 4
RWKV_HEAD_DIM = 64
RWKV_CHUNK = 64
RWKV_TOKEN_BLOCK = 512
RWKV_PAIRS_PER_BLOCK = 2
LANES = 128
VMEM_LIMIT = 56 * 1024 * 1024

def _params(sem, vmem=VMEM_LIMIT):
    return pltpu.CompilerParams(dimension_semantics=sem, vmem_limit_bytes=vmem)


def _dot(a, b, hi=False):
    dn = (((1,), (0,)), ((), ()))
    if hi:
        return lax.dot_general(a, b, dn, precision=HI, preferred_element_type=F32)
    return lax.dot_general(a.astype(BF16), b.astype(BF16), dn, preferred_element_type=F32)


def _dot_nt(a, b, hi=False):
    dn = (((1,), (1,)), ((), ()))
    if hi:
        return lax.dot_general(a, b, dn, precision=HI, preferred_element_type=F32)
    return lax.dot_general(a.astype(BF16), b.astype(BF16), dn, preferred_element_type=F32)


def _dot_tn(a, b):
    dn = (((0,), (0,)), ((), ()))
    return lax.dot_general(a.astype(BF16), b.astype(BF16), dn, preferred_element_type=F32)


def _silu(x):
    return x * jax.nn.sigmoid(x)


def _softplus(x):
    return jnp.maximum(x, 0.0) + jnp.log1p(jnp.exp(-jnp.abs(x)))


def _rms(x, g):
    return x * lax.rsqrt(jnp.mean(x * x, axis=-1, keepdims=True) + RMS_EPS) * g


def _adaln(x, g, m):
    return _rms(x, g) * (1.0 + m[1:2]) + m[0:1]


def _mod_kernel(c_ref, w_ref, b_ref, o_ref):
    c = c_ref[...]
    o_ref[0] = _dot(_silu(c), w_ref[0]) + b_ref[0]


def _modulation(c, w_mod, b_mod):
    depth, d, n = w_mod.shape
    bsz = c.shape[0]
    rows = 8
    cp = jnp.pad(c, ((0, rows - bsz), (0, 0)))
    tn = 1024
    out = pl.pallas_call(
        _mod_kernel,
        grid=(depth, n // tn),
        in_specs=[pl.BlockSpec((rows, d), lambda l, j: (0, 0)),
                  pl.BlockSpec((1, d, tn), lambda l, j: (l, 0, j)),
                  pl.BlockSpec((1, 1, tn), lambda l, j: (l, 0, j))],
        out_specs=pl.BlockSpec((1, rows, tn), lambda l, j: (l, 0, j)),
        out_shape=jax.ShapeDtypeStruct((depth, rows, n), F32),
        compiler_params=_params(("arbitrary", "arbitrary")),
        name="modulation",
    )(cp, w_mod, b_mod.reshape(depth, 1, n))
    return out[:, :bsz]


def _ffn_kernel(x_ref, m_ref, g_ref, w1_ref, w3_ref, w2_ref, *rest, tail):
    if tail == "final":
        fg_ref, o_ref, h_scr = rest
    elif tail == "next":
        ng_ref, nm_ref, o_ref, hn_ref, h_scr = rest
    else:
        o_ref, h_scr = rest
    j = pl.program_id(1)
    tm = x_ref.shape[0]
    rc = min(256, tm)

    @pl.when(j == 0)
    def _():
        for r0 in range(0, tm, rc):
            h_scr[r0:r0 + rc, :] = _adaln(x_ref[r0:r0 + rc, :], g_ref[...], m_ref[0]).astype(BF16)
        o_ref[...] = jnp.zeros_like(o_ref)

    h = h_scr[...]
    a = _dot(h, w1_ref[...])
    b = _dot(h, w3_ref[...])
    o_ref[...] += _dot(_silu(a) * b, w2_ref[...])

    @pl.when(j == pl.num_programs(1) - 1)
    def _():
        for r0 in range(0, tm, rc):
            o = x_ref[r0:r0 + rc, :] + (0.5 * m_ref[0][2:3]) * o_ref[r0:r0 + rc, :]
            if tail == "final":
                o = _rms(o, fg_ref[...])
            o_ref[r0:r0 + rc, :] = o
            if tail == "next":
                hn_ref[r0:r0 + rc, :] = _adaln(o, ng_ref[...], nm_ref[0]).astype(hn_ref.dtype)


def _ffn(x, m, g, w1, w3, w2, layer, which, seqlen, final_g=None, next_ln=None):
    t, d = x.shape
    f = w1.shape[-1]
    tm, tf = min(1024, seqlen), 256
    per = seqlen // tm
    tail = "final" if final_g is not None else "next" if next_ln is not None else "plain"
    vec = pl.BlockSpec((1, d), lambda i, j: (0, 0))
    mod = pl.BlockSpec((1, 3, d), lambda i, j: (i // per, 0, 0))
    tok = pl.BlockSpec((tm, d), lambda i, j: (i, 0))
    in_specs = [pl.BlockSpec((tm, d), lambda i, j: (i, 0), pipeline_mode=pl.Buffered(1)), mod, vec,
                pl.BlockSpec((None, None, d, tf), lambda i, j: (layer, which, 0, j)),
                pl.BlockSpec((None, None, d, tf), lambda i, j: (layer, which, 0, j)),
                pl.BlockSpec((None, None, tf, d), lambda i, j: (layer, which, j, 0))]
    args = [x, m, g.reshape(1, d), w1, w3, w2]
    out_specs, out_shape = tok, jax.ShapeDtypeStruct((t, d), F32)
    if tail == "final":
        in_specs.append(vec)
        args.append(final_g.reshape(1, d))
    elif tail == "next":
        in_specs += [vec, mod]
        args += [next_ln[0].reshape(1, d), next_ln[1]]
        out_specs, out_shape = [tok, tok], [out_shape, jax.ShapeDtypeStruct((t, d), BF16)]
    return pl.pallas_call(
        functools.partial(_ffn_kernel, tail=tail),
        grid=(t // tm, f // tf),
        in_specs=in_specs,
        out_specs=out_specs,
        out_shape=out_shape,
        scratch_shapes=[pltpu.VMEM((tm, d), BF16)],
        compiler_params=_params(("parallel", "arbitrary")),
        name="ffn",
    )(*args)


def _matmul_kernel(a_ref, w_ref, o_ref, w_scr, *, valid):
    @pl.when(pl.program_id(1) == 0)
    def _():
        w = w_ref[...]
        if valid < w.shape[1]:
            w = jnp.where(lax.broadcasted_iota(jnp.int32, w.shape, 1) < valid, w, 0.0)
        w_scr[...] = w.astype(BF16)

    o_ref[...] = _dot(a_ref[...], w_scr[...]).astype(o_ref.dtype)


def _matmul(a, w, lead, col0=0, n=None, out_dtype=F32):
    m, k = a.shape
    n = w.shape[2] - col0 if n is None else n
    n_pad = -(-n // LANES) * LANES
    tm = min(1024, m)
    tn = min(1024, n_pad)
    assert col0 % tn == 0 and n_pad % tn == 0
    cb = col0 // tn
    return pl.pallas_call(
        functools.partial(_matmul_kernel, valid=min(tn, n)),
        grid=(n_pad // tn, m // tm),
        in_specs=[pl.BlockSpec((tm, k), lambda j, i: (i, 0)),
                  pl.BlockSpec((None, k, tn), lambda j, i: (lead, 0, cb + j))],
        out_specs=pl.BlockSpec((tm, tn), lambda j, i: (i, j)),
        out_shape=jax.ShapeDtypeStruct((m, n_pad), out_dtype),
        scratch_shapes=[pltpu.VMEM((k, tn), BF16)],
        compiler_params=_params(("parallel", "arbitrary")),
        name="matmul",
    )(a, w)


def _matmul_resid_kernel(*refs, n_in):
    a_refs = refs[:n_in]
    w_refs = refs[n_in:2 * n_in]
    x_ref, m_ref, o_ref = refs[2 * n_in:2 * n_in + 3]
    w_scrs = refs[2 * n_in + 3:]

    @pl.when(pl.program_id(1) == 0)
    def _():
        for w_ref, w_scr in zip(w_refs, w_scrs):
            w_scr[...] = w_ref[...].astype(BF16)

    acc = _dot(a_refs[0][...], w_scrs[0][...])
    for a_ref, w_scr in zip(a_refs[1:], w_scrs[1:]):
        acc = acc + _dot(a_ref[...], w_scr[...])
    o_ref[...] = x_ref[...] + m_ref[0][2:3] * acc


def _matmul_resid(a_list, w, x, m, seqlen):
    t, d = x.shape
    tm, tn = min(512, seqlen), 1024
    per = seqlen // tm
    n_in = len(a_list)
    kb = w.shape[0] // n_in
    arrays, in_specs = [], []
    for a in a_list:
        arr, blk = a if isinstance(a, tuple) else (a, 0)
        arrays.append(arr)
        in_specs.append(pl.BlockSpec((tm, kb), lambda j, i, blk=blk: (i, blk)))
    in_specs += [pl.BlockSpec((kb, tn), lambda j, i, r=r: (r, j)) for r in range(n_in)]
    in_specs += [pl.BlockSpec((tm, tn), lambda j, i: (i, j)),
                 pl.BlockSpec((1, 3, tn), lambda j, i: (i // per, 0, j))]
    return pl.pallas_call(
        functools.partial(_matmul_resid_kernel, n_in=n_in),
        grid=(d // tn, t // tm),
        in_specs=in_specs,
        out_specs=pl.BlockSpec((tm, tn), lambda j, i: (i, j)),
        out_shape=jax.ShapeDtypeStruct((t, d), F32),
        scratch_shapes=[pltpu.VMEM((kb, tn), BF16)] * n_in,
        compiler_params=_params(("parallel", "arbitrary")),
        name="matmul_resid",
    )(*arrays, *([w] * n_in), x, m)


def _s5_kernel(u_ref, lrg_ref, lig_ref, ldg_ref, lrl_ref, lil_ref, ldl_ref, btr_ref, bti_ref, cr_ref, ci_ref,
               o_ref, *, nc, levels):
    q, gc, p = S5_CHUNK, S5_GROUP_CH, S5_STATE
    ng = LANES // gc
    gp = ng * p
    rows = u_ref.shape[0] // q
    half = q // 2

    def make_pow(lre, lim, ldt):
        lr = jnp.minimum(lre, -1e-4)
        dt = jnp.exp(ldt)

        def lam_pow(mult):
            mag = jnp.exp(lr * dt * mult)
            ang = lim * dt * mult
            return mag * jnp.cos(ang), mag * jnp.sin(ang)

        lb_re, lb_im = lam_pow(1.0)
        den = lr * lr + lim * lim
        nr, ni = lb_re - 1.0, lb_im
        return lam_pow, (nr * lr + ni * lim) / den, (ni * lr - nr * lim) / den

    pow_g, f_re_g, f_im_g = make_pow(lrg_ref[...], lig_ref[...], ldg_ref[...])
    pow_l, f_re_l, f_im_l = make_pow(lrl_ref[0], lil_ref[0], ldl_ref[0])

    ri = lax.broadcasted_iota(jnp.int32, (LANES, ng), 0)
    ci_ = lax.broadcasted_iota(jnp.int32, (LANES, ng), 1)
    to_rows = (ri // gc == ci_).astype(F32)
    expand = lambda a: _dot(to_rows, a, hi=True)
    pi_ = lax.broadcasted_iota(jnp.int32, (p, gp), 0)
    pj_ = lax.broadcasted_iota(jnp.int32, (p, gp), 1)
    to_lanes = (pi_ == pj_ % p).astype(F32)
    tile = lambda a: _dot(a, to_lanes, hi=True)
    same_rc = (lax.broadcasted_iota(jnp.int32, (LANES, LANES), 0) // gc
               == lax.broadcasted_iota(jnp.int32, (LANES, LANES), 1) // gc)
    same_rl = (lax.broadcasted_iota(jnp.int32, (LANES, gp), 0) // gc
               == lax.broadcasted_iota(jnp.int32, (LANES, gp), 1) // p)

    btr, bti = btr_ref[...], bti_ref[...]
    cr, ci = cr_ref[...], ci_ref[...]
    f_re_r, f_im_r = expand(f_re_g), expand(f_im_g)
    bb_re = f_re_r * btr - f_im_r * bti
    bb_im = f_re_r * bti + f_im_r * btr

    w = []
    for tau in range(q):
        pr, pi = pow_g(float(tau))
        pr, pi = expand(pr), expand(pi)
        k_t = (_dot_nt(bb_re, cr * pr - ci * pi, hi=True) - _dot_nt(bb_im, cr * pi + ci * pr, hi=True))
        w.append(jnp.where(same_rc, k_t, 0.0).astype(BF16))
    zero = jnp.zeros((LANES, LANES), BF16)

    def w_pair(delta):
        lo = w[2 * delta - 1] if delta > 0 else zero
        return jnp.concatenate([jnp.concatenate([w[2 * delta], w[2 * delta + 1]], axis=1),
                                jnp.concatenate([lo, w[2 * delta]], axis=1)], axis=0)

    u_pair = []
    for jp in range(half):
        u_pair.append(jnp.concatenate(
            [u_ref[pl.ds(2 * jp + jj, rows, stride=q), :].astype(BF16) for jj in range(2)], axis=1))

    bt_re, bt_im = tile(btr), tile(bti)
    bbl_re = jnp.where(same_rl, f_re_l * bt_re - f_im_l * bt_im, 0.0)
    bbl_im = jnp.where(same_rl, f_re_l * bt_im + f_im_l * bt_re, 0.0)
    st = None
    for jp in range(half):
        blocks = []
        for j in (2 * jp, 2 * jp + 1):
            pr, pi = pow_l(float(q - 1 - j))
            blocks.append(jnp.concatenate([bbl_re * pr - bbl_im * pi, bbl_re * pi + bbl_im * pr], axis=1))
        term = _dot(u_pair[jp], jnp.concatenate(blocks, axis=0))
        st = term if st is None else st + term
    sr, si = st[:, :gp], st[:, gp:]

    cidx = lax.broadcasted_iota(jnp.int32, (rows, 1), 0) % nc
    for k in range(levels):
        sh = 1 << k
        ar, ai = pow_l(float(q * sh))
        xr = pltpu.roll(sr, sh, axis=0)
        xi = pltpu.roll(si, sh, axis=0)
        ok = cidx >= sh
        sr, si = (sr + jnp.where(ok, ar * xr - ai * xi, 0.0),
                  si + jnp.where(ok, ar * xi + ai * xr, 0.0))
    ok = cidx >= 1
    s_prev = jnp.concatenate([jnp.where(ok, pltpu.roll(sr, 1, axis=0), 0.0),
                              jnp.where(ok, pltpu.roll(si, 1, axis=0), 0.0)], axis=1).astype(BF16)

    ct_re, ct_im = tile(cr), tile(ci)
    ct_re = jnp.where(same_rl, ct_re, 0.0)
    ct_im = jnp.where(same_rl, ct_im, 0.0)
    for ip in range(half):
        acc = None
        for jp in range(ip + 1):
            term = _dot(u_pair[jp], w_pair(ip - jp))
            acc = term if acc is None else acc + term
        blocks = []
        for i in (2 * ip, 2 * ip + 1):
            pr, pi = pow_l(float(i + 1))
            blocks.append(jnp.concatenate([ct_re * pr - ct_im * pi, -(ct_re * pi + ct_im * pr)], axis=1))
        acc = acc + _dot_nt(s_prev, jnp.concatenate(blocks, axis=0))
        o_ref[pl.ds(2 * ip, rows, stride=q), :] = acc[:, :LANES]
        o_ref[pl.ds(2 * ip + 1, rows, stride=q), :] = acc[:, LANES:]


def _s5_post_kernel(y_ref, u_ref, d_ref, w_ref, b_ref, o_ref):
    y = y_ref[...] + d_ref[...] * u_ref[...]
    y = jax.nn.gelu(y)
    o_ref[...] = (y * jax.nn.sigmoid(_dot(y, w_ref[...]) + b_ref[...])).astype(o_ref.dtype)


def _s5_branch(u, bsz, seqlen, lam_re, lam_im, log_dt, b_re, b_im, c_re, c_im, d_skip, glu_w, glu_b):
    t, width = u.shape
    q, gc, p = S5_CHUNK, S5_GROUP_CH, S5_STATE
    groups = width // gc
    nc = seqlen // q
    levels = int(math.log2(nc))
    assert (1 << levels) == nc
    ng = LANES // gc
    nblk = width // LANES
    gp = ng * p
    ldt = jnp.broadcast_to(log_dt[:, None], (groups, p))
    by_group = pl.BlockSpec((ng, p), lambda k: (k, 0))
    by_lane = pl.BlockSpec((1, 1, gp), lambda k: (k, 0, 0))
    by_chan = pl.BlockSpec((LANES, p), lambda k: (k, 0))
    tok = pl.BlockSpec((t, LANES), lambda k: (0, k))
    lanes3 = lambda a: a.reshape(nblk, 1, gp)
    chan2 = lambda a: a.reshape(width, p)
    y = pl.pallas_call(
        functools.partial(_s5_kernel, nc=nc, levels=levels),
        grid=(nblk,),
        in_specs=[tok, by_group, by_group, by_group, by_lane, by_lane, by_lane,
                  by_chan, by_chan, by_chan, by_chan],
        out_specs=tok,
        out_shape=jax.ShapeDtypeStruct((t, width), F32),
        compiler_params=_params(("parallel",)),
        name="s5_conv",
    )(u, lam_re, lam_im, ldt, lanes3(lam_re), lanes3(lam_im), lanes3(ldt),
      chan2(jnp.swapaxes(b_re, 1, 2)), chan2(jnp.swapaxes(b_im, 1, 2)), chan2(c_re), chan2(c_im))

    tm = min(512, t)
    return pl.pallas_call(
        _s5_post_kernel,
        grid=(t // tm,),
        in_specs=[pl.BlockSpec((tm, width), lambda i: (i, 0)),
                  pl.BlockSpec((tm, width), lambda i: (i, 0)),
                  pl.BlockSpec((1, width), lambda i: (0, 0)),
                  pl.BlockSpec((width, width), lambda i: (0, 0)),
                  pl.BlockSpec((1, width), lambda i: (0, 0))],
        out_specs=pl.BlockSpec((tm, width), lambda i: (i, 0)),
        out_shape=jax.ShapeDtypeStruct((t, width), BF16),
        compiler_params=_params(("parallel",)),
        name="s5_post",
    )(y, u, d_skip.reshape(1, width), glu_w.astype(BF16), glu_b.reshape(1, width))


def _ssd_kernel(z_ref, xbc_ref, dt_ref, cw_ref, cb_ref, dtb_ref, alog_ref, dsk_ref, ng_ref,
                o_ref, ext_scr, st_scr, *, heads, inner):
    lc = SSD_CHUNK
    hd, ns = SSD_HEAD_DIM, SSD_STATE
    rpg = heads // SSD_GROUPS
    gw = SSD_GROUPS * ns
    tail = SSD_CONV - 1

    @pl.when(pl.program_id(1) == 0)
    def _():
        ext_scr[0:8, :] = jnp.zeros((8, ext_scr.shape[1]), F32)
        st_scr[...] = jnp.zeros_like(st_scr)

    ext_scr[8:8 + lc, :] = xbc_ref[...]
    cw = cw_ref[...]
    conv = cb_ref[...] + cw[0:1] * ext_scr[pl.ds(8 - tail, lc), :]
    for k in range(1, SSD_CONV):
        conv = conv + cw[k:k + 1] * ext_scr[pl.ds(8 - tail + k, lc), :]
    ext_scr[0:8, :] = ext_scr[lc:lc + 8, :]
    act = _silu(conv)
    xs = act[:, :inner]
    bs = act[:, inner:inner + gw]
    cs = act[:, inner + gw:]

    dt = _softplus(dt_ref[...] + dtb_ref[...])
    adt = dt * (-jnp.exp(alog_ref[...]))
    ri = lax.broadcasted_iota(jnp.int32, (lc, lc), 0)
    ci = lax.broadcasted_iota(jnp.int32, (lc, lc), 1)
    causal = ri >= ci
    a_cum = _dot(causal.astype(F32), adt, hi=True)
    a_cum_t = a_cum.T
    dt_t = dt.T

    y_parts = []
    for g in range(SSD_GROUPS):
        cs_g = cs[:, g * ns:(g + 1) * ns]
        bs_g = bs[:, g * ns:(g + 1) * ns]
        cb = _dot_nt(cs_g, bs_g)
        bs_gt = bs_g.T
        for r in range(rpg):
            h = g * rpg + r
            col = a_cum[:, h:h + 1]
            row = a_cum_t[h:h + 1, :]
            dt_row = dt_t[h:h + 1, :]
            decay = jnp.exp(jnp.where(causal, col - row, -jnp.inf))
            xs_h = xs[:, h * hd:(h + 1) * hd]
            st = st_scr[h]
            lhs = jnp.concatenate([cb * decay * dt_row, cs_g * jnp.exp(col)], axis=1)
            y_parts.append(_dot(lhs, jnp.concatenate([xs_h, st], axis=0)))
            last = row[:, lc - 1:lc]
            to_end = jnp.exp(last - row)
            st_scr[h] = jnp.exp(last) * st + _dot(bs_gt * (to_end * dt_row), xs_h)
    y = jnp.concatenate(y_parts, axis=1) + dsk_ref[...] * xs
    y = y * _silu(z_ref[...])
    o_ref[...] = _rms(y, ng_ref[...]).astype(o_ref.dtype)


def _ssd_branch(z, xbc, dt_raw, bsz, seqlen, conv_w, conv_b, dt_bias, a_log, d_skip, norm_g):
    t, inner = z.shape
    cd = xbc.shape[1]
    heads = inner // SSD_HEAD_DIM
    lc = SSD_CHUNK
    nc = seqlen // lc
    pad = LANES - heads
    padv = lambda v: jnp.pad(v.reshape(1, heads), ((0, 0), (0, pad)))
    row = lambda n: pl.BlockSpec((1, n), lambda b, c: (0, 0))
    tok = lambda n: pl.BlockSpec((lc, n), lambda b, c: (b * nc + c, 0))
    return pl.pallas_call(
        functools.partial(_ssd_kernel, heads=heads, inner=inner),
        grid=(bsz, nc),
        in_specs=[tok(inner), tok(cd), tok(LANES),
                  pl.BlockSpec((SSD_CONV, cd), lambda b, c: (0, 0)), row(cd),
                  row(LANES), row(LANES), row(inner), row(inner)],
        out_specs=tok(inner),
        out_shape=jax.ShapeDtypeStruct((t, inner), BF16),
        scratch_shapes=[pltpu.VMEM((8 + lc + 8, cd), F32),
                        pltpu.VMEM((heads, SSD_STATE, SSD_HEAD_DIM), F32)],
        compiler_params=_params(("parallel", "arbitrary")),
        name="ssd",
    )(z, xbc, dt_raw, conv_w, conv_b.reshape(1, cd), padv(dt_bias), padv(a_log),
      jnp.repeat(d_skip, SSD_HEAD_DIM).reshape(1, inner), norm_g.reshape(1, inner))


def _hybrid_mixer(x, h, m, seqlen, idx, w_in, w_out, lam_re, lam_im, log_dt, b_re, b_im, c_re, c_im, s5_d,
                  glu_w, glu_b, conv_w, conv_b, dt_bias, a_log, ssd_d, ssd_norm_g):
    t, d = x.shape
    bsz = t // seqlen
    s5w = s5_d.shape[0]
    inner = ssd_norm_g.shape[0]
    cd = conv_w.shape[1]
    heads = dt_bias.shape[0]
    o1, o2, o3 = s5w, s5w + inner, s5w + inner + cd
    u = _matmul(h, w_in, idx, 0, o1)
    z = _matmul(h, w_in, idx, o1, inner)
    xbc = _matmul(h, w_in, idx, o2, cd)
    dt_raw = _matmul(h, w_in, idx, o3, heads)
    y_s5 = _s5_branch(u, bsz, seqlen, lam_re, lam_im, log_dt, b_re, b_im, c_re, c_im, s5_d, glu_w, glu_b)
    y_ssd = _ssd_branch(z, xbc, dt_raw, bsz, seqlen, conv_w, conv_b, dt_bias, a_log, ssd_d, ssd_norm_g)
    assert inner % s5w == 0
    return _matmul_resid([y_s5] + [(y_ssd, i) for i in range(inner // s5w)], w_out, x, m, seqlen)


def _rwkv_mix_kernel(x_ref, m_ref, g_ref, mu_ref, w1_ref, w2_ref, w0_ref, a1_ref, a2_ref, a0_ref, g1_ref, g2_ref,
                     xr_ref, xk_ref, xv_ref, lw_ref, a_ref, gate_ref, prev_scr):
    h = _adaln(x_ref[...], g_ref[...], m_ref[0])
    tm = h.shape[0]

    @pl.when(pl.program_id(1) == 0)
    def _():
        prev_scr[...] = jnp.zeros_like(prev_scr)

    first = lax.broadcasted_iota(jnp.int32, (tm, 1), 0) == 0
    shifted = jnp.where(first, prev_scr[7:8, :], pltpu.roll(h, 1, axis=0))
    prev_scr[...] = h[tm - 8:tm]
    xx = shifted - h
    mu = mu_ref[...]
    mix = lambda i: h + xx * mu[i:i + 1]
    xr_ref[...] = mix(0).astype(xr_ref.dtype)
    xk_ref[...] = mix(2).astype(xk_ref.dtype)
    xv_ref[...] = mix(3).astype(xv_ref.dtype)
    w = -_softplus(-(w0_ref[...] + _dot(jnp.tanh(_dot(mix(1), w1_ref[...])), w2_ref[...]))) - 0.5
    lw_ref[...] = -jnp.exp(w)
    a_ref[...] = jax.nn.sigmoid(a0_ref[...] + _dot(_dot(mix(4), a1_ref[...]), a2_ref[...]))
    gate_ref[...] = _dot(jax.nn.sigmoid(_dot(mix(5), g1_ref[...])), g2_ref[...])


def _rwkv_mix(x, m, g, mu, w1, w2, w0, a1, a2, a0, g1, g2, bsz, seqlen):
    t, d = x.shape
    tm = min(256, seqlen)
    per = seqlen // tm
    tok = pl.BlockSpec((tm, d), lambda b, i: (b * per + i, 0))
    vec = pl.BlockSpec((1, d), lambda b, i: (0, 0))
    whole = lambda a: pl.BlockSpec(a.shape, lambda b, i: (0, 0))

    def lora(a, b):
        rank = a.shape[1]
        rp = -(-rank // LANES) * LANES
        return (jnp.pad(a, ((0, 0), (0, rp - rank))).astype(BF16),
                jnp.pad(b, ((0, rp - rank), (0, 0))).astype(BF16))

    (w1p, w2p), (a1p, a2p), (g1p, g2p) = lora(w1, w2), lora(a1, a2), lora(g1, g2)
    return pl.pallas_call(
        _rwkv_mix_kernel,
        grid=(bsz, per),
        in_specs=[tok, pl.BlockSpec((1, 3, d), lambda b, i: (b, 0, 0)), vec,
                  pl.BlockSpec((8, d), lambda b, i: (0, 0)),
                  whole(w1p), whole(w2p), vec, whole(a1p), whole(a2p), vec, whole(g1p), whole(g2p)],
        out_specs=[tok] * 6,
        out_shape=[jax.ShapeDtypeStruct((t, d), BF16)] * 3 + [jax.ShapeDtypeStruct((t, d), F32)] * 3,
        scratch_shapes=[pltpu.VMEM((8, d), F32)],
        compiler_params=_params(("parallel", "arbitrary")),
        name="rwkv_mix",
    )(x, m, g.reshape(1, d), jnp.pad(mu, ((0, 2), (0, 0))),
      w1p, w2p, w0.reshape(1, d), a1p, a2p, a0.reshape(1, d), g1p, g2p)


def _rwkv_scan_kernel(r_ref, k_ref, v_ref, lw_ref, a_ref, g_ref, kk_ref, ka_ref, rk_ref, lg_ref, lb_ref,
                      o_ref, s_scr):
    c = RWKV_CHUNK
    hd = RWKV_HEAD_DIM
    c2 = 2 * c
    tokens = r_ref.shape[0]
    pairs = r_ref.shape[1] // LANES
    tb = pairs * tokens
    per_pair = tokens // c
    n_chunks = tb // c

    @pl.when(pl.program_id(2) == 0)
    def _():
        s_scr[...] = jnp.zeros_like(s_scr)

    def rows_of(ref):
        return jnp.concatenate([ref[:, p * LANES:(p + 1) * LANES] for p in range(pairs)], axis=0)

    def param_rows(ref):
        return jnp.concatenate([jnp.broadcast_to(ref[:, p * LANES:(p + 1) * LANES], (tokens, LANES))
                                for p in range(pairs)], axis=0)

    lane_head = lax.broadcasted_iota(jnp.int32, (1, LANES), 1) // hd
    head0 = lane_head == 0
    same_head = ((lax.broadcasted_iota(jnp.int32, (LANES, LANES), 0) // hd)
                 == (lax.broadcasted_iota(jnp.int32, (LANES, LANES), 1) // hd))
    ones_bd = same_head.astype(F32)
    tg = min(tb, 2 * LANES)
    ri = lax.broadcasted_iota(jnp.int32, (tg, tg), 0)
    ci = lax.broadcasted_iota(jnp.int32, (tg, tg), 1)
    tri_b16 = ((ri // c == ci // c) & (ri >= ci)).astype(BF16)
    r2 = lax.broadcasted_iota(jnp.int32, (1, c2, c2), 1)
    q2 = lax.broadcasted_iota(jnp.int32, (1, c2, c2), 2)
    same_blk = (r2 // c) == (q2 // c)
    strict = same_blk & ((r2 % c) > (q2 % c))
    incl = same_blk & ((r2 % c) >= (q2 % c))
    eye2 = (r2 == q2).astype(F32)
    k_k, k_a, r_k = param_rows(kk_ref), param_rows(ka_ref), param_rows(rk_ref)
    ones_b16 = ones_bd.astype(BF16)

    def stack(x):
        x = x.reshape(n_chunks, c, LANES)
        return jnp.concatenate([jnp.where(head0, x, 0.0), jnp.where(head0, 0.0, x)], axis=1)

    def unstack(x):
        return (x[:, :c] + x[:, c:]).reshape(tb, LANES)

    def bdot(p, q, dims):
        return lax.dot_general(p.astype(BF16), q.astype(BF16), (dims, ((0,), (0,))), preferred_element_type=F32)

    bmm = lambda p, q: bdot(p, q, ((2,), (1,)))
    bmm_nt = lambda p, q: bdot(p, q, ((2,), (2,)))
    bmm_tn = lambda p, q: bdot(p, q, ((1,), (1,)))

    def split(x):
        hi_part = x.astype(BF16)
        return hi_part, (x - hi_part.astype(F32)).astype(BF16)

    ones2 = jnp.concatenate([ones_b16, ones_b16], axis=0)
    zeros_b16 = jnp.zeros_like(ones_b16)
    ones_two = jnp.concatenate([jnp.concatenate([ones_b16, zeros_b16], axis=1),
                                jnp.concatenate([zeros_b16, ones_b16], axis=1)], axis=0)

    def head_sum(x):
        return jnp.dot(jnp.concatenate(split(x), axis=1), ones2, preferred_element_type=F32)

    def head_sum_pair(x1, x2):
        out = jnp.dot(jnp.concatenate([x1.astype(BF16), x2.astype(BF16)], axis=1), ones_two,
                      preferred_element_type=F32)
        return out[:, :LANES], out[:, LANES:]

    def cumsum_rows(x):
        outs = []
        for g0 in range(0, tb, tg):
            xg = x[g0:g0 + tg]
            p1, p2 = split(xg)
            p3 =(xg - p1.astype(F32) - p2.astype(F32)).astype(BF16)
            wide = jnp.dot(tri_b16, jnp.concatenate([p1, p2], axis=1), preferred_element_type=F32)
            outs.append(wide[:, :LANES] + wide[:, LANES:] + jnp.dot(tri_b16, p3, preferred_element_type=F32))
        return outs[0] if len(outs) == 1 else jnp.concatenate(outs, axis=0)

    def lhs3(hi_part, lo_part):
        return jnp.concatenate([hi_part, lo_part], axis=2)

    def rhs3(hi_part, lo_part):
        top = jnp.concatenate([hi_part, lo_part], axis=2)
        bot = jnp.concatenate([hi_part, jnp.zeros_like(lo_part)], axis=2)
        return jnp.concatenate([top, bot], axis=1)

    def fold(x):
        return x[:, :, :c2] + x[:, :, c2:]

    r, k, v, lw, a = rows_of(r_ref), rows_of(k_ref), rows_of(v_ref), rows_of(lw_ref), rows_of(a_ref)
    kk = k * k_k
    k2 = k * (1.0 + (a - 1.0) * k_a)
    kk_sq, rk_sum = head_sum_pair(kk * kk, r * k2 * r_k)
    kk = kk / jnp.maximum(jnp.sqrt(kk_sq), 1e-12)
    bv = kk * a
    cum = cumsum_rows(lw)
    cum3 = cum.reshape(n_chunks, c, LANES)
    tot = jnp.broadcast_to(cum3[:, c - 1:c, :], cum3.shape).reshape(tb, LANES)
    g_inv = jnp.exp(-cum)
    rt_u = r * jnp.exp(cum)
    at = stack(-kk * jnp.exp(cum - lw))
    rt = stack(rt_u)
    bt = stack(bv * g_inv)
    kt = stack(k2 * g_inv)
    to_end = jnp.exp(tot - cum)
    b_end = (bv * to_end).reshape(n_chunks, c, LANES)
    k_end = (k2 * to_end).reshape(n_chunks, c, LANES)
    vs = stack(v)

    gram = bmm_nt(jnp.concatenate([at, rt], axis=1), jnp.concatenate([bt, kt], axis=1))
    n_ab = jnp.where(strict, gram[:, :c2, :c2], 0.0)
    a_ak = jnp.where(strict, gram[:, :c2, c2:], 0.0)
    a_rb = jnp.where(incl, gram[:, c2:, :c2], 0.0)
    a_rk = jnp.where(incl, gram[:, c2:, c2:], 0.0)
    pw_h, pw_l = split(n_ab)
    pw = fold(bmm(lhs3(pw_h, pw_l), rhs3(pw_h, pw_l)))
    tinv = eye2 + n_ab
    for _ in range(int(math.log2(c)) - 2):
        pw_h, pw_l = split(pw)
        t_h, t_l = split(tinv)
        both = fold(bmm(jnp.concatenate([lhs3(pw_h, pw_l), lhs3(t_h, t_l)], axis=1), rhs3(pw_h, pw_l)))
        pw = both[:, :c2]
        tinv = tinv + both[:, c2:]
    pw_h, pw_l = split(pw)
    t_h, t_l = split(tinv)
    tinv = tinv + fold(bmm(lhs3(t_h, t_l), rhs3(pw_h, pw_l)))
    pq = bmm(tinv, jnp.concatenate([at, bmm(a_ak, vs)], axis=2))
    ry = bmm(jnp.concatenate([a_rb, a_rk], axis=2),
             jnp.concatenate([pq, jnp.concatenate([jnp.zeros_like(vs), vs], axis=2)], axis=1))
    rq = rt_u + unstack(ry[:, :, :LANES])
    y0 = unstack(ry[:, :, LANES:])
    p_u = unstack(pq[:, :, :LANES]).reshape(n_chunks, c, LANES)
    q_u = unstack(pq[:, :, LANES:]).reshape(n_chunks, c, LANES)
    gm = jnp.where(same_head, bmm_tn(p_u, b_end), 0.0)
    dm = jnp.where(same_head, bmm_tn(jnp.concatenate([q_u, v.reshape(n_chunks, c, LANES)], axis=1),
                                     jnp.concatenate([b_end, k_end], axis=1)), 0.0)
    g_tot = jnp.exp(cum3[:, c - 1:c, :])
    bonus = rk_sum * v

    s = [s_scr[p] for p in range(pairs)]
    ys = [[] for _ in range(pairs)]
    for i in range(per_pair):
        for p in range(pairs):
            n = p * per_pair + i
            ys[p].append(_dot_nt(rq[n * c:(n + 1) * c], s[p]))
            s[p] = s[p] * g_tot[n] + _dot(s[p], gm[n]) + dm[n]
    for p in range(pairs):
        s_scr[p] = s[p]
    y = jnp.concatenate([y_c for y_p in ys for y_c in y_p], axis=0) + y0
    mean = head_sum(y) * (1.0 / hd)
    dy = y - mean
    var = head_sum(dy * dy) * (1.0 / hd)
    yn = dy * lax.rsqrt(var + RWKV_GN_EPS) * param_rows(lg_ref) + param_rows(lb_ref)
    out = ((yn + bonus) * rows_of(g_ref)).astype(o_ref.dtype)
    for p in range(pairs):
        o_ref[:, p * LANES:(p + 1) * LANES] = out[p * tokens:(p + 1) * tokens]


def _rwkv_scan(r, k, v, lw, a, g, k_k, k_a, r_k, ln_g, ln_b, bsz, seqlen):
    t, d = r.shape
    tb = min(RWKV_TOKEN_BLOCK, seqlen)
    per = seqlen // tb
    pairs = RWKV_PAIRS_PER_BLOCK
    bw = pairs * LANES
    tok = pl.BlockSpec((tb, bw), lambda b, h, i: (b * per + i, h))
    row = pl.BlockSpec((1, bw), lambda b, h, i: (0, h))
    vec = lambda p: p.reshape(1, d)
    return pl.pallas_call(
        _rwkv_scan_kernel,
        grid=(bsz, d // bw, per),
        in_specs=[tok] * 6 + [row] * 5,
        out_specs=tok,
        out_shape=jax.ShapeDtypeStruct((t, d), BF16),
        scratch_shapes=[pltpu.VMEM((pairs, LANES, LANES), F32)],
        compiler_params=_params(("parallel", "parallel", "arbitrary")),
        name="rwkv_scan",
    )(r, k, v, lw, a, g, vec(k_k), vec(k_a), vec(r_k), vec(ln_g), vec(ln_b))


def _rwkv_mixer(x, m, g, seqlen, idx, mu, w_r, w_k, w_v, w_o, w0, w1, w2, a0, a1, a2, g1, g2,
                k_k, k_a, r_k, ln_g, ln_b):
    t, d = x.shape
    bsz = t // seqlen
    xr, xk, xv, lw, a, gate = _rwkv_mix(x, m, g, mu, w1, w2, w0, a1, a2, a0, g1, g2, bsz, seqlen)
    r = _matmul(xr, w_r, idx)
    k = _matmul(xk, w_k, idx)
    v = _matmul(xv, w_v, idx)
    y = _rwkv_scan(r, k, v, lw, a, gate, k_k, k_a, r_k.reshape(-1), ln_g, ln_b, bsz, seqlen)
    return _matmul_resid([y], w_o, x, m, seqlen)


def kernel(x, c, w_mod, b_mod, norm_g, ffn_w1, ffn_w3, ffn_w2, hyb_w_in, hyb_w_out, s5_lambda_re, s5_lambda_im, s5_log_dt, s5_b_re, s5_b_im, s5_c_re, s5_c_im, s5_d, s5_glu_w, s5_glu_b, ssd_conv_w, ssd_conv_b, ssd_dt_bias, ssd_a_log, ssd_d, ssd_norm_g, rwkv_mu, rwkv_w_r, rwkv_w_k, rwkv_w_v, rwkv_w_o, rwkv_w0, rwkv_w1, rwkv_w2, rwkv_a0, rwkv_a1, rwkv_a2, rwkv_g1, rwkv_g2, rwkv_k_k, rwkv_k_a, rwkv_r_k, rwkv_ln_g, rwkv_ln_b, final_g):
    bsz, seqlen, d = x.shape
    depth = w_mod.shape[0]
    xf = x.reshape(bsz * seqlen, d)
    mod = _modulation(c, w_mod, b_mod).reshape(depth, bsz, 3, 3, d)
    for layer in range(depth):
        i = layer // 2
        m0, m1, m2 = mod[layer, :, 0], mod[layer, :, 1], mod[layer, :, 2]
        if layer % 2 == 0:
            xf, h = _ffn(xf, m0, norm_g[layer, 0], ffn_w1, ffn_w3, ffn_w2, layer, 0, seqlen,
                         next_ln=(norm_g[layer, 1], m1))
            xf = _hybrid_mixer(xf, h, m1, seqlen, i, hyb_w_in, hyb_w_out[i], s5_lambda_re[i],
                               s5_lambda_im[i], s5_log_dt[i], s5_b_re[i], s5_b_im[i], s5_c_re[i], s5_c_im[i],
                               s5_d[i], s5_glu_w[i], s5_glu_b[i], ssd_conv_w[i], ssd_conv_b[i], ssd_dt_bias[i],
                               ssd_a_log[i], ssd_d[i], ssd_norm_g[i])
        else:
            xf = _ffn(xf, m0, norm_g[layer, 0], ffn_w1, ffn_w3, ffn_w2, layer, 0, seqlen)
            xf = _rwkv_mixer(xf, m1, norm_g[layer, 1], seqlen, i, rwkv_mu[i], rwkv_w_r, rwkv_w_k, rwkv_w_v,
                             rwkv_w_o[i], rwkv_w0[i], rwkv_w1[i], rwkv_w2[i], rwkv_a0[i], rwkv_a1[i], rwkv_a2[i],
                             rwkv_g1[i], rwkv_g2[i], rwkv_k_k[i], rwkv_k_a[i], rwkv_r_k[i], rwkv_ln_g[i],
                             rwkv_ln_b[i])
        xf = _ffn(xf, m2, norm_g[layer, 2], ffn_w1, ffn_w3, ffn_w2, layer, 1, seqlen,
                  final_g=final_g if layer == depth - 1 else None)
    return xf.reshape(bsz, seqlen, d)
```

```python
import functools
import math

import jax
import jax.numpy as jnp
from jax import lax
from jax.experimental import pallas as pl
from jax.experimental.pallas import tpu as pltpu

F32 = jnp.float32
BF16 = jnp.bfloat16
HI = lax.Precision.HIGHEST

RMS_EPS = 1e-6
RWKV_GN_EPS = 64e-5

S5_GROUP_CH = 16
S5_STATE = 64
S5_CHUNK = 16
SSD_HEAD_DIM = 64
SSD_STATE = 128
SSD_GROUPS = 8
SSD_CHUNK = 128
SSD_CONV = 4
RWKV_HEAD_DIM = 64
RWKV_CHUNK = 64
RWKV_TOKEN_BLOCK = 512
RWKV_PAIRS_PER_BLOCK = 2
LANES = 128
VMEM_LIMIT = 56 * 1024 * 1024

def _params(sem, vmem=VMEM_LIMIT):
    return pltpu.CompilerParams(dimension_semantics=sem, vmem_limit_bytes=vmem)


def _dot(a, b, hi=False):
    dn = (((1,), (0,)), ((), ()))
    if hi:
        return lax.dot_general(a, b, dn, precision=HI, preferred_element_type=F32)
    return lax.dot_general(a.astype(BF16), b.astype(BF16), dn, preferred_element_type=F32)


def _dot_nt(a, b, hi=False):
    dn = (((1,), (1,)), ((), ()))
    if hi:
        return lax.dot_general(a, b, dn, precision=HI, preferred_element_type=F32)
    return lax.dot_general(a.astype(BF16), b.astype(BF16), dn, preferred_element_type=F32)


def _dot_tn(a, b):
    dn = (((0,), (0,)), ((), ()))
    return lax.dot_general(a.astype(BF16), b.astype(BF16), dn, preferred_element_type=F32)


def _silu(x):
    return x * jax.nn.sigmoid(x)


def _softplus(x):
    return jnp.maximum(x, 0.0) + jnp.log1p(jnp.exp(-jnp.abs(x)))


def _rms(x, g):
    return x * lax.rsqrt(jnp.mean(x * x, axis=-1, keepdims=True) + RMS_EPS) * g


def _adaln(x, g, m):
    return _rms(x, g) * (1.0 + m[1:2]) + m[0:1]


def _mod_kernel(c_ref, w_ref, b_ref, o_ref):
    c = c_ref[...]
    o_ref[0] = _dot(_silu(c), w_ref[0]) + b_ref[0]


def _modulation(c, w_mod, b_mod):
    depth, d, n = w_mod.shape
    bsz = c.shape[0]
    rows = 8
    cp = jnp.pad(c, ((0, rows - bsz), (0, 0)))
    tn = 1024
    out = pl.pallas_call(
        _mod_kernel,
        grid=(depth, n // tn),
        in_specs=[pl.BlockSpec((rows, d), lambda l, j: (0, 0)),
                  pl.BlockSpec((1, d, tn), lambda l, j: (l, 0, j)),
                  pl.BlockSpec((1, 1, tn), lambda l, j: (l, 0, j))],
        out_specs=pl.BlockSpec((1, rows, tn), lambda l, j: (l, 0, j)),
        out_shape=jax.ShapeDtypeStruct((depth, rows, n), F32),
        compiler_params=_params(("arbitrary", "arbitrary")),
        name="modulation",
    )(cp, w_mod, b_mod.reshape(depth, 1, n))
    return out[:, :bsz]


def _ffn_kernel(x_ref, m_ref, g_ref, w1_hbm, w3_hbm, w2_hbm, *rest, tail, layer, which, tf):
    if tail == "final":
        fg_ref, o_ref, h_scr, w1_buf, w3_buf, w2_buf, sem = rest
    elif tail == "next":
        ng_ref, nm_ref, o_ref, hn_ref, h_scr, w1_buf, w3_buf, w2_buf, sem = rest
    else:
        o_ref, h_scr, w1_buf, w3_buf, w2_buf, sem = rest
    tm = x_ref.shape[0]
    rc = min(256, tm)
    n_tiles = w1_hbm.shape[-1] // tf

    def tile_copies(j, slot):
        cols = pl.ds(pl.multiple_of(j * tf, tf), tf)
        return (pltpu.make_async_copy(w1_hbm.at[layer, which, :, cols], w1_buf.at[slot], sem.at[0, slot]),
                pltpu.make_async_copy(w3_hbm.at[layer, which, :, cols], w3_buf.at[slot], sem.at[1, slot]),
                pltpu.make_async_copy(w2_hbm.at[layer, which, cols, :], w2_buf.at[slot], sem.at[2, slot]))

    for cp in tile_copies(0, 0):
        cp.start()
    for r0 in range(0, tm, rc):
        h_scr[r0:r0 + rc, :] = _adaln(x_ref[r0:r0 + rc, :], g_ref[...], m_ref[0]).astype(BF16)
    o_ref[...] = jnp.zeros_like(o_ref)

    def f_tile(j, slot):
        for cp in tile_copies(j, slot):
            cp.wait()

        @pl.when(j + 1 < n_tiles)
        def _():
            for cp in tile_copies(j + 1, 1 - slot):
                cp.start()

        h = h_scr[...]
        a = _dot(h, w1_buf[slot])
        b = _dot(h, w3_buf[slot])
        o_ref[...] += _dot(_silu(a) * b, w2_buf[slot])

    def two_tiles(jj, carry):
        f_tile(2 * jj, 0)
        f_tile(2 * jj + 1, 1)
        return carry

    lax.fori_loop(0, n_tiles // 2, two_tiles, 0)

    for r0 in range(0, tm, rc):
        o = x_ref[r0:r0 + rc, :] + (0.5 * m_ref[0][2:3]) * o_ref[r0:r0 + rc, :]
        if tail == "final":
            o = _rms(o, fg_ref[...])
        o_ref[r0:r0 + rc, :] = o
        if tail == "next":
            hn_ref[r0:r0 + rc, :] = _adaln(o, ng_ref[...], nm_ref[0]).astype(hn_ref.dtype)


def _ffn(x, m, g, w1, w3, w2, layer, which, seqlen, final_g=None, next_ln=None):
    t, d = x.shape
    f = w1.shape[-1]
    tm, tf = min(1024, seqlen), 256
    assert f % (2 * tf) == 0
    per = seqlen // tm
    tail = "final" if final_g is not None else "next" if next_ln is not None else "plain"
    vec = pl.BlockSpec((1, d), lambda i: (0, 0))
    mod = pl.BlockSpec((1, 3, d), lambda i: (i // per, 0, 0))
    tok = pl.BlockSpec((tm, d), lambda i: (i, 0))
    hbm = pl.BlockSpec(memory_space=pl.ANY)
    in_specs = [pl.BlockSpec((tm, d), lambda i: (i, 0), pipeline_mode=pl.Buffered(1)), mod, vec, hbm, hbm, hbm]
    args = [x, m, g.reshape(1, d), w1, w3, w2]
    out_specs, out_shape = tok, jax.ShapeDtypeStruct((t, d), F32)
    if tail == "final":
        in_specs.append(vec)
        args.append(final_g.reshape(1, d))
    elif tail == "next":
        in_specs += [vec, mod]
        args += [next_ln[0].reshape(1, d), next_ln[1]]
        out_specs, out_shape = [tok, tok], [out_shape, jax.ShapeDtypeStruct((t, d), BF16)]
    return pl.pallas_call(
        functools.partial(_ffn_kernel, tail=tail, layer=layer, which=which, tf=tf),
        grid=(t // tm,),
        in_specs=in_specs,
        out_specs=out_specs,
        out_shape=out_shape,
        scratch_shapes=[pltpu.VMEM((tm, d), BF16),
                        pltpu.VMEM((2, d, tf), F32), pltpu.VMEM((2, d, tf), F32), pltpu.VMEM((2, tf, d), F32),
                        pltpu.SemaphoreType.DMA((3, 2))],
        compiler_params=_params(("arbitrary",)),
        name="ffn",
    )(*args)


def _matmul_kernel(a_ref, w_ref, o_ref, w_scr, *, valid, out_major):
    @pl.when(pl.program_id(1) == 0)
    def _():
        w = w_ref[...]
        n_axis = 0 if out_major else 1
        if valid < w.shape[n_axis]:
            w = jnp.where(lax.broadcasted_iota(jnp.int32, w.shape, n_axis) < valid, w, 0.0)
        w_scr[...] = w.astype(BF16)

    dot = _dot_nt if out_major else _dot
    o_ref[...] = dot(a_ref[...], w_scr[...]).astype(o_ref.dtype)


def _matmul(a, w, lead, col0=0, n=None, out_dtype=F32, out_major=False):
    m, k = a.shape
    n_total = w.shape[1] if out_major else w.shape[2]
    n = n_total - col0 if n is None else n
    n_pad = -(-n // LANES) * LANES
    tm = min(1024, m)
    tn = min(1024, n_pad)
    assert col0 % tn == 0 and n_pad % tn == 0
    cb = col0 // tn
    if out_major:
        w_spec = pl.BlockSpec((None, tn, k), lambda j, i: (lead, cb + j, 0))
        w_tile = (tn, k)
    else:
        w_spec = pl.BlockSpec((None, k, tn), lambda j, i: (lead, 0, cb + j))
        w_tile = (k, tn)
    return pl.pallas_call(
        functools.partial(_matmul_kernel, valid=min(tn, n), out_major=out_major),
        grid=(n_pad // tn, m // tm),
        in_specs=[pl.BlockSpec((tm, k), lambda j, i: (i, 0)), w_spec],
        out_specs=pl.BlockSpec((tm, tn), lambda j, i: (i, j)),
        out_shape=jax.ShapeDtypeStruct((m, n_pad), out_dtype),
        scratch_shapes=[pltpu.VMEM(w_tile, BF16)],
        compiler_params=_params(("parallel", "arbitrary")),
        name="matmul",
    )(a, w)


def _matmul_resid_kernel(*refs, n_in):
    a_refs = refs[:n_in]
    w_refs = refs[n_in:2 * n_in]
    x_ref, m_ref, o_ref = refs[2 * n_in:2 * n_in + 3]
    w_scrs = refs[2 * n_in + 3:]

    @pl.when(pl.program_id(1) == 0)
    def _():
        for w_ref, w_scr in zip(w_refs, w_scrs):
            w_scr[...] = w_ref[...].astype(BF16)

    acc = _dot(a_refs[0][...], w_scrs[0][...])
    for a_ref, w_scr in zip(a_refs[1:], w_scrs[1:]):
        acc = acc + _dot(a_ref[...], w_scr[...])
    o_ref[...] = x_ref[...] + m_ref[0][2:3] * acc


def _matmul_resid(a_list, w, x, m, seqlen):
    t, d = x.shape
    tm, tn = min(512, seqlen), 1024
    per = seqlen // tm
    n_in = len(a_list)
    kb = w.shape[0] // n_in
    arrays, in_specs = [], []
    for a in a_list:
        arr, blk = a if isinstance(a, tuple) else (a, 0)
        arrays.append(arr)
        in_specs.append(pl.BlockSpec((tm, kb), lambda j, i, blk=blk: (i, blk)))
    in_specs += [pl.BlockSpec((kb, tn), lambda j, i, r=r: (r, j)) for r in range(n_in)]
    in_specs += [pl.BlockSpec((tm, tn), lambda j, i: (i, j)),
                 pl.BlockSpec((1, 3, tn), lambda j, i: (i // per, 0, j))]
    return pl.pallas_call(
        functools.partial(_matmul_resid_kernel, n_in=n_in),
        grid=(d // tn, t // tm),
        in_specs=in_specs,
        out_specs=pl.BlockSpec((tm, tn), lambda j, i: (i, j)),
        out_shape=jax.ShapeDtypeStruct((t, d), F32),
        scratch_shapes=[pltpu.VMEM((kb, tn), BF16)] * n_in,
        compiler_params=_params(("parallel", "arbitrary")),
        name="matmul_resid",
    )(*arrays, *([w] * n_in), x, m)


def _s5_kernel(u_ref, lrg_ref, lig_ref, ldg_ref, lrl_ref, lil_ref, ldl_ref, btr_ref, bti_ref, cr_ref, ci_ref,
               o_ref, *, nc, levels):
    q, gc, p = S5_CHUNK, S5_GROUP_CH, S5_STATE
    ng = LANES // gc
    gp = ng * p
    rows = u_ref.shape[0] // q
    half = q // 2

    def make_pow(lre, lim, ldt):
        lr = jnp.minimum(lre, -1e-4)
        dt = jnp.exp(ldt)

        def lam_pow(mult):
            mag = jnp.exp(lr * dt * mult)
            ang = lim * dt * mult
            return mag * jnp.cos(ang), mag * jnp.sin(ang)

        lb_re, lb_im = lam_pow(1.0)
        den = lr * lr + lim * lim
        nr, ni = lb_re - 1.0, lb_im
        return lam_pow, (nr * lr + ni * lim) / den, (ni * lr - nr * lim) / den

    pow_g, f_re_g, f_im_g = make_pow(lrg_ref[...], lig_ref[...], ldg_ref[...])
    pow_l, f_re_l, f_im_l = make_pow(lrl_ref[0], lil_ref[0], ldl_ref[0])

    ri = lax.broadcasted_iota(jnp.int32, (LANES, ng), 0)
    ci_ = lax.broadcasted_iota(jnp.int32, (LANES, ng), 1)
    to_rows = (ri // gc == ci_).astype(F32)
    expand = lambda a: _dot(to_rows, a, hi=True)
    pi_ = lax.broadcasted_iota(jnp.int32, (p, gp), 0)
    pj_ = lax.broadcasted_iota(jnp.int32, (p, gp), 1)
    to_lanes = (pi_ == pj_ % p).astype(F32)
    tile = lambda a: _dot(a, to_lanes, hi=True)
    same_rc = (lax.broadcasted_iota(jnp.int32, (LANES, LANES), 0) // gc
               == lax.broadcasted_iota(jnp.int32, (LANES, LANES), 1) // gc)
    same_rl = (lax.broadcasted_iota(jnp.int32, (LANES, gp), 0) // gc
               == lax.broadcasted_iota(jnp.int32, (LANES, gp), 1) // p)

    btr, bti = btr_ref[...], bti_ref[...]
    cr, ci = cr_ref[...], ci_ref[...]
    f_re_r, f_im_r = expand(f_re_g), expand(f_im_g)
    bb_re = f_re_r * btr - f_im_r * bti
    bb_im = f_re_r * bti + f_im_r * btr

    w = []
    for tau in range(q):
        pr, pi = pow_g(float(tau))
        pr, pi = expand(pr), expand(pi)
        k_t = (_dot_nt(bb_re, cr * pr - ci * pi, hi=True) - _dot_nt(bb_im, cr * pi + ci * pr, hi=True))
        w.append(jnp.where(same_rc, k_t, 0.0).astype(BF16))
    zero = jnp.zeros((LANES, LANES), BF16)

    def w_pair(delta):
        lo = w[2 * delta - 1] if delta > 0 else zero
        return jnp.concatenate([jnp.concatenate([w[2 * delta], w[2 * delta + 1]], axis=1),
                                jnp.concatenate([lo, w[2 * delta]], axis=1)], axis=0)

    u_pair = []
    for jp in range(half):
        u_pair.append(jnp.concatenate(
            [u_ref[pl.ds(2 * jp + jj, rows, stride=q), :].astype(BF16) for jj in range(2)], axis=1))

    bt_re, bt_im = tile(btr), tile(bti)
    bbl_re = jnp.where(same_rl, f_re_l * bt_re - f_im_l * bt_im, 0.0)
    bbl_im = jnp.where(same_rl, f_re_l * bt_im + f_im_l * bt_re, 0.0)
    st = None
    for jp in range(half):
        blocks = []
        for j in (2 * jp, 2 * jp + 1):
            pr, pi = pow_l(float(q - 1 - j))
            blocks.append(jnp.concatenate([bbl_re * pr - bbl_im * pi, bbl_re * pi + bbl_im * pr], axis=1))
        term = _dot(u_pair[jp], jnp.concatenate(blocks, axis=0))
        st = term if st is None else st + term
    sr, si = st[:, :gp], st[:, gp:]

    cidx = lax.broadcasted_iota(jnp.int32, (rows, 1), 0) % nc
    for k in range(levels):
        sh = 1 << k
        ar, ai = pow_l(float(q * sh))
        xr = pltpu.roll(sr, sh, axis=0)
        xi = pltpu.roll(si, sh, axis=0)
        ok = cidx >= sh
        sr, si = (sr + jnp.where(ok, ar * xr - ai * xi, 0.0),
                  si + jnp.where(ok, ar * xi + ai * xr, 0.0))
    ok = cidx >= 1
    s_prev = jnp.concatenate([jnp.where(ok, pltpu.roll(sr, 1, axis=0), 0.0),
                              jnp.where(ok, pltpu.roll(si, 1, axis=0), 0.0)], axis=1).astype(BF16)

    ct_re, ct_im = tile(cr), tile(ci)
    ct_re = jnp.where(same_rl, ct_re, 0.0)
    ct_im = jnp.where(same_rl, ct_im, 0.0)
    for ip in range(half):
        acc = None
        for jp in range(ip + 1):
            term = _dot(u_pair[jp], w_pair(ip - jp))
            acc = term if acc is None else acc + term
        blocks = []
        for i in (2 * ip, 2 * ip + 1):
            pr, pi = pow_l(float(i + 1))
            blocks.append(jnp.concatenate([ct_re * pr - ct_im * pi, -(ct_re * pi + ct_im * pr)], axis=1))
        acc = acc + _dot_nt(s_prev, jnp.concatenate(blocks, axis=0))
        o_ref[pl.ds(2 * ip, rows, stride=q), :] = acc[:, :LANES]
        o_ref[pl.ds(2 * ip + 1, rows, stride=q), :] = acc[:, LANES:]


def _s5_post_kernel(y_ref, u_ref, d_ref, w_ref, b_ref, o_ref):
    y = y_ref[...] + d_ref[...] * u_ref[...]
    y = jax.nn.gelu(y)
    o_ref[...] = (y * jax.nn.sigmoid(_dot(y, w_ref[...]) + b_ref[...])).astype(o_ref.dtype)


def _s5_branch(u, bsz, seqlen, lam_re, lam_im, log_dt, b_re, b_im, c_re, c_im, d_skip, glu_w, glu_b):
    t, width = u.shape
    q, gc, p = S5_CHUNK, S5_GROUP_CH, S5_STATE
    groups = width // gc
    nc = seqlen // q
    levels = int(math.log2(nc))
    assert (1 << levels) == nc
    ng = LANES // gc
    nblk = width // LANES
    gp = ng * p
    ldt = jnp.broadcast_to(log_dt[:, None], (groups, p))
    by_group = pl.BlockSpec((ng, p), lambda k: (k, 0))
    by_lane = pl.BlockSpec((1, 1, gp), lambda k: (k, 0, 0))
    by_chan = pl.BlockSpec((LANES, p), lambda k: (k, 0))
    tok = pl.BlockSpec((t, LANES), lambda k: (0, k))
    lanes3 = lambda a: a.reshape(nblk, 1, gp)
    chan2 = lambda a: a.reshape(width, p)
    y = pl.pallas_call(
        functools.partial(_s5_kernel, nc=nc, levels=levels),
        grid=(nblk,),
        in_specs=[tok, by_group, by_group, by_group, by_lane, by_lane, by_lane,
                  by_chan, by_chan, by_chan, by_chan],
        out_specs=tok,
        out_shape=jax.ShapeDtypeStruct((t, width), F32),
        compiler_params=_params(("parallel",)),
        name="s5_conv",
    )(u, lam_re, lam_im, ldt, lanes3(lam_re), lanes3(lam_im), lanes3(ldt),
      chan2(jnp.swapaxes(b_re, 1, 2)), chan2(jnp.swapaxes(b_im, 1, 2)), chan2(c_re), chan2(c_im))

    tm = min(512, t)
    return pl.pallas_call(
        _s5_post_kernel,
        grid=(t // tm,),
        in_specs=[pl.BlockSpec((tm, width), lambda i: (i, 0)),
                  pl.BlockSpec((tm, width), lambda i: (i, 0)),
                  pl.BlockSpec((1, width), lambda i: (0, 0)),
                  pl.BlockSpec((width, width), lambda i: (0, 0)),
                  pl.BlockSpec((1, width), lambda i: (0, 0))],
        out_specs=pl.BlockSpec((tm, width), lambda i: (i, 0)),
        out_shape=jax.ShapeDtypeStruct((t, width), BF16),
        compiler_params=_params(("parallel",)),
        name="s5_post",
    )(y, u, d_skip.reshape(1, width), glu_w.astype(BF16), glu_b.reshape(1, width))


def _ssd_kernel(z_ref, xbc_ref, dt_ref, cw_ref, cb_ref, dtb_ref, alog_ref, dsk_ref, ng_ref,
                o_ref, ext_scr, st_scr, *, heads, inner):
    lc = SSD_CHUNK
    hd, ns = SSD_HEAD_DIM, SSD_STATE
    rpg = heads // SSD_GROUPS
    gw = SSD_GROUPS * ns
    tail = SSD_CONV - 1

    @pl.when(pl.program_id(1) == 0)
    def _():
        ext_scr[0:8, :] = jnp.zeros((8, ext_scr.shape[1]), F32)
        st_scr[...] = jnp.zeros_like(st_scr)

    ext_scr[8:8 + lc, :] = xbc_ref[...]
    cw = cw_ref[...]
    conv = cb_ref[...] + cw[0:1] * ext_scr[pl.ds(8 - tail, lc), :]
    for k in range(1, SSD_CONV):
        conv = conv + cw[k:k + 1] * ext_scr[pl.ds(8 - tail + k, lc), :]
    ext_scr[0:8, :] = ext_scr[lc:lc + 8, :]
    act = _silu(conv)
    xs = act[:, :inner]
    bs = act[:, inner:inner + gw]
    cs = act[:, inner + gw:]

    dt = _softplus(dt_ref[...] + dtb_ref[...])
    adt = dt * (-jnp.exp(alog_ref[...]))
    ri = lax.broadcasted_iota(jnp.int32, (lc, lc), 0)
    ci = lax.broadcasted_iota(jnp.int32, (lc, lc), 1)
    causal = ri >= ci
    a_cum = _dot(causal.astype(F32), adt, hi=True)
    a_cum_t = a_cum.T
    dt_t = dt.T

    y_parts = []
    for g in range(SSD_GROUPS):
        cs_g = cs[:, g * ns:(g + 1) * ns]
        bs_g = bs[:, g * ns:(g + 1) * ns]
        cb = _dot_nt(cs_g, bs_g)
        bs_gt = bs_g.T
        for r in range(rpg):
            h = g * rpg + r
            col = a_cum[:, h:h + 1]
            row = a_cum_t[h:h + 1, :]
            dt_row = dt_t[h:h + 1, :]
            decay = jnp.exp(jnp.where(causal, col - row, -jnp.inf))
            xs_h = xs[:, h * hd:(h + 1) * hd]
            st = st_scr[h]
            lhs = jnp.concatenate([cb * decay * dt_row, cs_g * jnp.exp(col)], axis=1)
            y_parts.append(_dot(lhs, jnp.concatenate([xs_h, st], axis=0)))
            last = row[:, lc - 1:lc]
            to_end = jnp.exp(last - row)
            st_scr[h] = jnp.exp(last) * st + _dot(bs_gt * (to_end * dt_row), xs_h)
    y = jnp.concatenate(y_parts, axis=1) + dsk_ref[...] * xs
    y = y * _silu(z_ref[...])
    o_ref[...] = _rms(y, ng_ref[...]).astype(o_ref.dtype)


def _ssd_branch(z, xbc, dt_raw, bsz, seqlen, conv_w, conv_b, dt_bias, a_log, d_skip, norm_g):
    t, inner = z.shape
    cd = xbc.shape[1]
    heads = inner // SSD_HEAD_DIM
    lc = SSD_CHUNK
    nc = seqlen // lc
    pad = LANES - heads
    padv = lambda v: jnp.pad(v.reshape(1, heads), ((0, 0), (0, pad)))
    row = lambda n: pl.BlockSpec((1, n), lambda b, c: (0, 0))
    tok = lambda n: pl.BlockSpec((lc, n), lambda b, c: (b * nc + c, 0))
    return pl.pallas_call(
        functools.partial(_ssd_kernel, heads=heads, inner=inner),
        grid=(bsz, nc),
        in_specs=[tok(inner), tok(cd), tok(LANES),
                  pl.BlockSpec((SSD_CONV, cd), lambda b, c: (0, 0)), row(cd),
                  row(LANES), row(LANES), row(inner), row(inner)],
        out_specs=tok(inner),
        out_shape=jax.ShapeDtypeStruct((t, inner), BF16),
        scratch_shapes=[pltpu.VMEM((8 + lc + 8, cd), F32),
                        pltpu.VMEM((heads, SSD_STATE, SSD_HEAD_DIM), F32)],
        compiler_params=_params(("parallel", "arbitrary")),
        name="ssd",
    )(z, xbc, dt_raw, conv_w, conv_b.reshape(1, cd), padv(dt_bias), padv(a_log),
      jnp.repeat(d_skip, SSD_HEAD_DIM).reshape(1, inner), norm_g.reshape(1, inner))


def _hybrid_mixer(x, h, m, seqlen, idx, w_in, w_out, lam_re, lam_im, log_dt, b_re, b_im, c_re, c_im, s5_d,
                  glu_w, glu_b, conv_w, conv_b, dt_bias, a_log, ssd_d, ssd_norm_g):
    t, d = x.shape
    bsz = t // seqlen
    s5w = s5_d.shape[0]
    inner = ssd_norm_g.shape[0]
    cd = conv_w.shape[1]
    heads = dt_bias.shape[0]
    o1, o2, o3 = s5w, s5w + inner, s5w + inner + cd
    w_t = jnp.swapaxes(w_in, 1, 2)
    u = _matmul(h, w_t, idx, 0, o1, out_major=True)
    z = _matmul(h, w_t, idx, o1, inner, out_major=True)
    xbc = _matmul(h, w_t, idx, o2, cd, out_major=True)
    dt_raw = _matmul(h, w_t, idx, o3, heads, out_major=True)
    y_s5 = _s5_branch(u, bsz, seqlen, lam_re, lam_im, log_dt, b_re, b_im, c_re, c_im, s5_d, glu_w, glu_b)
    y_ssd = _ssd_branch(z, xbc, dt_raw, bsz, seqlen, conv_w, conv_b, dt_bias, a_log, ssd_d, ssd_norm_g)
    assert inner % s5w == 0
    return _matmul_resid([y_s5] + [(y_ssd, i) for i in range(inner // s5w)], w_out, x, m, seqlen)


def _rwkv_mix_kernel(x_ref, m_ref, g_ref, mu_ref, w1_ref, w2_ref, w0_ref, a1_ref, a2_ref, a0_ref, g1_ref, g2_ref,
                     xr_ref, xk_ref, xv_ref, lw_ref, a_ref, gate_ref, prev_scr):
    h = _adaln(x_ref[...], g_ref[...], m_ref[0])
    tm = h.shape[0]

    @pl.when(pl.program_id(1) == 0)
    def _():
        prev_scr[...] = jnp.zeros_like(prev_scr)

    first = lax.broadcasted_iota(jnp.int32, (tm, 1), 0) == 0
    shifted = jnp.where(first, prev_scr[7:8, :], pltpu.roll(h, 1, axis=0))
    prev_scr[...] = h[tm - 8:tm]
    xx = shifted - h
    mu = mu_ref[...]
    mix = lambda i: h + xx * mu[i:i + 1]
    xr_ref[...] = mix(0).astype(xr_ref.dtype)
    xk_ref[...] = mix(2).astype(xk_ref.dtype)
    xv_ref[...] = mix(3).astype(xv_ref.dtype)
    w = -_softplus(-(w0_ref[...] + _dot(jnp.tanh(_dot(mix(1), w1_ref[...])), w2_ref[...]))) - 0.5
    lw_ref[...] = -jnp.exp(w)
    a_ref[...] = jax.nn.sigmoid(a0_ref[...] + _dot(_dot(mix(4), a1_ref[...]), a2_ref[...]))
    gate_ref[...] = _dot(jax.nn.sigmoid(_dot(mix(5), g1_ref[...])), g2_ref[...])


def _rwkv_mix(x, m, g, mu, w1, w2, w0, a1, a2, a0, g1, g2, bsz, seqlen):
    t, d = x.shape
    tm = min(256, seqlen)
    per = seqlen // tm
    tok = pl.BlockSpec((tm, d), lambda b, i: (b * per + i, 0))
    vec = pl.BlockSpec((1, d), lambda b, i: (0, 0))
    whole = lambda a: pl.BlockSpec(a.shape, lambda b, i: (0, 0))

    def lora(a, b):
        rank = a.shape[1]
        rp = -(-rank // LANES) * LANES
        return (jnp.pad(a, ((0, 0), (0, rp - rank))).astype(BF16),
                jnp.pad(b, ((0, rp - rank), (0, 0))).astype(BF16))

    (w1p, w2p), (a1p, a2p), (g1p, g2p) = lora(w1, w2), lora(a1, a2), lora(g1, g2)
    return pl.pallas_call(
        _rwkv_mix_kernel,
        grid=(bsz, per),
        in_specs=[tok, pl.BlockSpec((1, 3, d), lambda b, i: (b, 0, 0)), vec,
                  pl.BlockSpec((8, d), lambda b, i: (0, 0)),
                  whole(w1p), whole(w2p), vec, whole(a1p), whole(a2p), vec, whole(g1p), whole(g2p)],
        out_specs=[tok] * 6,
        out_shape=[jax.ShapeDtypeStruct((t, d), BF16)] * 3 + [jax.ShapeDtypeStruct((t, d), F32)] * 3,
        scratch_shapes=[pltpu.VMEM((8, d), F32)],
        compiler_params=_params(("parallel", "arbitrary")),
        name="rwkv_mix",
    )(x, m, g.reshape(1, d), jnp.pad(mu, ((0, 2), (0, 0))),
      w1p, w2p, w0.reshape(1, d), a1p, a2p, a0.reshape(1, d), g1p, g2p)


def _rwkv_scan_kernel(r_ref, k_ref, v_ref, lw_ref, a_ref, g_ref, kk_ref, ka_ref, rk_ref, lg_ref, lb_ref,
                      o_ref, s_scr):
    c = RWKV_CHUNK
    hd = RWKV_HEAD_DIM
    c2 = 2 * c
    tokens = r_ref.shape[0]
    pairs = r_ref.shape[1] // LANES
    tb = pairs * tokens
    per_pair = tokens // c
    n_chunks = tb // c

    @pl.when(pl.program_id(2) == 0)
    def _():
        s_scr[...] = jnp.zeros_like(s_scr)

    def rows_of(ref):
        return jnp.concatenate([ref[:, p * LANES:(p + 1) * LANES] for p in range(pairs)], axis=0)

    def param_rows(ref):
        return jnp.concatenate([jnp.broadcast_to(ref[:, p * LANES:(p + 1) * LANES], (tokens, LANES))
                                for p in range(pairs)], axis=0)

    lane_head = lax.broadcasted_iota(jnp.int32, (1, LANES), 1) // hd
    head0 = lane_head == 0
    same_head = ((lax.broadcasted_iota(jnp.int32, (LANES, LANES), 0) // hd)
                 == (lax.broadcasted_iota(jnp.int32, (LANES, LANES), 1) // hd))
    ones_bd = same_head.astype(F32)
    tg = min(tb, 2 * LANES)
    ri = lax.broadcasted_iota(jnp.int32, (tg, tg), 0)
    ci = lax.broadcasted_iota(jnp.int32, (tg, tg), 1)
    tri_b16 = ((ri // c == ci // c) & (ri >= ci)).astype(BF16)
    r2 = lax.broadcasted_iota(jnp.int32, (1, c2, c2), 1)
    q2 = lax.broadcasted_iota(jnp.int32, (1, c2, c2), 2)
    same_blk = (r2 // c) == (q2 // c)
    strict = same_blk & ((r2 % c) > (q2 % c))
    incl = same_blk & ((r2 % c) >= (q2 % c))
    eye2 = (r2 == q2).astype(F32)
    k_k, k_a, r_k = param_rows(kk_ref), param_rows(ka_ref), param_rows(rk_ref)
    ones_b16 = ones_bd.astype(BF16)

    def stack(x):
        x = x.reshape(n_chunks, c, LANES)
        return jnp.concatenate([jnp.where(head0, x, 0.0), jnp.where(head0, 0.0, x)], axis=1)

    def unstack(x):
        return (x[:, :c] + x[:, c:]).reshape(tb, LANES)

    def bdot(p, q, dims):
        return lax.dot_general(p.astype(BF16), q.astype(BF16), (dims, ((0,), (0,))), preferred_element_type=F32)

    bmm = lambda p, q: bdot(p, q, ((2,), (1,)))
    bmm_nt = lambda p, q: bdot(p, q, ((2,), (2,)))
    bmm_tn = lambda p, q: bdot(p, q, ((1,), (1,)))

    def split(x):
        hi_part = x.astype(BF16)
        return hi_part, (x - hi_part.astype(F32)).astype(BF16)

    ones2 = jnp.concatenate([ones_b16, ones_b16], axis=0)
    zeros_b16 = jnp.zeros_like(ones_b16)
    ones_two = jnp.concatenate([jnp.concatenate([ones_b16, zeros_b16], axis=1),
                                jnp.concatenate([zeros_b16, ones_b16], axis=1)], axis=0)

    def head_sum(x):
        return jnp.dot(jnp.concatenate(split(x), axis=1), ones2, preferred_element_type=F32)

    def head_sum_pair(x1, x2):
        out = jnp.dot(jnp.concatenate([x1.astype(BF16), x2.astype(BF16)], axis=1), ones_two,
                      preferred_element_type=F32)
        return out[:, :LANES], out[:, LANES:]

    def cumsum_rows(x):
        outs = []
        for g0 in range(0, tb, tg):
            xg = x[g0:g0 + tg]
            p1, p2 = split(xg)
            p3 =(xg - p1.astype(F32) - p2.astype(F32)).astype(BF16)
            wide = jnp.dot(tri_b16, jnp.concatenate([p1, p2], axis=1), preferred_element_type=F32)
            outs.append(wide[:, :LANES] + wide[:, LANES:] + jnp.dot(tri_b16, p3, preferred_element_type=F32))
        return outs[0] if len(outs) == 1 else jnp.concatenate(outs, axis=0)

    def lhs3(hi_part, lo_part):
        return jnp.concatenate([hi_part, lo_part], axis=2)

    def rhs3(hi_part, lo_part):
        top = jnp.concatenate([hi_part, lo_part], axis=2)
        bot = jnp.concatenate([hi_part, jnp.zeros_like(lo_part)], axis=2)
        return jnp.concatenate([top, bot], axis=1)

    def fold(x):
        return x[:, :, :c2] + x[:, :, c2:]

    r, k, v, lw, a = rows_of(r_ref), rows_of(k_ref), rows_of(v_ref), rows_of(lw_ref), rows_of(a_ref)
    kk = k * k_k
    k2 = k * (1.0 + (a - 1.0) * k_a)
    kk_sq, rk_sum = head_sum_pair(kk * kk, r * k2 * r_k)
    kk = kk / jnp.maximum(jnp.sqrt(kk_sq), 1e-12)
    bv = kk * a
    cum = cumsum_rows(lw)
    cum3 = cum.reshape(n_chunks, c, LANES)
    tot = jnp.broadcast_to(cum3[:, c - 1:c, :], cum3.shape).reshape(tb, LANES)
    g_inv = jnp.exp(-cum)
    rt_u = r * jnp.exp(cum)
    at = stack(-kk * jnp.exp(cum - lw))
    rt = stack(rt_u)
    bt = stack(bv * g_inv)
    kt = stack(k2 * g_inv)
    to_end = jnp.exp(tot - cum)
    b_end = (bv * to_end).reshape(n_chunks, c, LANES)
    k_end = (k2 * to_end).reshape(n_chunks, c, LANES)
    vs = stack(v)

    gram = bmm_nt(jnp.concatenate([at, rt], axis=1), jnp.concatenate([bt, kt], axis=1))
    n_ab = jnp.where(strict, gram[:, :c2, :c2], 0.0)
    a_ak = jnp.where(strict, gram[:, :c2, c2:], 0.0)
    a_rb = jnp.where(incl, gram[:, c2:, :c2], 0.0)
    a_rk = jnp.where(incl, gram[:, c2:, c2:], 0.0)
    pw_h, pw_l = split(n_ab)
    pw = fold(bmm(lhs3(pw_h, pw_l), rhs3(pw_h, pw_l)))
    tinv = eye2 + n_ab
    for _ in range(int(math.log2(c)) - 2):
        pw_h, pw_l = split(pw)
        t_h, t_l = split(tinv)
        both = fold(bmm(jnp.concatenate([lhs3(pw_h, pw_l), lhs3(t_h, t_l)], axis=1), rhs3(pw_h, pw_l)))
        pw = both[:, :c2]
        tinv = tinv + both[:, c2:]
    pw_h, pw_l = split(pw)
    t_h, t_l = split(tinv)
    tinv = tinv + fold(bmm(lhs3(t_h, t_l), rhs3(pw_h, pw_l)))
    pq = bmm(tinv, jnp.concatenate([at, bmm(a_ak, vs)], axis=2))
    ry = bmm(jnp.concatenate([a_rb, a_rk], axis=2),
             jnp.concatenate([pq, jnp.concatenate([jnp.zeros_like(vs), vs], axis=2)], axis=1))
    rq = rt_u + unstack(ry[:, :, :LANES])
    y0 = unstack(ry[:, :, LANES:])
    p_u = unstack(pq[:, :, :LANES]).reshape(n_chunks, c, LANES)
    q_u = unstack(pq[:, :, LANES:]).reshape(n_chunks, c, LANES)
    gm = jnp.where(same_head, bmm_tn(p_u, b_end), 0.0)
    dm = jnp.where(same_head, bmm_tn(jnp.concatenate([q_u, v.reshape(n_chunks, c, LANES)], axis=1),
                                     jnp.concatenate([b_end, k_end], axis=1)), 0.0)
    g_tot = jnp.exp(cum3[:, c - 1:c, :])
    bonus = rk_sum * v

    s = [s_scr[p] for p in range(pairs)]
    ys = [[] for _ in range(pairs)]
    for i in range(per_pair):
        for p in range(pairs):
            n = p * per_pair + i
            ys[p].append(_dot_nt(rq[n * c:(n + 1) * c], s[p]))
            s[p] = s[p] * g_tot[n] + _dot(s[p], gm[n]) + dm[n]
    for p in range(pairs):
        s_scr[p] = s[p]
    y = jnp.concatenate([y_c for y_p in ys for y_c in y_p], axis=0) + y0
    mean = head_sum(y) * (1.0 / hd)
    dy = y - mean
    var = head_sum(dy * dy) * (1.0 / hd)
    yn = dy * lax.rsqrt(var + RWKV_GN_EPS) * param_rows(lg_ref) + param_rows(lb_ref)
    out = ((yn + bonus) * rows_of(g_ref)).astype(o_ref.dtype)
    for p in range(pairs):
        o_ref[:, p * LANES:(p + 1) * LANES] = out[p * tokens:(p + 1) * tokens]


def _rwkv_scan(r, k, v, lw, a, g, k_k, k_a, r_k, ln_g, ln_b, bsz, seqlen):
    t, d = r.shape
    tb = min(RWKV_TOKEN_BLOCK, seqlen)
    per = seqlen // tb
    pairs = RWKV_PAIRS_PER_BLOCK
    bw = pairs * LANES
    tok = pl.BlockSpec((tb, bw), lambda b, h, i: (b * per + i, h))
    row = pl.BlockSpec((1, bw), lambda b, h, i: (0, h))
    vec = lambda p: p.reshape(1, d)
    return pl.pallas_call(
        _rwkv_scan_kernel,
        grid=(bsz, d // bw, per),
        in_specs=[tok] * 6 + [row] * 5,
        out_specs=tok,
        out_shape=jax.ShapeDtypeStruct((t, d), BF16),
        scratch_shapes=[pltpu.VMEM((pairs, LANES, LANES), F32)],
        compiler_params=_params(("parallel", "parallel", "arbitrary")),
        name="rwkv_scan",
    )(r, k, v, lw, a, g, vec(k_k), vec(k_a), vec(r_k), vec(ln_g), vec(ln_b))


def _rwkv_mixer(x, m, g, seqlen, idx, mu, w_r, w_k, w_v, w_o, w0, w1, w2, a0, a1, a2, g1, g2,
                k_k, k_a, r_k, ln_g, ln_b):
    t, d = x.shape
    bsz = t // seqlen
    xr, xk, xv, lw, a, gate = _rwkv_mix(x, m, g, mu, w1, w2, w0, a1, a2, a0, g1, g2, bsz, seqlen)
    r = _matmul(xr, w_r, idx)
    k = _matmul(xk, w_k, idx)
    v = _matmul(xv, w_v, idx)
    y = _rwkv_scan(r, k, v, lw, a, gate, k_k, k_a, r_k.reshape(-1), ln_g, ln_b, bsz, seqlen)
    return _matmul_resid([y], w_o, x, m, seqlen)


def kernel(x, c, w_mod, b_mod, norm_g, ffn_w1, ffn_w3, ffn_w2, hyb_w_in, hyb_w_out, s5_lambda_re, s5_lambda_im, s5_log_dt, s5_b_re, s5_b_im, s5_c_re, s5_c_im, s5_d, s5_glu_w, s5_glu_b, ssd_conv_w, ssd_conv_b, ssd_dt_bias, ssd_a_log, ssd_d, ssd_norm_g, rwkv_mu, rwkv_w_r, rwkv_w_k, rwkv_w_v, rwkv_w_o, rwkv_w0, rwkv_w1, rwkv_w2, rwkv_a0, rwkv_a1, rwkv_a2, rwkv_g1, rwkv_g2, rwkv_k_k, rwkv_k_a, rwkv_r_k, rwkv_ln_g, rwkv_ln_b, final_g):
    bsz, seqlen, d = x.shape
    depth = w_mod.shape[0]
    xf = x.reshape(bsz * seqlen, d)
    mod = _modulation(c, w_mod, b_mod).reshape(depth, bsz, 3, 3, d)
    for layer in range(depth):
        i = layer // 2
        m0, m1, m2 = mod[layer, :, 0], mod[layer, :, 1], mod[layer, :, 2]
        if layer % 2 == 0:
            xf, h = _ffn(xf, m0, norm_g[layer, 0], ffn_w1, ffn_w3, ffn_w2, layer, 0, seqlen,
                         next_ln=(norm_g[layer, 1], m1))
            xf = _hybrid_mixer(xf, h, m1, seqlen, i, hyb_w_in, hyb_w_out[i], s5_lambda_re[i],
                               s5_lambda_im[i], s5_log_dt[i], s5_b_re[i], s5_b_im[i], s5_c_re[i], s5_c_im[i],
                               s5_d[i], s5_glu_w[i], s5_glu_b[i], ssd_conv_w[i], ssd_conv_b[i], ssd_dt_bias[i],
                               ssd_a_log[i], ssd_d[i], ssd_norm_g[i])
        else:
            xf = _ffn(xf, m0, norm_g[layer, 0], ffn_w1, ffn_w3, ffn_w2, layer, 0, seqlen)
            xf = _rwkv_mixer(xf, m1, norm_g[layer, 1], seqlen, i, rwkv_mu[i], rwkv_w_r, rwkv_w_k, rwkv_w_v,
                             rwkv_w_o[i], rwkv_w0[i], rwkv_w1[i], rwkv_w2[i], rwkv_a0[i], rwkv_a1[i], rwkv_a2[i],
                             rwkv_g1[i], rwkv_g2[i], rwkv_k_k[i], rwkv_k_a[i], rwkv_r_k[i], rwkv_ln_g[i],
                             rwkv_ln_b[i])
        xf = _ffn(xf, m2, norm_g[layer, 2], ffn_w1, ffn_w3, ffn_w2, layer, 1, seqlen,
                  final_g=final_g if layer == depth - 1 else None)
    return xf.reshape(bsz, seqlen, d)
```

```python
import functools
import math

import jax
import jax.numpy as jnp
from jax import lax
from jax.experimental import pallas as pl
from jax.experimental.pallas import tpu as pltpu

F32 = jnp.float32
BF16 = jnp.bfloat16
HI = lax.Precision.HIGHEST

RMS_EPS = 1e-6
RWKV_GN_EPS = 64e-5

S5_GROUP_CH = 16
S5_STATE = 64
S5_CHUNK = 16
SSD_HEAD_DIM = 64
SSD_STATE = 128
SSD_GROUPS = 8
SSD_CHUNK = 128
SSD_CONV = 4
RWKV_HEAD_DIM = 64
RWKV_CHUNK = 64
RWKV_TOKEN_BLOCK = 512
RWKV_PAIRS_PER_BLOCK = 2
LANES = 128
VMEM_LIMIT = 56 * 1024 * 1024
FFN_VMEM_LIMIT = 58 * 1024 * 1024
FFN_DMA_SPLIT = 4

def _params(sem, vmem=VMEM_LIMIT):
    return pltpu.CompilerParams(dimension_semantics=sem, vmem_limit_bytes=vmem)


def _dot(a, b, hi=False):
    dn = (((1,), (0,)), ((), ()))
    if hi:
        return lax.dot_general(a, b, dn, precision=HI, preferred_element_type=F32)
    return lax.dot_general(a.astype(BF16), b.astype(BF16), dn, preferred_element_type=F32)


def _dot_nt(a, b, hi=False):
    dn = (((1,), (1,)), ((), ()))
    if hi:
        return lax.dot_general(a, b, dn, precision=HI, preferred_element_type=F32)
    return lax.dot_general(a.astype(BF16), b.astype(BF16), dn, preferred_element_type=F32)


def _dot_tn(a, b):
    dn = (((0,), (0,)), ((), ()))
    return lax.dot_general(a.astype(BF16), b.astype(BF16), dn, preferred_element_type=F32)


def _silu(x):
    return x * jax.nn.sigmoid(x)


def _softplus(x):
    return jnp.maximum(x, 0.0) + jnp.log1p(jnp.exp(-jnp.abs(x)))


def _rms(x, g):
    return x * lax.rsqrt(jnp.mean(x * x, axis=-1, keepdims=True) + RMS_EPS) * g


def _adaln(x, g, m):
    return _rms(x, g) * (1.0 + m[1:2]) + m[0:1]


def _mod_kernel(c_ref, w_ref, b_ref, o_ref):
    c = c_ref[...]
    o_ref[0] = _dot(_silu(c), w_ref[0]) + b_ref[0]


def _modulation(c, w_mod, b_mod):
    depth, d, n = w_mod.shape
    bsz = c.shape[0]
    rows = 8
    cp = jnp.pad(c, ((0, rows - bsz), (0, 0)))
    tn = 1024
    out = pl.pallas_call(
        _mod_kernel,
        grid=(depth, n // tn),
        in_specs=[pl.BlockSpec((rows, d), lambda l, j: (0, 0)),
                  pl.BlockSpec((1, d, tn), lambda l, j: (l, 0, j)),
                  pl.BlockSpec((1, 1, tn), lambda l, j: (l, 0, j))],
        out_specs=pl.BlockSpec((1, rows, tn), lambda l, j: (l, 0, j)),
        out_shape=jax.ShapeDtypeStruct((depth, rows, n), F32),
        compiler_params=_params(("arbitrary", "arbitrary")),
        name="modulation",
    )(cp, w_mod, b_mod.reshape(depth, 1, n))
    return out[:, :bsz]


def _ffn_kernel(x_ref, m_ref, g_ref, w1_hbm, w3_hbm, w2_hbm, *rest, tail, layer, which, tf):
    if tail == "final":
        fg_ref, o_ref, h_scr, w1_buf, w3_buf, w2_buf, sem = rest
    elif tail == "next":
        ng_ref, nm_ref, o_ref, hn_ref, h_scr, w1_buf, w3_buf, w2_buf, sem = rest
    else:
        o_ref, h_scr, w1_buf, w3_buf, w2_buf, sem = rest
    tm = x_ref.shape[0]
    rc = min(256, tm)
    n_tiles = w1_hbm.shape[-1] // tf

    def tile_copies(j, slot):
        cols = pl.multiple_of(j * tf, tf)
        d = w1_hbm.shape[-2]
        rk, rf = d // FFN_DMA_SPLIT, tf // FFN_DMA_SPLIT
        out = []
        for s in range(FFN_DMA_SPLIT):
            rows = pl.ds(s * rk, rk)
            out.append(pltpu.make_async_copy(w1_hbm.at[layer, which, rows, pl.ds(cols, tf)],
                                             w1_buf.at[slot, rows, :], sem.at[0, s, slot]))
            out.append(pltpu.make_async_copy(w3_hbm.at[layer, which, rows, pl.ds(cols, tf)],
                                             w3_buf.at[slot, rows, :], sem.at[1, s, slot]))
            out.append(pltpu.make_async_copy(w2_hbm.at[layer, which, pl.ds(cols + s * rf, rf), :],
                                             w2_buf.at[slot, pl.ds(s * rf, rf), :], sem.at[2, s, slot]))
        return out

    for cp in tile_copies(0, 0):
        cp.start()
    for r0 in range(0, tm, rc):
        h_scr[r0:r0 + rc, :] = _adaln(x_ref[r0:r0 + rc, :], g_ref[...], m_ref[0]).astype(BF16)
    o_ref[...] = jnp.zeros_like(o_ref)

    def f_tile(j, slot):
        for cp in tile_copies(j, slot):
            cp.wait()

        @pl.when(j + 1 < n_tiles)
        def _():
            for cp in tile_copies(j + 1, 1 - slot):
                cp.start()

        h = h_scr[...]
        a = _dot(h, w1_buf[slot])
        b = _dot(h, w3_buf[slot])
        o_ref[...] += _dot(_silu(a) * b, w2_buf[slot])

    def two_tiles(jj, carry):
        f_tile(2 * jj, 0)
        f_tile(2 * jj + 1, 1)
        return carry

    lax.fori_loop(0, n_tiles // 2, two_tiles, 0)

    for r0 in range(0, tm, rc):
        o = x_ref[r0:r0 + rc, :] + (0.5 * m_ref[0][2:3]) * o_ref[r0:r0 + rc, :]
        if tail == "final":
            o = _rms(o, fg_ref[...])
        o_ref[r0:r0 + rc, :] = o
        if tail == "next":
            hn_ref[r0:r0 + rc, :] = _adaln(o, ng_ref[...], nm_ref[0]).astype(hn_ref.dtype)


def _ffn(x, m, g, w1, w3, w2, layer, which, seqlen, final_g=None, next_ln=None):
    t, d = x.shape
    f = w1.shape[-1]
    tm, tf = min(1024, seqlen), 256
    assert f % (2 * tf) == 0
    per = seqlen // tm
    tail = "final" if final_g is not None else "next" if next_ln is not None else "plain"
    vec = pl.BlockSpec((1, d), lambda i: (0, 0))
    mod = pl.BlockSpec((1, 3, d), lambda i: (i // per, 0, 0))
    tok = pl.BlockSpec((tm, d), lambda i: (i, 0))
    hbm = pl.BlockSpec(memory_space=pl.ANY)
    x_spec = pl.BlockSpec((tm, d), lambda i: (i, 0), pipeline_mode=pl.Buffered(1)) if tail == "next" else tok
    in_specs = [x_spec, mod, vec, hbm, hbm, hbm]
    args = [x, m, g.reshape(1, d), w1, w3, w2]
    out_specs, out_shape = tok, jax.ShapeDtypeStruct((t, d), F32)
    if tail == "final":
        in_specs.append(vec)
        args.append(final_g.reshape(1, d))
    elif tail == "next":
        in_specs += [vec, mod]
        args += [next_ln[0].reshape(1, d), next_ln[1]]
        out_specs, out_shape = [tok, tok], [out_shape, jax.ShapeDtypeStruct((t, d), BF16)]
    return pl.pallas_call(
        functools.partial(_ffn_kernel, tail=tail, layer=layer, which=which, tf=tf),
        grid=(t // tm,),
        in_specs=in_specs,
        out_specs=out_specs,
        out_shape=out_shape,
        scratch_shapes=[pltpu.VMEM((tm, d), BF16),
                        pltpu.VMEM((2, d, tf), F32), pltpu.VMEM((2, d, tf), F32), pltpu.VMEM((2, tf, d), F32),
                        pltpu.SemaphoreType.DMA((3, FFN_DMA_SPLIT, 2))],
        compiler_params=_params(("arbitrary",), FFN_VMEM_LIMIT),
        name="ffn",
    )(*args)


def _matmul_kernel(a_ref, w_ref, o_ref, w_scr, *, valid, out_major):
    @pl.when(pl.program_id(1) == 0)
    def _():
        w = w_ref[...]
        n_axis = 0 if out_major else 1
        if valid < w.shape[n_axis]:
            w = jnp.where(lax.broadcasted_iota(jnp.int32, w.shape, n_axis) < valid, w, 0.0)
        w_scr[...] = w.astype(BF16)

    dot = _dot_nt if out_major else _dot
    o_ref[...] = dot(a_ref[...], w_scr[...]).astype(o_ref.dtype)


def _matmul(a, w, lead, col0=0, n=None, out_dtype=F32, out_major=False):
    m, k = a.shape
    n_total = w.shape[1] if out_major else w.shape[2]
    n = n_total - col0 if n is None else n
    n_pad = -(-n // LANES) * LANES
    tm = min(1024, m)
    tn = min(1024, n_pad)
    assert col0 % tn == 0 and n_pad % tn == 0
    cb = col0 // tn
    if out_major:
        w_spec = pl.BlockSpec((None, tn, k), lambda j, i: (lead, cb + j, 0))
        w_tile = (tn, k)
    else:
        w_spec = pl.BlockSpec((None, k, tn), lambda j, i: (lead, 0, cb + j))
        w_tile = (k, tn)
    return pl.pallas_call(
        functools.partial(_matmul_kernel, valid=min(tn, n), out_major=out_major),
        grid=(n_pad // tn, m // tm),
        in_specs=[pl.BlockSpec((tm, k), lambda j, i: (i, 0)), w_spec],
        out_specs=pl.BlockSpec((tm, tn), lambda j, i: (i, j)),
        out_shape=jax.ShapeDtypeStruct((m, n_pad), out_dtype),
        scratch_shapes=[pltpu.VMEM(w_tile, BF16)],
        compiler_params=_params(("parallel", "arbitrary")),
        name="matmul",
    )(a, w)


def _matmul_resid_kernel(*refs, n_in):
    a_refs = refs[:n_in]
    w_refs = refs[n_in:2 * n_in]
    x_ref, m_ref, o_ref = refs[2 * n_in:2 * n_in + 3]
    w_scrs = refs[2 * n_in + 3:]

    @pl.when(pl.program_id(1) == 0)
    def _():
        for w_ref, w_scr in zip(w_refs, w_scrs):
            w_scr[...] = w_ref[...].astype(BF16)

    acc = _dot(a_refs[0][...], w_scrs[0][...])
    for a_ref, w_scr in zip(a_refs[1:], w_scrs[1:]):
        acc = acc + _dot(a_ref[...], w_scr[...])
    o_ref[...] = x_ref[...] + m_ref[0][2:3] * acc


def _matmul_resid(a_list, w, x, m, seqlen):
    t, d = x.shape
    tm, tn = min(512, seqlen), 1024
    per = seqlen // tm
    n_in = len(a_list)
    kb = w.shape[0] // n_in
    arrays, in_specs = [], []
    for a in a_list:
        arr, blk = a if isinstance(a, tuple) else (a, 0)
        arrays.append(arr)
        in_specs.append(pl.BlockSpec((tm, kb), lambda j, i, blk=blk: (i, blk)))
    in_specs += [pl.BlockSpec((kb, tn), lambda j, i, r=r: (r, j)) for r in range(n_in)]
    in_specs += [pl.BlockSpec((tm, tn), lambda j, i: (i, j)),
                 pl.BlockSpec((1, 3, tn), lambda j, i: (i // per, 0, j))]
    return pl.pallas_call(
        functools.partial(_matmul_resid_kernel, n_in=n_in),
        grid=(d // tn, t // tm),
        in_specs=in_specs,
        out_specs=pl.BlockSpec((tm, tn), lambda j, i: (i, j)),
        out_shape=jax.ShapeDtypeStruct((t, d), F32),
        scratch_shapes=[pltpu.VMEM((kb, tn), BF16)] * n_in,
        compiler_params=_params(("parallel", "arbitrary")),
        name="matmul_resid",
    )(*arrays, *([w] * n_in), x, m)


def _s5_kernel(u_ref, lrg_ref, lig_ref, ldg_ref, lrl_ref, lil_ref, ldl_ref, btr_ref, bti_ref, cr_ref, ci_ref,
               o_ref, *, nc, levels):
    q, gc, p = S5_CHUNK, S5_GROUP_CH, S5_STATE
    ng = LANES // gc
    gp = ng * p
    rows = u_ref.shape[0] // q
    half = q // 2

    def make_pow(lre, lim, ldt):
        lr = jnp.minimum(lre, -1e-4)
        dt = jnp.exp(ldt)

        def lam_pow(mult):
            mag = jnp.exp(lr * dt * mult)
            ang = lim * dt * mult
            return mag * jnp.cos(ang), mag * jnp.sin(ang)

        lb_re, lb_im = lam_pow(1.0)
        den = lr * lr + lim * lim
        nr, ni = lb_re - 1.0, lb_im
        return lam_pow, (nr * lr + ni * lim) / den, (ni * lr - nr * lim) / den

    pow_g, f_re_g, f_im_g = make_pow(lrg_ref[...], lig_ref[...], ldg_ref[...])
    pow_l, f_re_l, f_im_l = make_pow(lrl_ref[0], lil_ref[0], ldl_ref[0])

    ri = lax.broadcasted_iota(jnp.int32, (LANES, ng), 0)
    ci_ = lax.broadcasted_iota(jnp.int32, (LANES, ng), 1)
    to_rows = (ri // gc == ci_).astype(F32)
    expand = lambda a: _dot(to_rows, a, hi=True)
    pi_ = lax.broadcasted_iota(jnp.int32, (p, gp), 0)
    pj_ = lax.broadcasted_iota(jnp.int32, (p, gp), 1)
    to_lanes = (pi_ == pj_ % p).astype(F32)
    tile = lambda a: _dot(a, to_lanes, hi=True)
    same_rc = (lax.broadcasted_iota(jnp.int32, (LANES, LANES), 0) // gc
               == lax.broadcasted_iota(jnp.int32, (LANES, LANES), 1) // gc)
    same_rl = (lax.broadcasted_iota(jnp.int32, (LANES, gp), 0) // gc
               == lax.broadcasted_iota(jnp.int32, (LANES, gp), 1) // p)

    btr, bti = btr_ref[...], bti_ref[...]
    cr, ci = cr_ref[...], ci_ref[...]
    f_re_r, f_im_r = expand(f_re_g), expand(f_im_g)
    bb_re = f_re_r * btr - f_im_r * bti
    bb_im = f_re_r * bti + f_im_r * btr

    w = []
    for tau in range(q):
        pr, pi = pow_g(float(tau))
        pr, pi = expand(pr), expand(pi)
        k_t = (_dot_nt(bb_re, cr * pr - ci * pi, hi=True) - _dot_nt(bb_im, cr * pi + ci * pr, hi=True))
        w.append(jnp.where(same_rc, k_t, 0.0).astype(BF16))
    zero = jnp.zeros((LANES, LANES), BF16)

    def w_pair(delta):
        lo = w[2 * delta - 1] if delta > 0 else zero
        return jnp.concatenate([jnp.concatenate([w[2 * delta], w[2 * delta + 1]], axis=1),
                                jnp.concatenate([lo, w[2 * delta]], axis=1)], axis=0)

    u_pair = []
    for jp in range(half):
        u_pair.append(jnp.concatenate(
            [u_ref[pl.ds(2 * jp + jj, rows, stride=q), :].astype(BF16) for jj in range(2)], axis=1))

    bt_re, bt_im = tile(btr), tile(bti)
    bbl_re = jnp.where(same_rl, f_re_l * bt_re - f_im_l * bt_im, 0.0)
    bbl_im = jnp.where(same_rl, f_re_l * bt_im + f_im_l * bt_re, 0.0)
    st = None
    for jp in range(half):
        blocks = []
        for j in (2 * jp, 2 * jp + 1):
            pr, pi = pow_l(float(q - 1 - j))
            blocks.append(jnp.concatenate([bbl_re * pr - bbl_im * pi, bbl_re * pi + bbl_im * pr], axis=1))
        term = _dot(u_pair[jp], jnp.concatenate(blocks, axis=0))
        st = term if st is None else st + term
    sr, si = st[:, :gp], st[:, gp:]

    cidx = lax.broadcasted_iota(jnp.int32, (rows, 1), 0) % nc
    for k in range(levels):
        sh = 1 << k
        ar, ai = pow_l(float(q * sh))
        xr = pltpu.roll(sr, sh, axis=0)
        xi = pltpu.roll(si, sh, axis=0)
        ok = cidx >= sh
        sr, si = (sr + jnp.where(ok, ar * xr - ai * xi, 0.0),
                  si + jnp.where(ok, ar * xi + ai * xr, 0.0))
    ok = cidx >= 1
    s_prev = jnp.concatenate([jnp.where(ok, pltpu.roll(sr, 1, axis=0), 0.0),
                              jnp.where(ok, pltpu.roll(si, 1, axis=0), 0.0)], axis=1).astype(BF16)

    ct_re, ct_im = tile(cr), tile(ci)
    ct_re = jnp.where(same_rl, ct_re, 0.0)
    ct_im = jnp.where(same_rl, ct_im, 0.0)
    for ip in range(half):
        acc = None
        for jp in range(ip + 1):
            term = _dot(u_pair[jp], w_pair(ip - jp))
            acc = term if acc is None else acc + term
        blocks = []
        for i in (2 * ip, 2 * ip + 1):
            pr, pi = pow_l(float(i + 1))
            blocks.append(jnp.concatenate([ct_re * pr - ct_im * pi, -(ct_re * pi + ct_im * pr)], axis=1))
        acc = acc + _dot_nt(s_prev, jnp.concatenate(blocks, axis=0))
        o_ref[pl.ds(2 * ip, rows, stride=q), :] = acc[:, :LANES]
        o_ref[pl.ds(2 * ip + 1, rows, stride=q), :] = acc[:, LANES:]


def _s5_post_kernel(y_ref, u_ref, d_ref, w_ref, b_ref, o_ref):
    y = y_ref[...] + d_ref[...] * u_ref[...]
    y = jax.nn.gelu(y)
    o_ref[...] = (y * jax.nn.sigmoid(_dot(y, w_ref[...]) + b_ref[...])).astype(o_ref.dtype)


def _s5_branch(u, bsz, seqlen, lam_re, lam_im, log_dt, b_re, b_im, c_re, c_im, d_skip, glu_w, glu_b):
    t, width = u.shape
    q, gc, p = S5_CHUNK, S5_GROUP_CH, S5_STATE
    groups = width // gc
    nc = seqlen // q
    levels = int(math.log2(nc))
    assert (1 << levels) == nc
    ng = LANES // gc
    nblk = width // LANES
    gp = ng * p
    ldt = jnp.broadcast_to(log_dt[:, None], (groups, p))
    by_group = pl.BlockSpec((ng, p), lambda k: (k, 0))
    by_lane = pl.BlockSpec((1, 1, gp), lambda k: (k, 0, 0))
    by_chan = pl.BlockSpec((LANES, p), lambda k: (k, 0))
    tok = pl.BlockSpec((t, LANES), lambda k: (0, k))
    lanes3 = lambda a: a.reshape(nblk, 1, gp)
    chan2 = lambda a: a.reshape(width, p)
    y = pl.pallas_call(
        functools.partial(_s5_kernel, nc=nc, levels=levels),
        grid=(nblk,),
        in_specs=[tok, by_group, by_group, by_group, by_lane, by_lane, by_lane,
                  by_chan, by_chan, by_chan, by_chan],
        out_specs=tok,
        out_shape=jax.ShapeDtypeStruct((t, width), F32),
        compiler_params=_params(("parallel",)),
        name="s5_conv",
    )(u, lam_re, lam_im, ldt, lanes3(lam_re), lanes3(lam_im), lanes3(ldt),
      chan2(jnp.swapaxes(b_re, 1, 2)), chan2(jnp.swapaxes(b_im, 1, 2)), chan2(c_re), chan2(c_im))

    tm = min(512, t)
    return pl.pallas_call(
        _s5_post_kernel,
        grid=(t // tm,),
        in_specs=[pl.BlockSpec((tm, width), lambda i: (i, 0)),
                  pl.BlockSpec((tm, width), lambda i: (i, 0)),
                  pl.BlockSpec((1, width), lambda i: (0, 0)),
                  pl.BlockSpec((width, width), lambda i: (0, 0)),
                  pl.BlockSpec((1, width), lambda i: (0, 0))],
        out_specs=pl.BlockSpec((tm, width), lambda i: (i, 0)),
        out_shape=jax.ShapeDtypeStruct((t, width), BF16),
        compiler_params=_params(("parallel",)),
        name="s5_post",
    )(y, u, d_skip.reshape(1, width), glu_w.astype(BF16), glu_b.reshape(1, width))


def _ssd_kernel(z_ref, xbc_ref, dt_ref, cw_ref, cb_ref, dtb_ref, alog_ref, dsk_ref, ng_ref,
                o_ref, ext_scr, st_scr, *, heads, inner):
    lc = SSD_CHUNK
    hd, ns = SSD_HEAD_DIM, SSD_STATE
    rpg = heads // SSD_GROUPS
    gw = SSD_GROUPS * ns
    tail = SSD_CONV - 1

    @pl.when(pl.program_id(1) == 0)
    def _():
        ext_scr[0:8, :] = jnp.zeros((8, ext_scr.shape[1]), F32)
        st_scr[...] = jnp.zeros_like(st_scr)

    ext_scr[8:8 + lc, :] = xbc_ref[...]
    cw = cw_ref[...]
    conv = cb_ref[...] + cw[0:1] * ext_scr[pl.ds(8 - tail, lc), :]
    for k in range(1, SSD_CONV):
        conv = conv + cw[k:k + 1] * ext_scr[pl.ds(8 - tail + k, lc), :]
    ext_scr[0:8, :] = ext_scr[lc:lc + 8, :]
    act = _silu(conv)
    xs = act[:, :inner]
    bs = act[:, inner:inner + gw]
    cs = act[:, inner + gw:]

    dt = _softplus(dt_ref[...] + dtb_ref[...])
    adt = dt * (-jnp.exp(alog_ref[...]))
    ri = lax.broadcasted_iota(jnp.int32, (lc, lc), 0)
    ci = lax.broadcasted_iota(jnp.int32, (lc, lc), 1)
    causal = ri >= ci
    a_cum = _dot(causal.astype(F32), adt, hi=True)
    a_cum_t = a_cum.T
    dt_t = dt.T

    y_parts = []
    for g in range(SSD_GROUPS):
        cs_g = cs[:, g * ns:(g + 1) * ns]
        bs_g = bs[:, g * ns:(g + 1) * ns]
        cb = _dot_nt(cs_g, bs_g)
        bs_gt = bs_g.T
        for r in range(rpg):
            h = g * rpg + r
            col = a_cum[:, h:h + 1]
            row = a_cum_t[h:h + 1, :]
            dt_row = dt_t[h:h + 1, :]
            decay = jnp.exp(jnp.where(causal, col - row, -jnp.inf))
            xs_h = xs[:, h * hd:(h + 1) * hd]
            st = st_scr[h]
            lhs = jnp.concatenate([cb * decay * dt_row, cs_g * jnp.exp(col)], axis=1)
            y_parts.append(_dot(lhs, jnp.concatenate([xs_h, st], axis=0)))
            last = row[:, lc - 1:lc]
            to_end = jnp.exp(last - row)
            st_scr[h] = jnp.exp(last) * st + _dot(bs_gt * (to_end * dt_row), xs_h)
    y = jnp.concatenate(y_parts, axis=1) + dsk_ref[...] * xs
    y = y * _silu(z_ref[...])
    o_ref[...] = _rms(y, ng_ref[...]).astype(o_ref.dtype)


def _ssd_branch(z, xbc, dt_raw, bsz, seqlen, conv_w, conv_b, dt_bias, a_log, d_skip, norm_g):
    t, inner = z.shape
    cd = xbc.shape[1]
    heads = inner // SSD_HEAD_DIM
    lc = SSD_CHUNK
    nc = seqlen // lc
    pad = LANES - heads
    padv = lambda v: jnp.pad(v.reshape(1, heads), ((0, 0), (0, pad)))
    row = lambda n: pl.BlockSpec((1, n), lambda b, c: (0, 0))
    tok = lambda n: pl.BlockSpec((lc, n), lambda b, c: (b * nc + c, 0))
    return pl.pallas_call(
        functools.partial(_ssd_kernel, heads=heads, inner=inner),
        grid=(bsz, nc),
        in_specs=[tok(inner), tok(cd), tok(LANES),
                  pl.BlockSpec((SSD_CONV, cd), lambda b, c: (0, 0)), row(cd),
                  row(LANES), row(LANES), row(inner), row(inner)],
        out_specs=tok(inner),
        out_shape=jax.ShapeDtypeStruct((t, inner), BF16),
        scratch_shapes=[pltpu.VMEM((8 + lc + 8, cd), F32),
                        pltpu.VMEM((heads, SSD_STATE, SSD_HEAD_DIM), F32)],
        compiler_params=_params(("parallel", "arbitrary")),
        name="ssd",
    )(z, xbc, dt_raw, conv_w, conv_b.reshape(1, cd), padv(dt_bias), padv(a_log),
      jnp.repeat(d_skip, SSD_HEAD_DIM).reshape(1, inner), norm_g.reshape(1, inner))


def _hybrid_mixer(x, h, m, seqlen, idx, w_in, w_out, lam_re, lam_im, log_dt, b_re, b_im, c_re, c_im, s5_d,
                  glu_w, glu_b, conv_w, conv_b, dt_bias, a_log, ssd_d, ssd_norm_g):
    t, d = x.shape
    bsz = t // seqlen
    s5w = s5_d.shape[0]
    inner = ssd_norm_g.shape[0]
    cd = conv_w.shape[1]
    heads = dt_bias.shape[0]
    o1, o2, o3 = s5w, s5w + inner, s5w + inner + cd
    w_t = jnp.swapaxes(w_in, 1, 2)
    u = _matmul(h, w_t, idx, 0, o1, out_major=True)
    z = _matmul(h, w_t, idx, o1, inner, out_major=True)
    xbc = _matmul(h, w_t, idx, o2, cd, out_major=True)
    dt_raw = _matmul(h, w_t, idx, o3, heads, out_major=True)
    y_s5 = _s5_branch(u, bsz, seqlen, lam_re, lam_im, log_dt, b_re, b_im, c_re, c_im, s5_d, glu_w, glu_b)
    y_ssd = _ssd_branch(z, xbc, dt_raw, bsz, seqlen, conv_w, conv_b, dt_bias, a_log, ssd_d, ssd_norm_g)
    assert inner % s5w == 0
    return _matmul_resid([y_s5] + [(y_ssd, i) for i in range(inner // s5w)], w_out, x, m, seqlen)


def _rwkv_mix_kernel(x_ref, m_ref, g_ref, mu_ref, w1_ref, w2_ref, w0_ref, a1_ref, a2_ref, a0_ref, g1_ref, g2_ref,
                     xr_ref, xk_ref, xv_ref, lw_ref, a_ref, gate_ref, prev_scr):
    h = _adaln(x_ref[...], g_ref[...], m_ref[0])
    tm = h.shape[0]

    @pl.when(pl.program_id(1) == 0)
    def _():
        prev_scr[...] = jnp.zeros_like(prev_scr)

    first = lax.broadcasted_iota(jnp.int32, (tm, 1), 0) == 0
    shifted = jnp.where(first, prev_scr[7:8, :], pltpu.roll(h, 1, axis=0))
    prev_scr[...] = h[tm - 8:tm]
    xx = shifted - h
    mu = mu_ref[...]
    mix = lambda i: h + xx * mu[i:i + 1]
    xr_ref[...] = mix(0).astype(xr_ref.dtype)
    xk_ref[...] = mix(2).astype(xk_ref.dtype)
    xv_ref[...] = mix(3).astype(xv_ref.dtype)
    w = -_softplus(-(w0_ref[...] + _dot(jnp.tanh(_dot(mix(1), w1_ref[...])), w2_ref[...]))) - 0.5
    lw_ref[...] = -jnp.exp(w)
    a_ref[...] = jax.nn.sigmoid(a0_ref[...] + _dot(_dot(mix(4), a1_ref[...]), a2_ref[...]))
    gate_ref[...] = _dot(jax.nn.sigmoid(_dot(mix(5), g1_ref[...])), g2_ref[...])


def _rwkv_mix(x, m, g, mu, w1, w2, w0, a1, a2, a0, g1, g2, bsz, seqlen):
    t, d = x.shape
    tm = min(256, seqlen)
    per = seqlen // tm
    tok = pl.BlockSpec((tm, d), lambda b, i: (b * per + i, 0))
    vec = pl.BlockSpec((1, d), lambda b, i: (0, 0))
    whole = lambda a: pl.BlockSpec(a.shape, lambda b, i: (0, 0))

    def lora(a, b):
        rank = a.shape[1]
        rp = -(-rank // LANES) * LANES
        return (jnp.pad(a, ((0, 0), (0, rp - rank))).astype(BF16),
                jnp.pad(b, ((0, rp - rank), (0, 0))).astype(BF16))

    (w1p, w2p), (a1p, a2p), (g1p, g2p) = lora(w1, w2), lora(a1, a2), lora(g1, g2)
    return pl.pallas_call(
        _rwkv_mix_kernel,
        grid=(bsz, per),
        in_specs=[tok, pl.BlockSpec((1, 3, d), lambda b, i: (b, 0, 0)), vec,
                  pl.BlockSpec((8, d), lambda b, i: (0, 0)),
                  whole(w1p), whole(w2p), vec, whole(a1p), whole(a2p), vec, whole(g1p), whole(g2p)],
        out_specs=[tok] * 6,
        out_shape=[jax.ShapeDtypeStruct((t, d), BF16)] * 3 + [jax.ShapeDtypeStruct((t, d), F32)] * 3,
        scratch_shapes=[pltpu.VMEM((8, d), F32)],
        compiler_params=_params(("parallel", "arbitrary")),
        name="rwkv_mix",
    )(x, m, g.reshape(1, d), jnp.pad(mu, ((0, 2), (0, 0))),
      w1p, w2p, w0.reshape(1, d), a1p, a2p, a0.reshape(1, d), g1p, g2p)


def _rwkv_scan_kernel(r_ref, k_ref, v_ref, lw_ref, a_ref, g_ref, kk_ref, ka_ref, rk_ref, lg_ref, lb_ref,
                      o_ref, s_scr):
    c = RWKV_CHUNK
    hd = RWKV_HEAD_DIM
    c2 = 2 * c
    tokens = r_ref.shape[0]
    pairs = r_ref.shape[1] // LANES
    tb = pairs * tokens
    per_pair = tokens // c
    n_chunks = tb // c

    @pl.when(pl.program_id(2) == 0)
    def _():
        s_scr[...] = jnp.zeros_like(s_scr)

    def rows_of(ref):
        return jnp.concatenate([ref[:, p * LANES:(p + 1) * LANES] for p in range(pairs)], axis=0)

    def param_rows(ref):
        return jnp.concatenate([jnp.broadcast_to(ref[:, p * LANES:(p + 1) * LANES], (tokens, LANES))
                                for p in range(pairs)], axis=0)

    lane_head = lax.broadcasted_iota(jnp.int32, (1, LANES), 1) // hd
    head0 = lane_head == 0
    same_head = ((lax.broadcasted_iota(jnp.int32, (LANES, LANES), 0) // hd)
                 == (lax.broadcasted_iota(jnp.int32, (LANES, LANES), 1) // hd))
    r2 = lax.broadcasted_iota(jnp.int32, (1, c2, c2), 1)
    q2 = lax.broadcasted_iota(jnp.int32, (1, c2, c2), 2)
    same_blk = (r2 // c) == (q2 // c)
    strict = same_blk & ((r2 % c) > (q2 % c))
    incl = same_blk & ((r2 % c) >= (q2 % c))
    eye2 = (r2 == q2).astype(F32)
    k_k, k_a, r_k = param_rows(kk_ref), param_rows(ka_ref), param_rows(rk_ref)

    def stack(x):
        x = x.reshape(n_chunks, c, LANES)
        return jnp.concatenate([jnp.where(head0, x, 0.0), jnp.where(head0, 0.0, x)], axis=1)

    def unstack(x):
        return (x[:, :c] + x[:, c:]).reshape(tb, LANES)

    def bdot(p, q, dims):
        return lax.dot_general(p.astype(BF16), q.astype(BF16), (dims, ((0,), (0,))), preferred_element_type=F32)

    bmm = lambda p, q: bdot(p, q, ((2,), (1,)))
    bmm_nt = lambda p, q: bdot(p, q, ((2,), (2,)))
    bmm_tn = lambda p, q: bdot(p, q, ((1,), (1,)))

    def split(x):
        hi_part = x.astype(BF16)
        return hi_part, (x - hi_part.astype(F32)).astype(BF16)

    def head_sum(x):
        s0 = jnp.sum(jnp.where(head0, x, 0.0), axis=1, keepdims=True)
        s1 = jnp.sum(jnp.where(head0, 0.0, x), axis=1, keepdims=True)
        return jnp.where(head0, s0, s1)

    def cumsum_rows(x):
        row_in_chunk = lax.broadcasted_iota(jnp.int32, (tb, 1), 0) % c
        shift = 1
        while shift < c:
            x = x + jnp.where(row_in_chunk >= shift, pltpu.roll(x, shift, axis=0), 0.0)
            shift *= 2
        return x

    def lhs3(hi_part, lo_part):
        return jnp.concatenate([hi_part, lo_part], axis=2)

    def rhs3(hi_part, lo_part):
        top = jnp.concatenate([hi_part, lo_part], axis=2)
        bot = jnp.concatenate([hi_part, jnp.zeros_like(lo_part)], axis=2)
        return jnp.concatenate([top, bot], axis=1)

    def fold(x):
        return x[:, :, :c2] + x[:, :, c2:]

    r, k, v, lw, a = rows_of(r_ref), rows_of(k_ref), rows_of(v_ref), rows_of(lw_ref), rows_of(a_ref)
    kk = k * k_k
    k2 = k * (1.0 + (a - 1.0) * k_a)
    rk_sum = head_sum(r * k2 * r_k)
    kk = kk / jnp.maximum(jnp.sqrt(head_sum(kk * kk)), 1e-12)
    bv = kk * a
    cum = cumsum_rows(lw)
    cum3 = cum.reshape(n_chunks, c, LANES)
    tot = jnp.broadcast_to(cum3[:, c - 1:c, :], cum3.shape).reshape(tb, LANES)
    g_inv = jnp.exp(-cum)
    rt_u = r * jnp.exp(cum)
    at = stack(-kk * jnp.exp(cum - lw))
    rt = stack(rt_u)
    bt = stack(bv * g_inv)
    kt = stack(k2 * g_inv)
    to_end = jnp.exp(tot - cum)
    b_end = (bv * to_end).reshape(n_chunks, c, LANES)
    k_end = (k2 * to_end).reshape(n_chunks, c, LANES)
    vs = stack(v)

    gram = bmm_nt(jnp.concatenate([at, rt], axis=1), jnp.concatenate([bt, kt], axis=1))
    n_ab = jnp.where(strict, gram[:, :c2, :c2], 0.0)
    a_ak = jnp.where(strict, gram[:, :c2, c2:], 0.0)
    a_rb = jnp.where(incl, gram[:, c2:, :c2], 0.0)
    a_rk = jnp.where(incl, gram[:, c2:, c2:], 0.0)
    pw_h, pw_l = split(n_ab)
    pw = fold(bmm(lhs3(pw_h, pw_l), rhs3(pw_h, pw_l)))
    tinv = eye2 + n_ab
    for _ in range(int(math.log2(c)) - 2):
        pw_h, pw_l = split(pw)
        t_h, t_l = split(tinv)
        both = fold(bmm(jnp.concatenate([lhs3(pw_h, pw_l), lhs3(t_h, t_l)], axis=1), rhs3(pw_h, pw_l)))
        pw = both[:, :c2]
        tinv = tinv + both[:, c2:]
    pw_h, pw_l = split(pw)
    t_h, t_l = split(tinv)
    tinv = tinv + fold(bmm(lhs3(t_h, t_l), rhs3(pw_h, pw_l)))
    pq = bmm(tinv, jnp.concatenate([at, bmm(a_ak, vs)], axis=2))
    ry = bmm(jnp.concatenate([a_rb, a_rk], axis=2),
             jnp.concatenate([pq, jnp.concatenate([jnp.zeros_like(vs), vs], axis=2)], axis=1))
    rq = rt_u + unstack(ry[:, :, :LANES])
    y0 = unstack(ry[:, :, LANES:])
    p_u = unstack(pq[:, :, :LANES]).reshape(n_chunks, c, LANES)
    q_u = unstack(pq[:, :, LANES:]).reshape(n_chunks, c, LANES)
    gm = jnp.where(same_head, bmm_tn(p_u, b_end), 0.0)
    dm = jnp.where(same_head, bmm_tn(jnp.concatenate([q_u, v.reshape(n_chunks, c, LANES)], axis=1),
                                     jnp.concatenate([b_end, k_end], axis=1)), 0.0)
    g_tot = jnp.exp(cum3[:, c - 1:c, :])
    bonus = rk_sum * v

    s = [s_scr[p] for p in range(pairs)]
    ys = [[] for _ in range(pairs)]
    for i in range(per_pair):
        for p in range(pairs):
            n = p * per_pair + i
            ys[p].append(_dot_nt(rq[n * c:(n + 1) * c], s[p]))
            s[p] = s[p] * g_tot[n] + _dot(s[p], gm[n]) + dm[n]
    for p in range(pairs):
        s_scr[p] = s[p]
    y = jnp.concatenate([y_c for y_p in ys for y_c in y_p], axis=0) + y0
    mean = head_sum(y) * (1.0 / hd)
    dy = y - mean
    var = head_sum(dy * dy) * (1.0 / hd)
    yn = dy * lax.rsqrt(var + RWKV_GN_EPS) * param_rows(lg_ref) + param_rows(lb_ref)
    out = ((yn + bonus) * rows_of(g_ref)).astype(o_ref.dtype)
    for p in range(pairs):
        o_ref[:, p * LANES:(p + 1) * LANES] = out[p * tokens:(p + 1) * tokens]


def _rwkv_scan(r, k, v, lw, a, g, k_k, k_a, r_k, ln_g, ln_b, bsz, seqlen):
    t, d = r.shape
    tb = min(RWKV_TOKEN_BLOCK, seqlen)
    per = seqlen // tb
    pairs = RWKV_PAIRS_PER_BLOCK
    bw = pairs * LANES
    tok = pl.BlockSpec((tb, bw), lambda b, h, i: (b * per + i, h))
    row = pl.BlockSpec((1, bw), lambda b, h, i: (0, h))
    vec = lambda p: p.reshape(1, d)
    return pl.pallas_call(
        _rwkv_scan_kernel,
        grid=(bsz, d // bw, per),
        in_specs=[tok] * 6 + [row] * 5,
        out_specs=tok,
        out_shape=jax.ShapeDtypeStruct((t, d), BF16),
        scratch_shapes=[pltpu.VMEM((pairs, LANES, LANES), F32)],
        compiler_params=_params(("parallel", "parallel", "arbitrary")),
        name="rwkv_scan",
    )(r, k, v, lw, a, g, vec(k_k), vec(k_a), vec(r_k), vec(ln_g), vec(ln_b))


def _rwkv_mixer(x, m, g, seqlen, idx, mu, w_r, w_k, w_v, w_o, w0, w1, w2, a0, a1, a2, g1, g2,
                k_k, k_a, r_k, ln_g, ln_b):
    t, d = x.shape
    bsz = t // seqlen
    xr, xk, xv, lw, a, gate = _rwkv_mix(x, m, g, mu, w1, w2, w0, a1, a2, a0, g1, g2, bsz, seqlen)
    r = _matmul(xr, w_r, idx)
    k = _matmul(xk, w_k, idx)
    v = _matmul(xv, w_v, idx)
    y = _rwkv_scan(r, k, v, lw, a, gate, k_k, k_a, r_k.reshape(-1), ln_g, ln_b, bsz, seqlen)
    return _matmul_resid([y], w_o, x, m, seqlen)


def kernel(x, c, w_mod, b_mod, norm_g, ffn_w1, ffn_w3, ffn_w2, hyb_w_in, hyb_w_out, s5_lambda_re, s5_lambda_im, s5_log_dt, s5_b_re, s5_b_im, s5_c_re, s5_c_im, s5_d, s5_glu_w, s5_glu_b, ssd_conv_w, ssd_conv_b, ssd_dt_bias, ssd_a_log, ssd_d, ssd_norm_g, rwkv_mu, rwkv_w_r, rwkv_w_k, rwkv_w_v, rwkv_w_o, rwkv_w0, rwkv_w1, rwkv_w2, rwkv_a0, rwkv_a1, rwkv_a2, rwkv_g1, rwkv_g2, rwkv_k_k, rwkv_k_a, rwkv_r_k, rwkv_ln_g, rwkv_ln_b, final_g):
    bsz, seqlen, d = x.shape
    depth = w_mod.shape[0]
    xf = x.reshape(bsz * seqlen, d)
    mod = _modulation(c, w_mod, b_mod).reshape(depth, bsz, 3, 3, d)
    for layer in range(depth):
        i = layer // 2
        m0, m1, m2 = mod[layer, :, 0], mod[layer, :, 1], mod[layer, :, 2]
        if layer % 2 == 0:
            xf, h = _ffn(xf, m0, norm_g[layer, 0], ffn_w1, ffn_w3, ffn_w2, layer, 0, seqlen,
                         next_ln=(norm_g[layer, 1], m1))
            xf = _hybrid_mixer(xf, h, m1, seqlen, i, hyb_w_in, hyb_w_out[i], s5_lambda_re[i],
                               s5_lambda_im[i], s5_log_dt[i], s5_b_re[i], s5_b_im[i], s5_c_re[i], s5_c_im[i],
                               s5_d[i], s5_glu_w[i], s5_glu_b[i], ssd_conv_w[i], ssd_conv_b[i], ssd_dt_bias[i],
                               ssd_a_log[i], ssd_d[i], ssd_norm_g[i])
        else:
            xf = _ffn(xf, m0, norm_g[layer, 0], ffn_w1, ffn_w3, ffn_w2, layer, 0, seqlen)
            xf = _rwkv_mixer(xf, m1, norm_g[layer, 1], seqlen, i, rwkv_mu[i], rwkv_w_r, rwkv_w_k, rwkv_w_v,
                             rwkv_w_o[i], rwkv_w0[i], rwkv_w1[i], rwkv_w2[i], rwkv_a0[i], rwkv_a1[i], rwkv_a2[i],
                             rwkv_g1[i], rwkv_g2[i], rwkv_k_k[i], rwkv_k_a[i], rwkv_r_k[i], rwkv_ln_g[i],
                             rwkv_ln_b[i])
        xf = _ffn(xf, m2, norm_g[layer, 2], ffn_w1, ffn_w3, ffn_w2, layer, 1, seqlen,
                  final_g=final_g if layer == depth - 1 else None)
    return xf.reshape(bsz, seqlen, d)
```

```python
import functools
import math

import jax
import jax.numpy as jnp
from jax import lax
from jax.experimental import pallas as pl
from jax.experimental.pallas import tpu as pltpu

F32 = jnp.float32
BF16 = jnp.bfloat16
HI = lax.Precision.HIGHEST

RMS_EPS = 1e-6
RWKV_GN_EPS = 64e-5

S5_GROUP_CH = 16
S5_STATE = 64
S5_CHUNK = 16
SSD_HEAD_DIM = 64
SSD_STATE = 128
SSD_GROUPS = 8
SSD_CHUNK = 128
SSD_CONV = 4
RWKV_HEAD_DIM = 64
RWKV_CHUNK = 64
RWKV_TOKEN_BLOCK = 512
RWKV_PAIRS_PER_BLOCK = 2
LANES = 128
VMEM_LIMIT = 56 * 1024 * 1024
FFN_VMEM_LIMIT = 58 * 1024 * 1024
FFN_DMA_SPLIT = 4

def _params(sem, vmem=VMEM_LIMIT):
    return pltpu.CompilerParams(dimension_semantics=sem, vmem_limit_bytes=vmem)


def _dot(a, b, hi=False):
    dn = (((1,), (0,)), ((), ()))
    if hi:
        return lax.dot_general(a, b, dn, precision=HI, preferred_element_type=F32)
    return lax.dot_general(a.astype(BF16), b.astype(BF16), dn, preferred_element_type=F32)


def _dot_nt(a, b, hi=False):
    dn = (((1,), (1,)), ((), ()))
    if hi:
        return lax.dot_general(a, b, dn, precision=HI, preferred_element_type=F32)
    return lax.dot_general(a.astype(BF16), b.astype(BF16), dn, preferred_element_type=F32)


def _dot_tn(a, b):
    dn = (((0,), (0,)), ((), ()))
    return lax.dot_general(a.astype(BF16), b.astype(BF16), dn, preferred_element_type=F32)


def _silu(x):
    return x * jax.nn.sigmoid(x)


def _softplus(x):
    return jnp.maximum(x, 0.0) + jnp.log1p(jnp.exp(-jnp.abs(x)))


def _rms(x, g):
    return x * lax.rsqrt(jnp.mean(x * x, axis=-1, keepdims=True) + RMS_EPS) * g


def _adaln(x, g, m):
    return _rms(x, g) * (1.0 + m[1:2]) + m[0:1]


def _mod_kernel(c_ref, w_ref, b_ref, o_ref):
    c = c_ref[...]
    o_ref[0] = _dot(_silu(c), w_ref[0]) + b_ref[0]


def _modulation(c, w_mod, b_mod):
    depth, d, n = w_mod.shape
    bsz = c.shape[0]
    rows = 8
    cp = jnp.pad(c, ((0, rows - bsz), (0, 0)))
    tn = 1024
    out = pl.pallas_call(
        _mod_kernel,
        grid=(depth, n // tn),
        in_specs=[pl.BlockSpec((rows, d), lambda l, j: (0, 0)),
                  pl.BlockSpec((1, d, tn), lambda l, j: (l, 0, j)),
                  pl.BlockSpec((1, 1, tn), lambda l, j: (l, 0, j))],
        out_specs=pl.BlockSpec((1, rows, tn), lambda l, j: (l, 0, j)),
        out_shape=jax.ShapeDtypeStruct((depth, rows, n), F32),
        compiler_params=_params(("arbitrary", "arbitrary")),
        name="modulation",
    )(cp, w_mod, b_mod.reshape(depth, 1, n))
    return out[:, :bsz]


def _ffn_kernel(x_ref, m_ref, g_ref, w1_hbm, w3_hbm, w2_hbm, *rest, tail, layer, which, tf):
    if tail == "final":
        fg_ref, o_ref, h_scr, w1_buf, w3_buf, w2_buf, sem = rest
    elif tail == "next":
        ng_ref, nm_ref, o_ref, hn_ref, h_scr, w1_buf, w3_buf, w2_buf, sem = rest
    else:
        o_ref, h_scr, w1_buf, w3_buf, w2_buf, sem = rest
    tm = x_ref.shape[0]
    rc = min(256, tm)
    n_tiles = w1_hbm.shape[-1] // tf

    def tile_copies(j, slot):
        cols = pl.multiple_of(j * tf, tf)
        d = w1_hbm.shape[-2]
        rk, rf = d // FFN_DMA_SPLIT, tf // FFN_DMA_SPLIT
        out = []
        for s in range(FFN_DMA_SPLIT):
            rows = pl.ds(s * rk, rk)
            out.append(pltpu.make_async_copy(w1_hbm.at[layer, which, rows, pl.ds(cols, tf)],
                                             w1_buf.at[slot, rows, :], sem.at[0, s, slot]))
            out.append(pltpu.make_async_copy(w3_hbm.at[layer, which, rows, pl.ds(cols, tf)],
                                             w3_buf.at[slot, rows, :], sem.at[1, s, slot]))
            out.append(pltpu.make_async_copy(w2_hbm.at[layer, which, pl.ds(cols + s * rf, rf), :],
                                             w2_buf.at[slot, pl.ds(s * rf, rf), :], sem.at[2, s, slot]))
        return out

    for cp in tile_copies(0, 0):
        cp.start()
    for r0 in range(0, tm, rc):
        h_scr[r0:r0 + rc, :] = _adaln(x_ref[r0:r0 + rc, :], g_ref[...], m_ref[0]).astype(BF16)

    def f_tile(j, slot, first=False):
        for cp in tile_copies(j, slot):
            cp.wait()

        @pl.when(j + 1 < n_tiles)
        def _():
            for cp in tile_copies(j + 1, 1 - slot):
                cp.start()

        h = h_scr[...]
        a = _dot(h, w1_buf[slot])
        b = _dot(h, w3_buf[slot])
        part = _dot(_silu(a) * b, w2_buf[slot])
        if first:
            o_ref[...] = part
        else:
            o_ref[...] += part

    def two_tiles(jj, carry):
        f_tile(2 * jj, 0)
        f_tile(2 * jj + 1, 1)
        return carry

    f_tile(jnp.int32(0), 0, first=True)
    f_tile(jnp.int32(1), 1)
    lax.fori_loop(1, n_tiles // 2, two_tiles, 0)

    for r0 in range(0, tm, rc):
        o = x_ref[r0:r0 + rc, :] + (0.5 * m_ref[0][2:3]) * o_ref[r0:r0 + rc, :]
        if tail == "final":
            o = _rms(o, fg_ref[...])
        o_ref[r0:r0 + rc, :] = o
        if tail == "next":
            hn_ref[r0:r0 + rc, :] = _adaln(o, ng_ref[...], nm_ref[0]).astype(hn_ref.dtype)


def _ffn(x, m, g, w1, w3, w2, layer, which, seqlen, final_g=None, next_ln=None):
    t, d = x.shape
    f = w1.shape[-1]
    tm, tf = min(1024, seqlen), 256
    assert f % (2 * tf) == 0
    per = seqlen // tm
    tail = "final" if final_g is not None else "next" if next_ln is not None else "plain"
    vec = pl.BlockSpec((1, d), lambda i: (0, 0))
    mod = pl.BlockSpec((1, 3, d), lambda i: (i // per, 0, 0))
    tok = pl.BlockSpec((tm, d), lambda i: (i, 0))
    hbm = pl.BlockSpec(memory_space=pl.ANY)
    x_spec = pl.BlockSpec((tm, d), lambda i: (i, 0), pipeline_mode=pl.Buffered(1)) if tail == "next" else tok
    in_specs = [x_spec, mod, vec, hbm, hbm, hbm]
    args = [x, m, g.reshape(1, d), w1, w3, w2]
    out_specs, out_shape = tok, jax.ShapeDtypeStruct((t, d), F32)
    if tail == "final":
        in_specs.append(vec)
        args.append(final_g.reshape(1, d))
    elif tail == "next":
        in_specs += [vec, mod]
        args += [next_ln[0].reshape(1, d), next_ln[1]]
        out_specs, out_shape = [tok, tok], [out_shape, jax.ShapeDtypeStruct((t, d), BF16)]
    return pl.pallas_call(
        functools.partial(_ffn_kernel, tail=tail, layer=layer, which=which, tf=tf),
        grid=(t // tm,),
        in_specs=in_specs,
        out_specs=out_specs,
        out_shape=out_shape,
        scratch_shapes=[pltpu.VMEM((tm, d), BF16),
                        pltpu.VMEM((2, d, tf), F32), pltpu.VMEM((2, d, tf), F32), pltpu.VMEM((2, tf, d), F32),
                        pltpu.SemaphoreType.DMA((3, FFN_DMA_SPLIT, 2))],
        compiler_params=_params(("arbitrary",), FFN_VMEM_LIMIT),
        name="ffn",
    )(*args)


def _matmul_kernel(a_ref, w_ref, o_ref, w_scr, *, valid, out_major):
    @pl.when(pl.program_id(1) == 0)
    def _():
        w = w_ref[...]
        n_axis = 0 if out_major else 1
        if valid < w.shape[n_axis]:
            w = jnp.where(lax.broadcasted_iota(jnp.int32, w.shape, n_axis) < valid, w, 0.0)
        w_scr[...] = w.astype(BF16)

    dot = _dot_nt if out_major else _dot
    o_ref[...] = dot(a_ref[...], w_scr[...]).astype(o_ref.dtype)


def _matmul(a, w, lead, col0=0, n=None, out_dtype=F32, out_major=False):
    m, k = a.shape
    n_total = w.shape[1] if out_major else w.shape[2]
    n = n_total - col0 if n is None else n
    n_pad = -(-n // LANES) * LANES
    tm = min(1024, m)
    tn = min(1024, n_pad)
    assert col0 % tn == 0 and n_pad % tn == 0
    cb = col0 // tn
    if out_major:
        w_spec = pl.BlockSpec((None, tn, k), lambda j, i: (lead, cb + j, 0))
        w_tile = (tn, k)
    else:
        w_spec = pl.BlockSpec((None, k, tn), lambda j, i: (lead, 0, cb + j))
        w_tile = (k, tn)
    return pl.pallas_call(
        functools.partial(_matmul_kernel, valid=min(tn, n), out_major=out_major),
        grid=(n_pad // tn, m // tm),
        in_specs=[pl.BlockSpec((tm, k), lambda j, i: (i, 0)), w_spec],
        out_specs=pl.BlockSpec((tm, tn), lambda j, i: (i, j)),
        out_shape=jax.ShapeDtypeStruct((m, n_pad), out_dtype),
        scratch_shapes=[pltpu.VMEM(w_tile, BF16)],
        compiler_params=_params(("parallel", "arbitrary")),
        name="matmul",
    )(a, w)


def _matmul_resid_kernel(*refs, n_in):
    a_refs = refs[:n_in]
    w_refs = refs[n_in:2 * n_in]
    x_ref, m_ref, o_ref = refs[2 * n_in:2 * n_in + 3]
    w_scrs = refs[2 * n_in + 3:]

    @pl.when(pl.program_id(1) == 0)
    def _():
        for w_ref, w_scr in zip(w_refs, w_scrs):
            w_scr[...] = w_ref[...].astype(BF16)

    acc = _dot(a_refs[0][...], w_scrs[0][...])
    for a_ref, w_scr in zip(a_refs[1:], w_scrs[1:]):
        acc = acc + _dot(a_ref[...], w_scr[...])
    o_ref[...] = x_ref[...] + m_ref[0][2:3] * acc


def _matmul_resid(a_list, w, x, m, seqlen):
    t, d = x.shape
    tm, tn = min(512, seqlen), 1024
    per = seqlen // tm
    n_in = len(a_list)
    kb = w.shape[0] // n_in
    arrays, in_specs = [], []
    for a in a_list:
        arr, blk = a if isinstance(a, tuple) else (a, 0)
        arrays.append(arr)
        in_specs.append(pl.BlockSpec((tm, kb), lambda j, i, blk=blk: (i, blk)))
    in_specs += [pl.BlockSpec((kb, tn), lambda j, i, r=r: (r, j)) for r in range(n_in)]
    in_specs += [pl.BlockSpec((tm, tn), lambda j, i: (i, j)),
                 pl.BlockSpec((1, 3, tn), lambda j, i: (i // per, 0, j))]
    return pl.pallas_call(
        functools.partial(_matmul_resid_kernel, n_in=n_in),
        grid=(d // tn, t // tm),
        in_specs=in_specs,
        out_specs=pl.BlockSpec((tm, tn), lambda j, i: (i, j)),
        out_shape=jax.ShapeDtypeStruct((t, d), F32),
        scratch_shapes=[pltpu.VMEM((kb, tn), BF16)] * n_in,
        compiler_params=_params(("parallel", "arbitrary")),
        name="matmul_resid",
    )(*arrays, *([w] * n_in), x, m)


def _s5_kernel(u_ref, lrg_ref, lig_ref, ldg_ref, lrl_ref, lil_ref, ldl_ref, btr_ref, bti_ref, cr_ref, ci_ref,
               o_ref, *, nc, levels):
    q, gc, p = S5_CHUNK, S5_GROUP_CH, S5_STATE
    ng = LANES // gc
    gp = ng * p
    rows = u_ref.shape[0] // q
    half = q // 2

    def discretize(lre, lim, ldt):
        lr = jnp.minimum(lre, -1e-4)
        dt = jnp.exp(ldt)
        mag = jnp.exp(lr * dt)
        lb_re, lb_im = mag * jnp.cos(lim * dt), mag * jnp.sin(lim * dt)
        den = lr * lr + lim * lim
        nr, ni = lb_re - 1.0, lb_im
        return lb_re, lb_im, (nr * lr + ni * lim) / den, (ni * lr - nr * lim) / den

    def powers(lb_re, lb_im, count):
        out = [(jnp.ones_like(lb_re), jnp.zeros_like(lb_im))]
        for _ in range(count - 1):
            pr, pi = out[-1]
            out.append((pr * lb_re - pi * lb_im, pr * lb_im + pi * lb_re))
        return out

    def split3(x):
        p1 = x.astype(BF16)
        r1 = x - p1.astype(F32)
        p2 = r1.astype(BF16)
        return p1, p2, (r1 - p2.astype(F32)).astype(BF16)

    ri = lax.broadcasted_iota(jnp.int32, (LANES, ng), 0)
    ci_ = lax.broadcasted_iota(jnp.int32, (LANES, ng), 1)
    to_rows = (ri // gc == ci_).astype(F32)
    expand = lambda a: _dot(to_rows, a, hi=True)
    pi_ = lax.broadcasted_iota(jnp.int32, (3 * p, gp), 0)
    pj_ = lax.broadcasted_iota(jnp.int32, (3 * p, gp), 1)
    to_lanes = (pi_ % p == pj_ % p).astype(BF16)
    tile = lambda a: jnp.dot(jnp.concatenate(split3(a), axis=1), to_lanes, preferred_element_type=F32)
    same_rc = (lax.broadcasted_iota(jnp.int32, (LANES, LANES), 0) // gc
               == lax.broadcasted_iota(jnp.int32, (LANES, LANES), 1) // gc)
    same_rl = (lax.broadcasted_iota(jnp.int32, (LANES, gp), 0) // gc
               == lax.broadcasted_iota(jnp.int32, (LANES, gp), 1) // p)

    lb_re_r, lb_im_r, f_re_r, f_im_r = [expand(a) for a in discretize(lrg_ref[...], lig_ref[...], ldg_ref[...])]
    lb_re_l, lb_im_l, f_re_l, f_im_l = discretize(lrl_ref[0], lil_ref[0], ldl_ref[0])
    pow_r = powers(lb_re_r, lb_im_r, q)
    pow_l = powers(lb_re_l, lb_im_l, q + 1)

    btr, bti = btr_ref[...], bti_ref[...]
    cr, ci = cr_ref[...], ci_ref[...]
    bb_re = f_re_r * btr - f_im_r * bti
    bb_im = f_re_r * bti + f_im_r * btr

    b_hi, b_lo, _ = split3(jnp.concatenate([bb_re, -bb_im], axis=1))
    b_cat = jnp.concatenate([b_hi, b_lo, b_hi], axis=1)
    w = []
    for pr, pi in pow_r:
        c_hi, c_lo, _ = split3(jnp.concatenate([cr * pr - ci * pi, cr * pi + ci * pr], axis=1))
        k_t = _dot_nt(b_cat, jnp.concatenate([c_hi, c_hi, c_lo], axis=1))
        w.append(jnp.where(same_rc, k_t, 0.0).astype(BF16))
    zero = jnp.zeros((LANES, LANES), BF16)

    def w_pair(delta):
        lo = w[2 * delta - 1] if delta > 0 else zero
        return jnp.concatenate([jnp.concatenate([w[2 * delta], w[2 * delta + 1]], axis=1),
                                jnp.concatenate([lo, w[2 * delta]], axis=1)], axis=0)

    u_pair = []
    for jp in range(half):
        u_pair.append(jnp.concatenate(
            [u_ref[pl.ds(2 * jp + jj, rows, stride=q), :].astype(BF16) for jj in range(2)], axis=1))

    bt_re, bt_im = tile(btr), tile(bti)
    bbl_re = jnp.where(same_rl, f_re_l * bt_re - f_im_l * bt_im, 0.0)
    bbl_im = jnp.where(same_rl, f_re_l * bt_im + f_im_l * bt_re, 0.0)
    st = None
    for jp in range(half):
        blocks = []
        for j in (2 * jp, 2 * jp + 1):
            pr, pi = pow_l[q - 1 - j]
            blocks.append(jnp.concatenate([bbl_re * pr - bbl_im * pi, bbl_re * pi + bbl_im * pr], axis=1))
        term = _dot(u_pair[jp], jnp.concatenate(blocks, axis=0))
        st = term if st is None else st + term
    sr, si = st[:, :gp], st[:, gp:]

    cidx = lax.broadcasted_iota(jnp.int32, (rows, 1), 0) % nc
    ar, ai = pow_l[q]
    for k in range(levels):
        sh = 1 << k
        if k > 0:
            ar, ai = ar * ar - ai * ai, 2.0 * ar * ai
        xr = pltpu.roll(sr, sh, axis=0)
        xi = pltpu.roll(si, sh, axis=0)
        ok = cidx >= sh
        sr, si = (sr + jnp.where(ok, ar * xr - ai * xi, 0.0),
                  si + jnp.where(ok, ar * xi + ai * xr, 0.0))
    ok = cidx >= 1
    s_prev = jnp.concatenate([jnp.where(ok, pltpu.roll(sr, 1, axis=0), 0.0),
                              jnp.where(ok, pltpu.roll(si, 1, axis=0), 0.0)], axis=1).astype(BF16)

    ct_re, ct_im = tile(cr), tile(ci)
    ct_re = jnp.where(same_rl, ct_re, 0.0)
    ct_im = jnp.where(same_rl, ct_im, 0.0)
    for ip in range(half):
        acc = None
        for jp in range(ip + 1):
            term = _dot(u_pair[jp], w_pair(ip - jp))
            acc = term if acc is None else acc + term
        blocks = []
        for i in (2 * ip, 2 * ip + 1):
            pr, pi = pow_l[i + 1]
            blocks.append(jnp.concatenate([ct_re * pr - ct_im * pi, -(ct_re * pi + ct_im * pr)], axis=1))
        acc = acc + _dot_nt(s_prev, jnp.concatenate(blocks, axis=0))
        o_ref[pl.ds(2 * ip, rows, stride=q), :] = acc[:, :LANES]
        o_ref[pl.ds(2 * ip + 1, rows, stride=q), :] = acc[:, LANES:]


def _s5_post_kernel(y_ref, u_ref, d_ref, w_ref, b_ref, o_ref):
    y = y_ref[...] + d_ref[...] * u_ref[...]
    y = jax.nn.gelu(y)
    o_ref[...] = (y * jax.nn.sigmoid(_dot(y, w_ref[...]) + b_ref[...])).astype(o_ref.dtype)


def _s5_branch(u, bsz, seqlen, lam_re, lam_im, log_dt, b_re, b_im, c_re, c_im, d_skip, glu_w, glu_b):
    t, width = u.shape
    q, gc, p = S5_CHUNK, S5_GROUP_CH, S5_STATE
    groups = width // gc
    nc = seqlen // q
    levels = int(math.log2(nc))
    assert (1 << levels) == nc
    ng = LANES // gc
    nblk = width // LANES
    gp = ng * p
    ldt = jnp.broadcast_to(log_dt[:, None], (groups, p))
    by_group = pl.BlockSpec((ng, p), lambda k: (k, 0))
    by_lane = pl.BlockSpec((1, 1, gp), lambda k: (k, 0, 0))
    by_chan = pl.BlockSpec((LANES, p), lambda k: (k, 0))
    tok = pl.BlockSpec((t, LANES), lambda k: (0, k))
    lanes3 = lambda a: a.reshape(nblk, 1, gp)
    chan2 = lambda a: a.reshape(width, p)
    y = pl.pallas_call(
        functools.partial(_s5_kernel, nc=nc, levels=levels),
        grid=(nblk,),
        in_specs=[tok, by_group, by_group, by_group, by_lane, by_lane, by_lane,
                  by_chan, by_chan, by_chan, by_chan],
        out_specs=tok,
        out_shape=jax.ShapeDtypeStruct((t, width), F32),
        compiler_params=_params(("parallel",)),
        name="s5_conv",
    )(u, lam_re, lam_im, ldt, lanes3(lam_re), lanes3(lam_im), lanes3(ldt),
      chan2(jnp.swapaxes(b_re, 1, 2)), chan2(jnp.swapaxes(b_im, 1, 2)), chan2(c_re), chan2(c_im))

    tm = min(512, t)
    return pl.pallas_call(
        _s5_post_kernel,
        grid=(t // tm,),
        in_specs=[pl.BlockSpec((tm, width), lambda i: (i, 0)),
                  pl.BlockSpec((tm, width), lambda i: (i, 0)),
                  pl.BlockSpec((1, width), lambda i: (0, 0)),
                  pl.BlockSpec((width, width), lambda i: (0, 0)),
                  pl.BlockSpec((1, width), lambda i: (0, 0))],
        out_specs=pl.BlockSpec((tm, width), lambda i: (i, 0)),
        out_shape=jax.ShapeDtypeStruct((t, width), BF16),
        compiler_params=_params(("parallel",)),
        name="s5_post",
    )(y, u, d_skip.reshape(1, width), glu_w.astype(BF16), glu_b.reshape(1, width))


def _ssd_kernel(z_ref, xbc_ref, dt_ref, cw_ref, cb_ref, dtb_ref, alog_ref, dsk_ref, ng_ref,
                o_ref, ext_scr, st_scr, *, heads, inner):
    lc = SSD_CHUNK
    hd, ns = SSD_HEAD_DIM, SSD_STATE
    rpg = heads // SSD_GROUPS
    gw = SSD_GROUPS * ns
    assert SSD_CONV - 1 <= 8

    @pl.when(pl.program_id(1) == 0)
    def _():
        ext_scr[0:8, :] = jnp.zeros((8, ext_scr.shape[1]), F32)
        st_scr[...] = jnp.zeros_like(st_scr)

    ext_scr[8:8 + lc, :] = xbc_ref[...]
    cw = cw_ref[...]
    ext = ext_scr[0:8 + lc, :]
    acc = cw[0:1] * ext
    for k in range(1, SSD_CONV):
        acc = cw[k:k + 1] * ext + pltpu.roll(acc, 1, axis=0)
    conv = cb_ref[...] + acc[8:8 + lc]
    ext_scr[0:8, :] = ext_scr[lc:lc + 8, :]
    act = _silu(conv)
    xs = act[:, :inner]
    bs = act[:, inner:inner + gw]
    cs = act[:, inner + gw:]

    dt = _softplus(dt_ref[...] + dtb_ref[...])
    adt = dt * (-jnp.exp(alog_ref[...]))
    ri = lax.broadcasted_iota(jnp.int32, (lc, lc), 0)
    ci = lax.broadcasted_iota(jnp.int32, (lc, lc), 1)
    causal = ri >= ci
    a_cum = _dot(causal.astype(F32), adt, hi=True)
    a_cum_t = a_cum.T
    dt_t = dt.T

    y_parts = []
    for g in range(SSD_GROUPS):
        cs_g = cs[:, g * ns:(g + 1) * ns]
        bs_g = bs[:, g * ns:(g + 1) * ns]
        cb = _dot_nt(cs_g, bs_g)
        bs_gt = bs_g.T
        for r in range(rpg):
            h = g * rpg + r
            col = a_cum[:, h:h + 1]
            row = a_cum_t[h:h + 1, :]
            dt_row = dt_t[h:h + 1, :]
            decay = jnp.exp(jnp.where(causal, col - row, -jnp.inf))
            xs_h = xs[:, h * hd:(h + 1) * hd]
            st = st_scr[h]
            lhs = jnp.concatenate([cb * decay * dt_row, cs_g * jnp.exp(col)], axis=1)
            y_parts.append(_dot(lhs, jnp.concatenate([xs_h, st], axis=0)))
            last = row[:, lc - 1:lc]
            to_end = jnp.exp(last - row)
            st_scr[h] = jnp.exp(last) * st + _dot(bs_gt * (to_end * dt_row), xs_h)
    y = jnp.concatenate(y_parts, axis=1) + dsk_ref[...] * xs
    y = y * _silu(z_ref[...])
    o_ref[...] = _rms(y, ng_ref[...]).astype(o_ref.dtype)


def _ssd_branch(z, xbc, dt_raw, bsz, seqlen, conv_w, conv_b, dt_bias, a_log, d_skip, norm_g):
    t, inner = z.shape
    cd = xbc.shape[1]
    heads = inner // SSD_HEAD_DIM
    lc = SSD_CHUNK
    nc = seqlen // lc
    pad = LANES - heads
    padv = lambda v: jnp.pad(v.reshape(1, heads), ((0, 0), (0, pad)))
    row = lambda n: pl.BlockSpec((1, n), lambda b, c: (0, 0))
    tok = lambda n: pl.BlockSpec((lc, n), lambda b, c: (b * nc + c, 0))
    return pl.pallas_call(
        functools.partial(_ssd_kernel, heads=heads, inner=inner),
        grid=(bsz, nc),
        in_specs=[tok(inner), tok(cd), tok(LANES),
                  pl.BlockSpec((SSD_CONV, cd), lambda b, c: (0, 0)), row(cd),
                  row(LANES), row(LANES), row(inner), row(inner)],
        out_specs=tok(inner),
        out_shape=jax.ShapeDtypeStruct((t, inner), BF16),
        scratch_shapes=[pltpu.VMEM((8 + lc + 8, cd), F32),
                        pltpu.VMEM((heads, SSD_STATE, SSD_HEAD_DIM), F32)],
        compiler_params=_params(("parallel", "arbitrary")),
        name="ssd",
    )(z, xbc, dt_raw, conv_w, conv_b.reshape(1, cd), padv(dt_bias), padv(a_log),
      jnp.repeat(d_skip, SSD_HEAD_DIM).reshape(1, inner), norm_g.reshape(1, inner))


def _hybrid_mixer(x, h, m, seqlen, idx, w_in, w_out, lam_re, lam_im, log_dt, b_re, b_im, c_re, c_im, s5_d,
                  glu_w, glu_b, conv_w, conv_b, dt_bias, a_log, ssd_d, ssd_norm_g):
    t, d = x.shape
    bsz = t // seqlen
    s5w = s5_d.shape[0]
    inner = ssd_norm_g.shape[0]
    cd = conv_w.shape[1]
    heads = dt_bias.shape[0]
    o1, o2, o3 = s5w, s5w + inner, s5w + inner + cd
    w_t = jnp.swapaxes(w_in, 1, 2)
    u = _matmul(h, w_t, idx, 0, o1, out_major=True)
    z = _matmul(h, w_t, idx, o1, inner, out_major=True)
    xbc = _matmul(h, w_t, idx, o2, cd, out_major=True)
    dt_raw = _matmul(h, w_t, idx, o3, heads, out_major=True)
    y_s5 = _s5_branch(u, bsz, seqlen, lam_re, lam_im, log_dt, b_re, b_im, c_re, c_im, s5_d, glu_w, glu_b)
    y_ssd = _ssd_branch(z, xbc, dt_raw, bsz, seqlen, conv_w, conv_b, dt_bias, a_log, ssd_d, ssd_norm_g)
    assert inner % s5w == 0
    return _matmul_resid([y_s5] + [(y_ssd, i) for i in range(inner // s5w)], w_out, x, m, seqlen)


def _rwkv_mix_kernel(x_ref, m_ref, g_ref, mu_ref, w1_ref, w2_ref, w0_ref, a1_ref, a2_ref, a0_ref, g1_ref, g2_ref,
                     xr_ref, xk_ref, xv_ref, lw_ref, a_ref, gate_ref, prev_scr):
    h = _adaln(x_ref[...], g_ref[...], m_ref[0])
    tm = h.shape[0]

    @pl.when(pl.program_id(1) == 0)
    def _():
        prev_scr[...] = jnp.zeros_like(prev_scr)

    first = lax.broadcasted_iota(jnp.int32, (tm, 1), 0) == 0
    shifted = jnp.where(first, prev_scr[7:8, :], pltpu.roll(h, 1, axis=0))
    prev_scr[...] = h[tm - 8:tm]
    xx = shifted - h
    mu = mu_ref[...]
    mix = lambda i: h + xx * mu[i:i + 1]
    xr_ref[...] = mix(0).astype(xr_ref.dtype)
    xk_ref[...] = mix(2).astype(xk_ref.dtype)
    xv_ref[...] = mix(3).astype(xv_ref.dtype)
    w = -_softplus(-(w0_ref[...] + _dot(jnp.tanh(_dot(mix(1), w1_ref[...])), w2_ref[...]))) - 0.5
    lw_ref[...] = -jnp.exp(w)
    a_ref[...] = jax.nn.sigmoid(a0_ref[...] + _dot(_dot(mix(4), a1_ref[...]), a2_ref[...]))
    gate_ref[...] = _dot(jax.nn.sigmoid(_dot(mix(5), g1_ref[...])), g2_ref[...])


def _rwkv_mix(x, m, g, mu, w1, w2, w0, a1, a2, a0, g1, g2, bsz, seqlen):
    t, d = x.shape
    tm = min(256, seqlen)
    per = seqlen // tm
    tok = pl.BlockSpec((tm, d), lambda b, i: (b * per + i, 0))
    vec = pl.BlockSpec((1, d), lambda b, i: (0, 0))
    whole = lambda a: pl.BlockSpec(a.shape, lambda b, i: (0, 0))

    def lora(a, b):
        rank = a.shape[1]
        rp = -(-rank // LANES) * LANES
        return (jnp.pad(a, ((0, 0), (0, rp - rank))).astype(BF16),
                jnp.pad(b, ((0, rp - rank), (0, 0))).astype(BF16))

    (w1p, w2p), (a1p, a2p), (g1p, g2p) = lora(w1, w2), lora(a1, a2), lora(g1, g2)
    return pl.pallas_call(
        _rwkv_mix_kernel,
        grid=(bsz, per),
        in_specs=[tok, pl.BlockSpec((1, 3, d), lambda b, i: (b, 0, 0)), vec,
                  pl.BlockSpec((8, d), lambda b, i: (0, 0)),
                  whole(w1p), whole(w2p), vec, whole(a1p), whole(a2p), vec, whole(g1p), whole(g2p)],
        out_specs=[tok] * 6,
        out_shape=[jax.ShapeDtypeStruct((t, d), BF16)] * 3 + [jax.ShapeDtypeStruct((t, d), F32)] * 3,
        scratch_shapes=[pltpu.VMEM((8, d), F32)],
        compiler_params=_params(("parallel", "arbitrary")),
        name="rwkv_mix",
    )(x, m, g.reshape(1, d), jnp.pad(mu, ((0, 2), (0, 0))),
      w1p, w2p, w0.reshape(1, d), a1p, a2p, a0.reshape(1, d), g1p, g2p)


def _rwkv_scan_kernel(r_ref, k_ref, v_ref, lw_ref, a_ref, g_ref, kk_ref, ka_ref, rk_ref, lg_ref, lb_ref,
                      o_ref, s_scr):
    c = RWKV_CHUNK
    hd = RWKV_HEAD_DIM
    c2 = 2 * c
    tokens = r_ref.shape[0]
    pairs = r_ref.shape[1] // LANES
    tb = pairs * tokens
    per_pair = tokens // c
    n_chunks = tb // c

    @pl.when(pl.program_id(2) == 0)
    def _():
        s_scr[...] = jnp.zeros_like(s_scr)

    def rows_of(ref):
        return jnp.concatenate([ref[:, p * LANES:(p + 1) * LANES] for p in range(pairs)], axis=0)

    def param_rows(ref):
        return jnp.concatenate([jnp.broadcast_to(ref[:, p * LANES:(p + 1) * LANES], (tokens, LANES))
                                for p in range(pairs)], axis=0)

    lane_head = lax.broadcasted_iota(jnp.int32, (1, LANES), 1) // hd
    head0 = lane_head == 0
    same_head = ((lax.broadcasted_iota(jnp.int32, (LANES, LANES), 0) // hd)
                 == (lax.broadcasted_iota(jnp.int32, (LANES, LANES), 1) // hd))
    r2 = lax.broadcasted_iota(jnp.int32, (1, c2, c2), 1)
    q2 = lax.broadcasted_iota(jnp.int32, (1, c2, c2), 2)
    same_blk = (r2 // c) == (q2 // c)
    strict = same_blk & ((r2 % c) > (q2 % c))
    incl = same_blk & ((r2 % c) >= (q2 % c))
    eye2 = (r2 == q2).astype(F32)
    k_k, k_a, r_k = param_rows(kk_ref), param_rows(ka_ref), param_rows(rk_ref)

    def stack(x):
        x = x.reshape(n_chunks, c, LANES)
        return jnp.concatenate([jnp.where(head0, x, 0.0), jnp.where(head0, 0.0, x)], axis=1)

    def unstack(x):
        return (x[:, :c] + x[:, c:]).reshape(tb, LANES)

    def bdot(p, q, dims):
        return lax.dot_general(p.astype(BF16), q.astype(BF16), (dims, ((0,), (0,))), preferred_element_type=F32)

    bmm = lambda p, q: bdot(p, q, ((2,), (1,)))
    bmm_nt = lambda p, q: bdot(p, q, ((2,), (2,)))
    bmm_tn = lambda p, q: bdot(p, q, ((1,), (1,)))

    def split(x):
        hi_part = x.astype(BF16)
        return hi_part, (x - hi_part.astype(F32)).astype(BF16)

    def head_sum(x):
        s0 = jnp.sum(jnp.where(head0, x, 0.0), axis=1, keepdims=True)
        s1 = jnp.sum(jnp.where(head0, 0.0, x), axis=1, keepdims=True)
        return jnp.where(head0, s0, s1)

    def cumsum_rows(x):
        row_in_chunk = lax.broadcasted_iota(jnp.int32, (tb, 1), 0) % c
        shift = 1
        while shift < c:
            x = x + jnp.where(row_in_chunk >= shift, pltpu.roll(x, shift, axis=0), 0.0)
            shift *= 2
        return x

    def lhs3(hi_part, lo_part):
        return jnp.concatenate([hi_part, lo_part], axis=2)

    def rhs3(hi_part, lo_part):
        top = jnp.concatenate([hi_part, lo_part], axis=2)
        bot = jnp.concatenate([hi_part, jnp.zeros_like(lo_part)], axis=2)
        return jnp.concatenate([top, bot], axis=1)

    def fold(x):
        return x[:, :, :c2] + x[:, :, c2:]

    r, k, v, lw, a = rows_of(r_ref), rows_of(k_ref), rows_of(v_ref), rows_of(lw_ref), rows_of(a_ref)
    kk = k * k_k
    k2 = k * (1.0 + (a - 1.0) * k_a)
    rk_sum = head_sum(r * k2 * r_k)
    kk = kk / jnp.maximum(jnp.sqrt(head_sum(kk * kk)), 1e-12)
    bv = kk * a
    cum = cumsum_rows(lw)
    cum3 = cum.reshape(n_chunks, c, LANES)
    tot = jnp.broadcast_to(cum3[:, c - 1:c, :], cum3.shape).reshape(tb, LANES)
    g_inv = jnp.exp(-cum)
    rt_u = r * jnp.exp(cum)
    at = stack(-kk * jnp.exp(cum - lw))
    rt = stack(rt_u)
    bt = stack(bv * g_inv)
    kt = stack(k2 * g_inv)
    to_end = jnp.exp(tot - cum)
    b_end = (bv * to_end).reshape(n_chunks, c, LANES)
    k_end = (k2 * to_end).reshape(n_chunks, c, LANES)
    vs = stack(v)

    gram = bmm_nt(jnp.concatenate([at, rt], axis=1), jnp.concatenate([bt, kt], axis=1))
    n_ab = jnp.where(strict, gram[:, :c2, :c2], 0.0)
    a_ak = jnp.where(strict, gram[:, :c2, c2:], 0.0)
    a_rb = jnp.where(incl, gram[:, c2:, :c2], 0.0)
    a_rk = jnp.where(incl, gram[:, c2:, c2:], 0.0)
    pw_h, pw_l = split(n_ab)
    pw = fold(bmm(lhs3(pw_h, pw_l), rhs3(pw_h, pw_l)))
    tinv = eye2 + n_ab
    for _ in range(int(math.log2(c)) - 2):
        pw_h, pw_l = split(pw)
        t_h, t_l = split(tinv)
        both = fold(bmm(jnp.concatenate([lhs3(pw_h, pw_l), lhs3(t_h, t_l)], axis=1), rhs3(pw_h, pw_l)))
        pw = both[:, :c2]
        tinv = tinv + both[:, c2:]
    pw_h, pw_l = split(pw)
    t_h, t_l = split(tinv)
    tinv = tinv + fold(bmm(lhs3(t_h, t_l), rhs3(pw_h, pw_l)))
    pq = bmm(tinv, jnp.concatenate([at, bmm(a_ak, vs)], axis=2))
    ry = bmm(jnp.concatenate([a_rb, a_rk], axis=2),
             jnp.concatenate([pq, jnp.concatenate([jnp.zeros_like(vs), vs], axis=2)], axis=1))
    rq = rt_u + unstack(ry[:, :, :LANES])
    y0 = unstack(ry[:, :, LANES:])
    p_u = unstack(pq[:, :, :LANES]).reshape(n_chunks, c, LANES)
    q_u = unstack(pq[:, :, LANES:]).reshape(n_chunks, c, LANES)
    gm = jnp.where(same_head, bmm_tn(p_u, b_end), 0.0)
    dm = jnp.where(same_head, bmm_tn(jnp.concatenate([q_u, v.reshape(n_chunks, c, LANES)], axis=1),
                                     jnp.concatenate([b_end, k_end], axis=1)), 0.0)
    g_tot = jnp.exp(cum3[:, c - 1:c, :])
    bonus = rk_sum * v

    s = [s_scr[p] for p in range(pairs)]
    ys = [[] for _ in range(pairs)]
    for i in range(per_pair):
        for p in range(pairs):
            n = p * per_pair + i
            ys[p].append(_dot_nt(rq[n * c:(n + 1) * c], s[p]))
            s[p] = s[p] * g_tot[n] + _dot(s[p], gm[n]) + dm[n]
    for p in range(pairs):
        s_scr[p] = s[p]
    y = jnp.concatenate([y_c for y_p in ys for y_c in y_p], axis=0) + y0
    mean = head_sum(y) * (1.0 / hd)
    dy = y - mean
    var = head_sum(dy * dy) * (1.0 / hd)
    yn = dy * lax.rsqrt(var + RWKV_GN_EPS) * param_rows(lg_ref) + param_rows(lb_ref)
    out = ((yn + bonus) * rows_of(g_ref)).astype(o_ref.dtype)
    for p in range(pairs):
        o_ref[:, p * LANES:(p + 1) * LANES] = out[p * tokens:(p + 1) * tokens]


def _rwkv_scan(r, k, v, lw, a, g, k_k, k_a, r_k, ln_g, ln_b, bsz, seqlen):
    t, d = r.shape
    tb = min(RWKV_TOKEN_BLOCK, seqlen)
    per = seqlen // tb
    pairs = RWKV_PAIRS_PER_BLOCK
    bw = pairs * LANES
    tok = pl.BlockSpec((tb, bw), lambda b, h, i: (b * per + i, h))
    row = pl.BlockSpec((1, bw), lambda b, h, i: (0, h))
    vec = lambda p: p.reshape(1, d)
    return pl.pallas_call(
        _rwkv_scan_kernel,
        grid=(bsz, d // bw, per),
        in_specs=[tok] * 6 + [row] * 5,
        out_specs=tok,
        out_shape=jax.ShapeDtypeStruct((t, d), BF16),
        scratch_shapes=[pltpu.VMEM((pairs, LANES, LANES), F32)],
        compiler_params=_params(("parallel", "parallel", "arbitrary")),
        name="rwkv_scan",
    )(r, k, v, lw, a, g, vec(k_k), vec(k_a), vec(r_k), vec(ln_g), vec(ln_b))


def _rwkv_mixer(x, m, g, seqlen, idx, mu, w_r, w_k, w_v, w_o, w0, w1, w2, a0, a1, a2, g1, g2,
                k_k, k_a, r_k, ln_g, ln_b):
    t, d = x.shape
    bsz = t // seqlen
    xr, xk, xv, lw, a, gate = _rwkv_mix(x, m, g, mu, w1, w2, w0, a1, a2, a0, g1, g2, bsz, seqlen)
    r = _matmul(xr, w_r, idx)
    k = _matmul(xk, w_k, idx)
    v = _matmul(xv, w_v, idx)
    y = _rwkv_scan(r, k, v, lw, a, gate, k_k, k_a, r_k.reshape(-1), ln_g, ln_b, bsz, seqlen)
    return _matmul_resid([y], w_o, x, m, seqlen)


def kernel(x, c, w_mod, b_mod, norm_g, ffn_w1, ffn_w3, ffn_w2, hyb_w_in, hyb_w_out, s5_lambda_re, s5_lambda_im, s5_log_dt, s5_b_re, s5_b_im, s5_c_re, s5_c_im, s5_d, s5_glu_w, s5_glu_b, ssd_conv_w, ssd_conv_b, ssd_dt_bias, ssd_a_log, ssd_d, ssd_norm_g, rwkv_mu, rwkv_w_r, rwkv_w_k, rwkv_w_v, rwkv_w_o, rwkv_w0, rwkv_w1, rwkv_w2, rwkv_a0, rwkv_a1, rwkv_a2, rwkv_g1, rwkv_g2, rwkv_k_k, rwkv_k_a, rwkv_r_k, rwkv_ln_g, rwkv_ln_b, final_g):
    bsz, seqlen, d = x.shape
    depth = w_mod.shape[0]
    xf = x.reshape(bsz * seqlen, d)
    mod = _modulation(c, w_mod, b_mod).reshape(depth, bsz, 3, 3, d)
    for layer in range(depth):
        i = layer // 2
        m0, m1, m2 = mod[layer, :, 0], mod[layer, :, 1], mod[layer, :, 2]
        if layer % 2 == 0:
            xf, h = _ffn(xf, m0, norm_g[layer, 0], ffn_w1, ffn_w3, ffn_w2, layer, 0, seqlen,
                         next_ln=(norm_g[layer, 1], m1))
            xf = _hybrid_mixer(xf, h, m1, seqlen, i, hyb_w_in, hyb_w_out[i], s5_lambda_re[i],
                               s5_lambda_im[i], s5_log_dt[i], s5_b_re[i], s5_b_im[i], s5_c_re[i], s5_c_im[i],
                               s5_d[i], s5_glu_w[i], s5_glu_b[i], ssd_conv_w[i], ssd_conv_b[i], ssd_dt_bias[i],
                               ssd_a_log[i], ssd_d[i], ssd_norm_g[i])
        else:
            xf = _ffn(xf, m0, norm_g[layer, 0], ffn_w1, ffn_w3, ffn_w2, layer, 0, seqlen)
            xf = _rwkv_mixer(xf, m1, norm_g[layer, 1], seqlen, i, rwkv_mu[i], rwkv_w_r, rwkv_w_k, rwkv_w_v,
                             rwkv_w_o[i], rwkv_w0[i], rwkv_w1[i], rwkv_w2[i], rwkv_a0[i], rwkv_a1[i], rwkv_a2[i],
                             rwkv_g1[i], rwkv_g2[i], rwkv_k_k[i], rwkv_k_a[i], rwkv_r_k[i], rwkv_ln_g[i],
                             rwkv_ln_b[i])
        xf = _ffn(xf, m2, norm_g[layer, 2], ffn_w1, ffn_w3, ffn_w2, layer, 1, seqlen,
                  final_g=final_g if layer == depth - 1 else None)
    return xf.reshape(bsz, seqlen, d)
```

```python
import functools
import math

import jax
import jax.numpy as jnp
from jax import lax
from jax.experimental import pallas as pl
from jax.experimental.pallas import tpu as pltpu

F32 = jnp.float32
BF16 = jnp.bfloat16
HI = lax.Precision.HIGHEST

RMS_EPS = 1e-6
RWKV_GN_EPS = 64e-5

S5_GROUP_CH = 16
S5_STATE = 64
S5_CHUNK = 16
SSD_HEAD_DIM = 64
SSD_STATE = 128
SSD_GROUPS = 8
SSD_CHUNK = 128
SSD_CONV = 4
RWKV_HEAD_DIM = 64
RWKV_CHUNK = 64
RWKV_TOKEN_BLOCK = 256
RWKV_PAIRS_PER_BLOCK = 4
LANES = 128
VMEM_LIMIT = 56 * 1024 * 1024
FFN_VMEM_LIMIT = 58 * 1024 * 1024
FFN_DMA_SPLIT = 4

def _params(sem, vmem=VMEM_LIMIT):
    return pltpu.CompilerParams(dimension_semantics=sem, vmem_limit_bytes=vmem)


def _dot(a, b, hi=False):
    dn = (((1,), (0,)), ((), ()))
    if hi:
        return lax.dot_general(a, b, dn, precision=HI, preferred_element_type=F32)
    return lax.dot_general(a.astype(BF16), b.astype(BF16), dn, preferred_element_type=F32)


def _dot_nt(a, b, hi=False):
    dn = (((1,), (1,)), ((), ()))
    if hi:
        return lax.dot_general(a, b, dn, precision=HI, preferred_element_type=F32)
    return lax.dot_general(a.astype(BF16), b.astype(BF16), dn, preferred_element_type=F32)


def _dot_tn(a, b):
    dn = (((0,), (0,)), ((), ()))
    return lax.dot_general(a.astype(BF16), b.astype(BF16), dn, preferred_element_type=F32)


def _silu(x):
    return x * jax.nn.sigmoid(x)


def _softplus(x):
    return jnp.maximum(x, 0.0) + jnp.log1p(jnp.exp(-jnp.abs(x)))


def _rms(x, g):
    return x * lax.rsqrt(jnp.mean(x * x, axis=-1, keepdims=True) + RMS_EPS) * g


def _adaln(x, g, m):
    return _rms(x, g) * (1.0 + m[1:2]) + m[0:1]


def _mod_kernel(c_ref, w_ref, b_ref, o_ref):
    c = c_ref[...]
    o_ref[0] = _dot(_silu(c), w_ref[0]) + b_ref[0]


def _modulation(c, w_mod, b_mod):
    depth, d, n = w_mod.shape
    bsz = c.shape[0]
    rows = 8
    cp = jnp.pad(c, ((0, rows - bsz), (0, 0)))
    tn = 1024
    out = pl.pallas_call(
        _mod_kernel,
        grid=(depth, n // tn),
        in_specs=[pl.BlockSpec((rows, d), lambda l, j: (0, 0)),
                  pl.BlockSpec((1, d, tn), lambda l, j: (l, 0, j)),
                  pl.BlockSpec((1, 1, tn), lambda l, j: (l, 0, j))],
        out_specs=pl.BlockSpec((1, rows, tn), lambda l, j: (l, 0, j)),
        out_shape=jax.ShapeDtypeStruct((depth, rows, n), F32),
        compiler_params=_params(("arbitrary", "arbitrary")),
        name="modulation",
    )(cp, w_mod, b_mod.reshape(depth, 1, n))
    return out[:, :bsz]


def _ffn_kernel(x_ref, m_ref, g_ref, w1_hbm, w3_hbm, w2_hbm, *rest, tail, layer, which, tf):
    if tail == "final":
        fg_ref, o_ref, h_scr, w1_buf, w3_buf, w2_buf, sem = rest
    elif tail == "next":
        ng_ref, nm_ref, o_ref, hn_ref, h_scr, w1_buf, w3_buf, w2_buf, sem = rest
    else:
        o_ref, h_scr, w1_buf, w3_buf, w2_buf, sem = rest
    tm = x_ref.shape[0]
    rc = min(256, tm)
    n_tiles = w1_hbm.shape[-1] // tf

    def tile_copies(j, slot):
        cols = pl.multiple_of(j * tf, tf)
        d = w1_hbm.shape[-2]
        rk, rf = d // FFN_DMA_SPLIT, tf // FFN_DMA_SPLIT
        out = []
        for s in range(FFN_DMA_SPLIT):
            rows = pl.ds(s * rk, rk)
            out.append(pltpu.make_async_copy(w1_hbm.at[layer, which, rows, pl.ds(cols, tf)],
                                             w1_buf.at[slot, rows, :], sem.at[0, s, slot]))
            out.append(pltpu.make_async_copy(w3_hbm.at[layer, which, rows, pl.ds(cols, tf)],
                                             w3_buf.at[slot, rows, :], sem.at[1, s, slot]))
            out.append(pltpu.make_async_copy(w2_hbm.at[layer, which, pl.ds(cols + s * rf, rf), :],
                                             w2_buf.at[slot, pl.ds(s * rf, rf), :], sem.at[2, s, slot]))
        return out

    for cp in tile_copies(0, 0):
        cp.start()
    for r0 in range(0, tm, rc):
        h_scr[r0:r0 + rc, :] = _adaln(x_ref[r0:r0 + rc, :], g_ref[...], m_ref[0]).astype(BF16)
    o_ref[...] = jnp.zeros_like(o_ref)

    def f_tile(j, slot):
        for cp in tile_copies(j, slot):
            cp.wait()

        @pl.when(j + 1 < n_tiles)
        def _():
            for cp in tile_copies(j + 1, 1 - slot):
                cp.start()

        h = h_scr[...]
        a = _dot(h, w1_buf[slot])
        b = _dot(h, w3_buf[slot])
        o_ref[...] += _dot(_silu(a) * b, w2_buf[slot])

    def two_tiles(jj, carry):
        f_tile(2 * jj, 0)
        f_tile(2 * jj + 1, 1)
        return carry

    lax.fori_loop(0, n_tiles // 2, two_tiles, 0)

    for r0 in range(0, tm, rc):
        o = x_ref[r0:r0 + rc, :] + (0.5 * m_ref[0][2:3]) * o_ref[r0:r0 + rc, :]
        if tail == "final":
            o = _rms(o, fg_ref[...])
        o_ref[r0:r0 + rc, :] = o
        if tail == "next":
            hn_ref[r0:r0 + rc, :] = _adaln(o, ng_ref[...], nm_ref[0]).astype(hn_ref.dtype)


def _ffn(x, m, g, w1, w3, w2, layer, which, seqlen, final_g=None, next_ln=None):
    t, d = x.shape
    f = w1.shape[-1]
    tm, tf = min(1024, seqlen), 256
    assert f % (2 * tf) == 0
    per = seqlen // tm
    tail = "final" if final_g is not None else "next" if next_ln is not None else "plain"
    vec = pl.BlockSpec((1, d), lambda i: (0, 0))
    mod = pl.BlockSpec((1, 3, d), lambda i: (i // per, 0, 0))
    tok = pl.BlockSpec((tm, d), lambda i: (i, 0))
    hbm = pl.BlockSpec(memory_space=pl.ANY)
    x_spec = pl.BlockSpec((tm, d), lambda i: (i, 0), pipeline_mode=pl.Buffered(1)) if tail == "next" else tok
    in_specs = [x_spec, mod, vec, hbm, hbm, hbm]
    args = [x, m, g.reshape(1, d), w1, w3, w2]
    out_specs, out_shape = tok, jax.ShapeDtypeStruct((t, d), F32)
    if tail == "final":
        in_specs.append(vec)
        args.append(final_g.reshape(1, d))
    elif tail == "next":
        in_specs += [vec, mod]
        args += [next_ln[0].reshape(1, d), next_ln[1]]
        out_specs, out_shape = [tok, tok], [out_shape, jax.ShapeDtypeStruct((t, d), BF16)]
    return pl.pallas_call(
        functools.partial(_ffn_kernel, tail=tail, layer=layer, which=which, tf=tf),
        grid=(t // tm,),
        in_specs=in_specs,
        out_specs=out_specs,
        out_shape=out_shape,
        scratch_shapes=[pltpu.VMEM((tm, d), BF16),
                        pltpu.VMEM((2, d, tf), F32), pltpu.VMEM((2, d, tf), F32), pltpu.VMEM((2, tf, d), F32),
                        pltpu.SemaphoreType.DMA((3, FFN_DMA_SPLIT, 2))],
        compiler_params=_params(("arbitrary",), FFN_VMEM_LIMIT),
        name="ffn",
    )(*args)


def _matmul_kernel(a_ref, w_ref, o_ref, w_scr, *, valid, out_major):
    @pl.when(pl.program_id(1) == 0)
    def _():
        w = w_ref[...]
        n_axis = 0 if out_major else 1
        if valid < w.shape[n_axis]:
            w = jnp.where(lax.broadcasted_iota(jnp.int32, w.shape, n_axis) < valid, w, 0.0)
        w_scr[...] = w.astype(BF16)

    dot = _dot_nt if out_major else _dot
    o_ref[...] = dot(a_ref[...], w_scr[...]).astype(o_ref.dtype)


def _matmul(a, w, lead, col0=0, n=None, out_dtype=F32, out_major=False):
    m, k = a.shape
    n_total = w.shape[1] if out_major else w.shape[2]
    n = n_total - col0 if n is None else n
    n_pad = -(-n // LANES) * LANES
    tm = min(1024, m)
    tn = min(1024, n_pad)
    assert col0 % tn == 0 and n_pad % tn == 0
    cb = col0 // tn
    if out_major:
        w_spec = pl.BlockSpec((None, tn, k), lambda j, i: (lead, cb + j, 0))
        w_tile = (tn, k)
    else:
        w_spec = pl.BlockSpec((None, k, tn), lambda j, i: (lead, 0, cb + j))
        w_tile = (k, tn)
    return pl.pallas_call(
        functools.partial(_matmul_kernel, valid=min(tn, n), out_major=out_major),
        grid=(n_pad // tn, m // tm),
        in_specs=[pl.BlockSpec((tm, k), lambda j, i: (i, 0)), w_spec],
        out_specs=pl.BlockSpec((tm, tn), lambda j, i: (i, j)),
        out_shape=jax.ShapeDtypeStruct((m, n_pad), out_dtype),
        scratch_shapes=[pltpu.VMEM(w_tile, BF16)],
        compiler_params=_params(("parallel", "arbitrary")),
        name="matmul",
    )(a, w)


def _matmul_resid_kernel(*refs, n_in):
    a_refs = refs[:n_in]
    w_refs = refs[n_in:2 * n_in]
    x_ref, m_ref, o_ref = refs[2 * n_in:2 * n_in + 3]
    w_scrs = refs[2 * n_in + 3:]

    @pl.when(pl.program_id(1) == 0)
    def _():
        for w_ref, w_scr in zip(w_refs, w_scrs):
            w_scr[...] = w_ref[...].astype(BF16)

    acc = _dot(a_refs[0][...], w_scrs[0][...])
    for a_ref, w_scr in zip(a_refs[1:], w_scrs[1:]):
        acc = acc + _dot(a_ref[...], w_scr[...])
    o_ref[...] = x_ref[...] + m_ref[0][2:3] * acc


def _matmul_resid(a_list, w, x, m, seqlen):
    t, d = x.shape
    tm, tn = min(512, seqlen), 1024
    per = seqlen // tm
    n_in = len(a_list)
    kb = w.shape[0] // n_in
    arrays, in_specs = [], []
    for a in a_list:
        arr, blk = a if isinstance(a, tuple) else (a, 0)
        arrays.append(arr)
        in_specs.append(pl.BlockSpec((tm, kb), lambda j, i, blk=blk: (i, blk)))
    in_specs += [pl.BlockSpec((kb, tn), lambda j, i, r=r: (r, j)) for r in range(n_in)]
    in_specs += [pl.BlockSpec((tm, tn), lambda j, i: (i, j)),
                 pl.BlockSpec((1, 3, tn), lambda j, i: (i // per, 0, j))]
    return pl.pallas_call(
        functools.partial(_matmul_resid_kernel, n_in=n_in),
        grid=(d // tn, t // tm),
        in_specs=in_specs,
        out_specs=pl.BlockSpec((tm, tn), lambda j, i: (i, j)),
        out_shape=jax.ShapeDtypeStruct((t, d), F32),
        scratch_shapes=[pltpu.VMEM((kb, tn), BF16)] * n_in,
        compiler_params=_params(("parallel", "arbitrary")),
        name="matmul_resid",
    )(*arrays, *([w] * n_in), x, m)


def _s5_kernel(u_ref, lrg_ref, lig_ref, ldg_ref, lrl_ref, lil_ref, ldl_ref, btr_ref, bti_ref, cr_ref, ci_ref,
               o_ref, *, nc, levels):
    q, gc, p = S5_CHUNK, S5_GROUP_CH, S5_STATE
    ng = LANES // gc
    gp = ng * p
    rows = u_ref.shape[0] // q
    half = q // 2

    def discretize(lre, lim, ldt):
        lr = jnp.minimum(lre, -1e-4)
        dt = jnp.exp(ldt)
        mag = jnp.exp(lr * dt)
        lb_re, lb_im = mag * jnp.cos(lim * dt), mag * jnp.sin(lim * dt)
        den = lr * lr + lim * lim
        nr, ni = lb_re - 1.0, lb_im
        return lb_re, lb_im, (nr * lr + ni * lim) / den, (ni * lr - nr * lim) / den

    def powers(lb_re, lb_im, count):
        out = [(jnp.ones_like(lb_re), jnp.zeros_like(lb_im))]
        for _ in range(count - 1):
            pr, pi = out[-1]
            out.append((pr * lb_re - pi * lb_im, pr * lb_im + pi * lb_re))
        return out

    def split3(x):
        p1 = x.astype(BF16)
        r1 = x - p1.astype(F32)
        p2 = r1.astype(BF16)
        return p1, p2, (r1 - p2.astype(F32)).astype(BF16)

    ri = lax.broadcasted_iota(jnp.int32, (LANES, ng), 0)
    ci_ = lax.broadcasted_iota(jnp.int32, (LANES, ng), 1)
    to_rows = (ri // gc == ci_).astype(F32)
    expand = lambda a: _dot(to_rows, a, hi=True)
    pi_ = lax.broadcasted_iota(jnp.int32, (3 * p, gp), 0)
    pj_ = lax.broadcasted_iota(jnp.int32, (3 * p, gp), 1)
    to_lanes = (pi_ % p == pj_ % p).astype(BF16)
    tile = lambda a: jnp.dot(jnp.concatenate(split3(a), axis=1), to_lanes, preferred_element_type=F32)
    same_rc = (lax.broadcasted_iota(jnp.int32, (LANES, LANES), 0) // gc
               == lax.broadcasted_iota(jnp.int32, (LANES, LANES), 1) // gc)
    same_rl = (lax.broadcasted_iota(jnp.int32, (LANES, gp), 0) // gc
               == lax.broadcasted_iota(jnp.int32, (LANES, gp), 1) // p)

    lb_re_r, lb_im_r, f_re_r, f_im_r = [expand(a) for a in discretize(lrg_ref[...], lig_ref[...], ldg_ref[...])]
    lb_re_l, lb_im_l, f_re_l, f_im_l = discretize(lrl_ref[0], lil_ref[0], ldl_ref[0])
    pow_r = powers(lb_re_r, lb_im_r, q)
    pow_l = powers(lb_re_l, lb_im_l, q + 1)

    btr, bti = btr_ref[...], bti_ref[...]
    cr, ci = cr_ref[...], ci_ref[...]
    bb_re = f_re_r * btr - f_im_r * bti
    bb_im = f_re_r * bti + f_im_r * btr

    b_hi, b_lo, _ = split3(jnp.concatenate([bb_re, -bb_im], axis=1))
    b_cat = jnp.concatenate([b_hi, b_lo, b_hi], axis=1)
    w = []
    for pr, pi in pow_r:
        c_hi, c_lo, _ = split3(jnp.concatenate([cr * pr - ci * pi, cr * pi + ci * pr], axis=1))
        k_t = _dot_nt(b_cat, jnp.concatenate([c_hi, c_hi, c_lo], axis=1))
        w.append(jnp.where(same_rc, k_t, 0.0).astype(BF16))
    zero = jnp.zeros((LANES, LANES), BF16)

    def w_pair(delta):
        lo = w[2 * delta - 1] if delta > 0 else zero
        return jnp.concatenate([jnp.concatenate([w[2 * delta], w[2 * delta + 1]], axis=1),
                                jnp.concatenate([lo, w[2 * delta]], axis=1)], axis=0)

    u_pair = []
    for jp in range(half):
        u_pair.append(jnp.concatenate(
            [u_ref[pl.ds(2 * jp + jj, rows, stride=q), :].astype(BF16) for jj in range(2)], axis=1))

    bt_re, bt_im = tile(btr), tile(bti)
    bbl_re = jnp.where(same_rl, f_re_l * bt_re - f_im_l * bt_im, 0.0)
    bbl_im = jnp.where(same_rl, f_re_l * bt_im + f_im_l * bt_re, 0.0)
    st = None
    for jp in range(half):
        blocks = []
        for j in (2 * jp, 2 * jp + 1):
            pr, pi = pow_l[q - 1 - j]
            blocks.append(jnp.concatenate([bbl_re * pr - bbl_im * pi, bbl_re * pi + bbl_im * pr], axis=1))
        term = _dot(u_pair[jp], jnp.concatenate(blocks, axis=0))
        st = term if st is None else st + term
    sr, si = st[:, :gp], st[:, gp:]

    cidx = lax.broadcasted_iota(jnp.int32, (rows, 1), 0) % nc
    ar, ai = pow_l[q]
    for k in range(levels):
        sh = 1 << k
        if k > 0:
            ar, ai = ar * ar - ai * ai, 2.0 * ar * ai
        xr = pltpu.roll(sr, sh, axis=0)
        xi = pltpu.roll(si, sh, axis=0)
        ok = cidx >= sh
        sr, si = (sr + jnp.where(ok, ar * xr - ai * xi, 0.0),
                  si + jnp.where(ok, ar * xi + ai * xr, 0.0))
    ok = cidx >= 1
    s_prev = jnp.concatenate([jnp.where(ok, pltpu.roll(sr, 1, axis=0), 0.0),
                              jnp.where(ok, pltpu.roll(si, 1, axis=0), 0.0)], axis=1).astype(BF16)

    ct_re, ct_im = tile(cr), tile(ci)
    ct_re = jnp.where(same_rl, ct_re, 0.0)
    ct_im = jnp.where(same_rl, ct_im, 0.0)
    for ip in range(half):
        acc = None
        for jp in range(ip + 1):
            term = _dot(u_pair[jp], w_pair(ip - jp))
            acc = term if acc is None else acc + term
        blocks = []
        for i in (2 * ip, 2 * ip + 1):
            pr, pi = pow_l[i + 1]
            blocks.append(jnp.concatenate([ct_re * pr - ct_im * pi, -(ct_re * pi + ct_im * pr)], axis=1))
        acc = acc + _dot_nt(s_prev, jnp.concatenate(blocks, axis=0))
        o_ref[pl.ds(2 * ip, rows, stride=q), :] = acc[:, :LANES]
        o_ref[pl.ds(2 * ip + 1, rows, stride=q), :] = acc[:, LANES:]


def _s5_post_kernel(y_ref, u_ref, d_ref, w_ref, b_ref, o_ref):
    y = y_ref[...] + d_ref[...] * u_ref[...]
    y = jax.nn.gelu(y)
    o_ref[...] = (y * jax.nn.sigmoid(_dot(y, w_ref[...]) + b_ref[...])).astype(o_ref.dtype)


def _s5_branch(u, bsz, seqlen, lam_re, lam_im, log_dt, b_re, b_im, c_re, c_im, d_skip, glu_w, glu_b):
    t, width = u.shape
    q, gc, p = S5_CHUNK, S5_GROUP_CH, S5_STATE
    groups = width // gc
    nc = seqlen // q
    levels = int(math.log2(nc))
    assert (1 << levels) == nc
    ng = LANES // gc
    nblk = width // LANES
    gp = ng * p
    ldt = jnp.broadcast_to(log_dt[:, None], (groups, p))
    by_group = pl.BlockSpec((ng, p), lambda k: (k, 0))
    by_lane = pl.BlockSpec((1, 1, gp), lambda k: (k, 0, 0))
    by_chan = pl.BlockSpec((LANES, p), lambda k: (k, 0))
    tok = pl.BlockSpec((t, LANES), lambda k: (0, k))
    lanes3 = lambda a: a.reshape(nblk, 1, gp)
    chan2 = lambda a: a.reshape(width, p)
    y = pl.pallas_call(
        functools.partial(_s5_kernel, nc=nc, levels=levels),
        grid=(nblk,),
        in_specs=[tok, by_group, by_group, by_group, by_lane, by_lane, by_lane,
                  by_chan, by_chan, by_chan, by_chan],
        out_specs=tok,
        out_shape=jax.ShapeDtypeStruct((t, width), F32),
        compiler_params=_params(("parallel",)),
        name="s5_conv",
    )(u, lam_re, lam_im, ldt, lanes3(lam_re), lanes3(lam_im), lanes3(ldt),
      chan2(jnp.swapaxes(b_re, 1, 2)), chan2(jnp.swapaxes(b_im, 1, 2)), chan2(c_re), chan2(c_im))

    tm = min(512, t)
    return pl.pallas_call(
        _s5_post_kernel,
        grid=(t // tm,),
        in_specs=[pl.BlockSpec((tm, width), lambda i: (i, 0)),
                  pl.BlockSpec((tm, width), lambda i: (i, 0)),
                  pl.BlockSpec((1, width), lambda i: (0, 0)),
                  pl.BlockSpec((width, width), lambda i: (0, 0)),
                  pl.BlockSpec((1, width), lambda i: (0, 0))],
        out_specs=pl.BlockSpec((tm, width), lambda i: (i, 0)),
        out_shape=jax.ShapeDtypeStruct((t, width), BF16),
        compiler_params=_params(("parallel",)),
        name="s5_post",
    )(y, u, d_skip.reshape(1, width), glu_w.astype(BF16), glu_b.reshape(1, width))


def _ssd_kernel(z_ref, xbc_ref, dt_ref, cw_ref, cb_ref, dtb_ref, alog_ref, dsk_ref, ng_ref,
                o_ref, ext_scr, st_scr, *, heads, inner):
    lc = SSD_CHUNK
    hd, ns = SSD_HEAD_DIM, SSD_STATE
    rpg = heads // SSD_GROUPS
    gw = SSD_GROUPS * ns
    assert SSD_CONV - 1 <= 8

    @pl.when(pl.program_id(1) == 0)
    def _():
        ext_scr[0:8, :] = jnp.zeros((8, ext_scr.shape[1]), F32)
        st_scr[...] = jnp.zeros_like(st_scr)

    ext_scr[8:8 + lc, :] = xbc_ref[...]
    cw = cw_ref[...]
    ext = ext_scr[0:8 + lc, :]
    acc = cw[0:1] * ext
    for k in range(1, SSD_CONV):
        acc = cw[k:k + 1] * ext + pltpu.roll(acc, 1, axis=0)
    conv = cb_ref[...] + acc[8:8 + lc]
    ext_scr[0:8, :] = ext_scr[lc:lc + 8, :]
    act = _silu(conv)
    xs = act[:, :inner]
    bs = act[:, inner:inner + gw]
    cs = act[:, inner + gw:]

    dt = _softplus(dt_ref[...] + dtb_ref[...])
    adt = dt * (-jnp.exp(alog_ref[...]))
    ri = lax.broadcasted_iota(jnp.int32, (lc, lc), 0)
    ci = lax.broadcasted_iota(jnp.int32, (lc, lc), 1)
    causal = ri >= ci
    a_cum = _dot(causal.astype(F32), adt, hi=True)
    a_cum_t = a_cum.T
    dt_t = dt.T

    y_parts = []
    for g in range(SSD_GROUPS):
        cs_g = cs[:, g * ns:(g + 1) * ns]
        bs_g = bs[:, g * ns:(g + 1) * ns]
        cb = _dot_nt(cs_g, bs_g)
        bs_gt = bs_g.T
        for r in range(rpg):
            h = g * rpg + r
            col = a_cum[:, h:h + 1]
            row = a_cum_t[h:h + 1, :]
            dt_row = dt_t[h:h + 1, :]
            decay = jnp.exp(jnp.where(causal, col - row, -jnp.inf))
            xs_h = xs[:, h * hd:(h + 1) * hd]
            st = st_scr[h]
            lhs = jnp.concatenate([cb * decay * dt_row, cs_g * jnp.exp(col)], axis=1)
            y_parts.append(_dot(lhs, jnp.concatenate([xs_h, st], axis=0)))
            last = row[:, lc - 1:lc]
            to_end = jnp.exp(last - row)
            st_scr[h] = jnp.exp(last) * st + _dot(bs_gt * (to_end * dt_row), xs_h)
    y = jnp.concatenate(y_parts, axis=1) + dsk_ref[...] * xs
    y = y * _silu(z_ref[...])
    o_ref[...] = _rms(y, ng_ref[...]).astype(o_ref.dtype)


def _ssd_branch(z, xbc, dt_raw, bsz, seqlen, conv_w, conv_b, dt_bias, a_log, d_skip, norm_g):
    t, inner = z.shape
    cd = xbc.shape[1]
    heads = inner // SSD_HEAD_DIM
    lc = SSD_CHUNK
    nc = seqlen // lc
    pad = LANES - heads
    padv = lambda v: jnp.pad(v.reshape(1, heads), ((0, 0), (0, pad)))
    row = lambda n: pl.BlockSpec((1, n), lambda b, c: (0, 0))
    tok = lambda n: pl.BlockSpec((lc, n), lambda b, c: (b * nc + c, 0))
    return pl.pallas_call(
        functools.partial(_ssd_kernel, heads=heads, inner=inner),
        grid=(bsz, nc),
        in_specs=[tok(inner), tok(cd), tok(LANES),
                  pl.BlockSpec((SSD_CONV, cd), lambda b, c: (0, 0)), row(cd),
                  row(LANES), row(LANES), row(inner), row(inner)],
        out_specs=tok(inner),
        out_shape=jax.ShapeDtypeStruct((t, inner), BF16),
        scratch_shapes=[pltpu.VMEM((8 + lc + 8, cd), F32),
                        pltpu.VMEM((heads, SSD_STATE, SSD_HEAD_DIM), F32)],
        compiler_params=_params(("parallel", "arbitrary")),
        name="ssd",
    )(z, xbc, dt_raw, conv_w, conv_b.reshape(1, cd), padv(dt_bias), padv(a_log),
      jnp.repeat(d_skip, SSD_HEAD_DIM).reshape(1, inner), norm_g.reshape(1, inner))


def _hybrid_mixer(x, h, m, seqlen, idx, w_in, w_out, lam_re, lam_im, log_dt, b_re, b_im, c_re, c_im, s5_d,
                  glu_w, glu_b, conv_w, conv_b, dt_bias, a_log, ssd_d, ssd_norm_g):
    t, d = x.shape
    bsz = t // seqlen
    s5w = s5_d.shape[0]
    inner = ssd_norm_g.shape[0]
    cd = conv_w.shape[1]
    heads = dt_bias.shape[0]
    o1, o2, o3 = s5w, s5w + inner, s5w + inner + cd
    w_t = jnp.swapaxes(w_in, 1, 2)
    u = _matmul(h, w_t, idx, 0, o1, out_major=True)
    z = _matmul(h, w_t, idx, o1, inner, out_major=True)
    xbc = _matmul(h, w_t, idx, o2, cd, out_major=True)
    dt_raw = _matmul(h, w_t, idx, o3, heads, out_major=True)
    y_s5 = _s5_branch(u, bsz, seqlen, lam_re, lam_im, log_dt, b_re, b_im, c_re, c_im, s5_d, glu_w, glu_b)
    y_ssd = _ssd_branch(z, xbc, dt_raw, bsz, seqlen, conv_w, conv_b, dt_bias, a_log, ssd_d, ssd_norm_g)
    assert inner % s5w == 0
    return _matmul_resid([y_s5] + [(y_ssd, i) for i in range(inner // s5w)], w_out, x, m, seqlen)


def _rwkv_mix_kernel(x_ref, m_ref, g_ref, mu_ref, w1_ref, w2_ref, w0_ref, a1_ref, a2_ref, a0_ref, g1_ref, g2_ref,
                     xr_ref, xk_ref, xv_ref, lw_ref, a_ref, gate_ref, prev_scr):
    h = _adaln(x_ref[...], g_ref[...], m_ref[0])
    tm = h.shape[0]

    @pl.when(pl.program_id(1) == 0)
    def _():
        prev_scr[...] = jnp.zeros_like(prev_scr)

    first = lax.broadcasted_iota(jnp.int32, (tm, 1), 0) == 0
    shifted = jnp.where(first, prev_scr[7:8, :], pltpu.roll(h, 1, axis=0))
    prev_scr[...] = h[tm - 8:tm]
    xx = shifted - h
    mu = mu_ref[...]
    mix = lambda i: h + xx * mu[i:i + 1]
    xr_ref[...] = mix(0).astype(xr_ref.dtype)
    xk_ref[...] = mix(2).astype(xk_ref.dtype)
    xv_ref[...] = mix(3).astype(xv_ref.dtype)
    w = -_softplus(-(w0_ref[...] + _dot(jnp.tanh(_dot(mix(1), w1_ref[...])), w2_ref[...]))) - 0.5
    lw_ref[...] = -jnp.exp(w)
    a_ref[...] = jax.nn.sigmoid(a0_ref[...] + _dot(_dot(mix(4), a1_ref[...]), a2_ref[...]))
    gate_ref[...] = _dot(jax.nn.sigmoid(_dot(mix(5), g1_ref[...])), g2_ref[...])


def _rwkv_mix(x, m, g, mu, w1, w2, w0, a1, a2, a0, g1, g2, bsz, seqlen):
    t, d = x.shape
    tm = min(256, seqlen)
    per = seqlen // tm
    tok = pl.BlockSpec((tm, d), lambda b, i: (b * per + i, 0))
    vec = pl.BlockSpec((1, d), lambda b, i: (0, 0))
    whole = lambda a: pl.BlockSpec(a.shape, lambda b, i: (0, 0))

    def lora(a, b):
        rank = a.shape[1]
        rp = -(-rank // LANES) * LANES
        return (jnp.pad(a, ((0, 0), (0, rp - rank))).astype(BF16),
                jnp.pad(b, ((0, rp - rank), (0, 0))).astype(BF16))

    (w1p, w2p), (a1p, a2p), (g1p, g2p) = lora(w1, w2), lora(a1, a2), lora(g1, g2)
    return pl.pallas_call(
        _rwkv_mix_kernel,
        grid=(bsz, per),
        in_specs=[tok, pl.BlockSpec((1, 3, d), lambda b, i: (b, 0, 0)), vec,
                  pl.BlockSpec((8, d), lambda b, i: (0, 0)),
                  whole(w1p), whole(w2p), vec, whole(a1p), whole(a2p), vec, whole(g1p), whole(g2p)],
        out_specs=[tok] * 6,
        out_shape=[jax.ShapeDtypeStruct((t, d), BF16)] * 3 + [jax.ShapeDtypeStruct((t, d), F32)] * 3,
        scratch_shapes=[pltpu.VMEM((8, d), F32)],
        compiler_params=_params(("parallel", "arbitrary")),
        name="rwkv_mix",
    )(x, m, g.reshape(1, d), jnp.pad(mu, ((0, 2), (0, 0))),
      w1p, w2p, w0.reshape(1, d), a1p, a2p, a0.reshape(1, d), g1p, g2p)


def _rwkv_scan_kernel(r_ref, k_ref, v_ref, lw_ref, a_ref, g_ref, kk_ref, ka_ref, rk_ref, lg_ref, lb_ref,
                      o_ref, s_scr):
    c = RWKV_CHUNK
    hd = RWKV_HEAD_DIM
    c2 = 2 * c
    tokens = r_ref.shape[0]
    pairs = r_ref.shape[1] // LANES
    tb = pairs * tokens
    per_pair = tokens // c
    n_chunks = tb // c

    @pl.when(pl.program_id(2) == 0)
    def _():
        s_scr[...] = jnp.zeros_like(s_scr)

    def rows_of(ref):
        return jnp.concatenate([ref[:, p * LANES:(p + 1) * LANES] for p in range(pairs)], axis=0)

    def param_rows(ref):
        return jnp.concatenate([jnp.broadcast_to(ref[:, p * LANES:(p + 1) * LANES], (tokens, LANES))
                                for p in range(pairs)], axis=0)

    lane_head = lax.broadcasted_iota(jnp.int32, (1, LANES), 1) // hd
    head0 = lane_head == 0
    same_head = ((lax.broadcasted_iota(jnp.int32, (LANES, LANES), 0) // hd)
                 == (lax.broadcasted_iota(jnp.int32, (LANES, LANES), 1) // hd))
    r2 = lax.broadcasted_iota(jnp.int32, (1, c2, c2), 1)
    q2 = lax.broadcasted_iota(jnp.int32, (1, c2, c2), 2)
    same_blk = (r2 // c) == (q2 // c)
    strict = same_blk & ((r2 % c) > (q2 % c))
    incl = same_blk & ((r2 % c) >= (q2 % c))
    eye2 = (r2 == q2).astype(F32)
    k_k, k_a, r_k = param_rows(kk_ref), param_rows(ka_ref), param_rows(rk_ref)

    def stack(x):
        x = x.reshape(n_chunks, c, LANES)
        return jnp.concatenate([jnp.where(head0, x, 0.0), jnp.where(head0, 0.0, x)], axis=1)

    def unstack(x):
        return (x[:, :c] + x[:, c:]).reshape(tb, LANES)

    def bdot(p, q, dims):
        return lax.dot_general(p.astype(BF16), q.astype(BF16), (dims, ((0,), (0,))), preferred_element_type=F32)

    bmm = lambda p, q: bdot(p, q, ((2,), (1,)))
    bmm_nt = lambda p, q: bdot(p, q, ((2,), (2,)))
    bmm_tn = lambda p, q: bdot(p, q, ((1,), (1,)))

    def split(x):
        hi_part = x.astype(BF16)
        return hi_part, (x - hi_part.astype(F32)).astype(BF16)

    def head_sum(x):
        s0 = jnp.sum(jnp.where(head0, x, 0.0), axis=1, keepdims=True)
        s1 = jnp.sum(jnp.where(head0, 0.0, x), axis=1, keepdims=True)
        return jnp.where(head0, s0, s1)

    def cumsum_rows(x):
        row_in_chunk = lax.broadcasted_iota(jnp.int32, (tb, 1), 0) % c
        shift = 1
        while shift < c:
            x = x + jnp.where(row_in_chunk >= shift, pltpu.roll(x, shift, axis=0), 0.0)
            shift *= 2
        return x

    def lhs3(hi_part, lo_part):
        return jnp.concatenate([hi_part, lo_part], axis=2)

    def rhs3(hi_part, lo_part):
        top = jnp.concatenate([hi_part, lo_part], axis=2)
        bot = jnp.concatenate([hi_part, jnp.zeros_like(lo_part)], axis=2)
        return jnp.concatenate([top, bot], axis=1)

    def fold(x):
        return x[:, :, :c2] + x[:, :, c2:]

    r, k, v, lw, a = rows_of(r_ref), rows_of(k_ref), rows_of(v_ref), rows_of(lw_ref), rows_of(a_ref)
    kk = k * k_k
    k2 = k * (1.0 + (a - 1.0) * k_a)
    rk_sum = head_sum(r * k2 * r_k)
    kk = kk / jnp.maximum(jnp.sqrt(head_sum(kk * kk)), 1e-12)
    bv = kk * a
    cum = cumsum_rows(lw)
    cum3 = cum.reshape(n_chunks, c, LANES)
    tot = jnp.broadcast_to(cum3[:, c - 1:c, :], cum3.shape).reshape(tb, LANES)
    g_inv = jnp.exp(-cum)
    rt_u = r * jnp.exp(cum)
    at = stack(-kk * jnp.exp(cum - lw))
    rt = stack(rt_u)
    bt = stack(bv * g_inv)
    kt = stack(k2 * g_inv)
    to_end = jnp.exp(tot - cum)
    b_end = (bv * to_end).reshape(n_chunks, c, LANES)
    k_end = (k2 * to_end).reshape(n_chunks, c, LANES)
    vs = stack(v)

    gram = bmm_nt(jnp.concatenate([at, rt], axis=1), jnp.concatenate([bt, kt], axis=1))
    n_ab = jnp.where(strict, gram[:, :c2, :c2], 0.0)
    a_ak = jnp.where(strict, gram[:, :c2, c2:], 0.0)
    a_rb = jnp.where(incl, gram[:, c2:, :c2], 0.0)
    a_rk = jnp.where(incl, gram[:, c2:, c2:], 0.0)
    pw_h, pw_l = split(n_ab)
    pw = fold(bmm(lhs3(pw_h, pw_l), rhs3(pw_h, pw_l)))
    tinv = eye2 + n_ab
    for _ in range(int(math.log2(c)) - 2):
        pw_h, pw_l = split(pw)
        t_h, t_l = split(tinv)
        both = fold(bmm(jnp.concatenate([lhs3(pw_h, pw_l), lhs3(t_h, t_l)], axis=1), rhs3(pw_h, pw_l)))
        pw = both[:, :c2]
        tinv = tinv + both[:, c2:]
    pw_h, pw_l = split(pw)
    t_h, t_l = split(tinv)
    tinv = tinv + fold(bmm(lhs3(t_h, t_l), rhs3(pw_h, pw_l)))
    pq = bmm(tinv, jnp.concatenate([at, bmm(a_ak, vs)], axis=2))
    ry = bmm(jnp.concatenate([a_rb, a_rk], axis=2),
             jnp.concatenate([pq, jnp.concatenate([jnp.zeros_like(vs), vs], axis=2)], axis=1))
    rq = rt_u + unstack(ry[:, :, :LANES])
    y0 = unstack(ry[:, :, LANES:])
    p_u = unstack(pq[:, :, :LANES]).reshape(n_chunks, c, LANES)
    q_u = unstack(pq[:, :, LANES:]).reshape(n_chunks, c, LANES)
    gm = jnp.where(same_head, bmm_tn(p_u, b_end), 0.0)
    dm = jnp.where(same_head, bmm_tn(jnp.concatenate([q_u, v.reshape(n_chunks, c, LANES)], axis=1),
                                     jnp.concatenate([b_end, k_end], axis=1)), 0.0)
    g_tot = jnp.exp(cum3[:, c - 1:c, :])
    bonus = rk_sum * v

    s = [s_scr[p] for p in range(pairs)]
    ys = [[] for _ in range(pairs)]
    for i in range(per_pair):
        for p in range(pairs):
            n = p * per_pair + i
            ys[p].append(_dot_nt(rq[n * c:(n + 1) * c], s[p]))
            s[p] = s[p] * g_tot[n] + _dot(s[p], gm[n]) + dm[n]
    for p in range(pairs):
        s_scr[p] = s[p]
    y = jnp.concatenate([y_c for y_p in ys for y_c in y_p], axis=0) + y0
    mean = head_sum(y) * (1.0 / hd)
    dy = y - mean
    var = head_sum(dy * dy) * (1.0 / hd)
    yn = dy * lax.rsqrt(var + RWKV_GN_EPS) * param_rows(lg_ref) + param_rows(lb_ref)
    out = ((yn + bonus) * rows_of(g_ref)).astype(o_ref.dtype)
    for p in range(pairs):
        o_ref[:, p * LANES:(p + 1) * LANES] = out[p * tokens:(p + 1) * tokens]


def _rwkv_scan(r, k, v, lw, a, g, k_k, k_a, r_k, ln_g, ln_b, bsz, seqlen):
    t, d = r.shape
    tb = min(RWKV_TOKEN_BLOCK, seqlen)
    per = seqlen // tb
    pairs = RWKV_PAIRS_PER_BLOCK
    bw = pairs * LANES
    tok = pl.BlockSpec((tb, bw), lambda b, h, i: (b * per + i, h))
    row = pl.BlockSpec((1, bw), lambda b, h, i: (0, h))
    vec = lambda p: p.reshape(1, d)
    return pl.pallas_call(
        _rwkv_scan_kernel,
        grid=(bsz, d // bw, per),
        in_specs=[tok] * 6 + [row] * 5,
        out_specs=tok,
        out_shape=jax.ShapeDtypeStruct((t, d), BF16),
        scratch_shapes=[pltpu.VMEM((pairs, LANES, LANES), F32)],
        compiler_params=_params(("parallel", "parallel", "arbitrary")),
        name="rwkv_scan",
    )(r, k, v, lw, a, g, vec(k_k), vec(k_a), vec(r_k), vec(ln_g), vec(ln_b))


def _rwkv_mixer(x, m, g, seqlen, idx, mu, w_r, w_k, w_v, w_o, w0, w1, w2, a0, a1, a2, g1, g2,
                k_k, k_a, r_k, ln_g, ln_b):
    t, d = x.shape
    bsz = t // seqlen
    xr, xk, xv, lw, a, gate = _rwkv_mix(x, m, g, mu, w1, w2, w0, a1, a2, a0, g1, g2, bsz, seqlen)
    r = _matmul(xr, w_r, idx)
    k = _matmul(xk, w_k, idx)
    v = _matmul(xv, w_v, idx)
    y = _rwkv_scan(r, k, v, lw, a, gate, k_k, k_a, r_k.reshape(-1), ln_g, ln_b, bsz, seqlen)
    return _matmul_resid([y], w_o, x, m, seqlen)


def kernel(x, c, w_mod, b_mod, norm_g, ffn_w1, ffn_w3, ffn_w2, hyb_w_in, hyb_w_out, s5_lambda_re, s5_lambda_im, s5_log_dt, s5_b_re, s5_b_im, s5_c_re, s5_c_im, s5_d, s5_glu_w, s5_glu_b, ssd_conv_w, ssd_conv_b, ssd_dt_bias, ssd_a_log, ssd_d, ssd_norm_g, rwkv_mu, rwkv_w_r, rwkv_w_k, rwkv_w_v, rwkv_w_o, rwkv_w0, rwkv_w1, rwkv_w2, rwkv_a0, rwkv_a1, rwkv_a2, rwkv_g1, rwkv_g2, rwkv_k_k, rwkv_k_a, rwkv_r_k, rwkv_ln_g, rwkv_ln_b, final_g):
    bsz, seqlen, d = x.shape
    depth = w_mod.shape[0]
    xf = x.reshape(bsz * seqlen, d)
    mod = _modulation(c, w_mod, b_mod).reshape(depth, bsz, 3, 3, d)
    for layer in range(depth):
        i = layer // 2
        m0, m1, m2 = mod[layer, :, 0], mod[layer, :, 1], mod[layer, :, 2]
        if layer % 2 == 0:
            xf, h = _ffn(xf, m0, norm_g[layer, 0], ffn_w1, ffn_w3, ffn_w2, layer, 0, seqlen,
                         next_ln=(norm_g[layer, 1], m1))
            xf = _hybrid_mixer(xf, h, m1, seqlen, i, hyb_w_in, hyb_w_out[i], s5_lambda_re[i],
                               s5_lambda_im[i], s5_log_dt[i], s5_b_re[i], s5_b_im[i], s5_c_re[i], s5_c_im[i],
                               s5_d[i], s5_glu_w[i], s5_glu_b[i], ssd_conv_w[i], ssd_conv_b[i], ssd_dt_bias[i],
                               ssd_a_log[i], ssd_d[i], ssd_norm_g[i])
        else:
            xf = _ffn(xf, m0, norm_g[layer, 0], ffn_w1, ffn_w3, ffn_w2, layer, 0, seqlen)
            xf = _rwkv_mixer(xf, m1, norm_g[layer, 1], seqlen, i, rwkv_mu[i], rwkv_w_r, rwkv_w_k, rwkv_w_v,
                             rwkv_w_o[i], rwkv_w0[i], rwkv_w1[i], rwkv_w2[i], rwkv_a0[i], rwkv_a1[i], rwkv_a2[i],
                             rwkv_g1[i], rwkv_g2[i], rwkv_k_k[i], rwkv_k_a[i], rwkv_r_k[i], rwkv_ln_g[i],
                             rwkv_ln_b[i])
        xf = _ffn(xf, m2, norm_g[layer, 2], ffn_w1, ffn_w3, ffn_w2, layer, 1, seqlen,
                  final_g=final_g if layer == depth - 1 else None)
    return xf.reshape(bsz, seqlen, d)
```

```python
import functools
import math

import jax
import jax.numpy as jnp
from jax import lax
from jax.experimental import pallas as pl
from jax.experimental.pallas import tpu as pltpu

F32 = jnp.float32
BF16 = jnp.bfloat16
HI = lax.Precision.HIGHEST

RMS_EPS = 1e-6
RWKV_GN_EPS = 64e-5

S5_GROUP_CH = 16
S5_STATE = 64
S5_CHUNK = 16
SSD_HEAD_DIM = 64
SSD_STATE = 128
SSD_GROUPS = 8
SSD_CHUNK = 128
SSD_CONV = 4
RWKV_HEAD_DIM = 64
RWKV_CHUNK = 64
RWKV_TOKEN_BLOCK = 256
RWKV_PAIRS_PER_BLOCK = 4
LANES = 128
VMEM_LIMIT = 56 * 1024 * 1024
FFN_VMEM_LIMIT = 58 * 1024 * 1024
FFN_DMA_SPLIT = 4

def _params(sem, vmem=VMEM_LIMIT):
    return pltpu.CompilerParams(dimension_semantics=sem, vmem_limit_bytes=vmem)


def _dot(a, b, hi=False):
    dn = (((1,), (0,)), ((), ()))
    if hi:
        return lax.dot_general(a, b, dn, precision=HI, preferred_element_type=F32)
    return lax.dot_general(a.astype(BF16), b.astype(BF16), dn, preferred_element_type=F32)


def _dot_nt(a, b, hi=False):
    dn = (((1,), (1,)), ((), ()))
    if hi:
        return lax.dot_general(a, b, dn, precision=HI, preferred_element_type=F32)
    return lax.dot_general(a.astype(BF16), b.astype(BF16), dn, preferred_element_type=F32)


def _dot_tn(a, b):
    dn = (((0,), (0,)), ((), ()))
    return lax.dot_general(a.astype(BF16), b.astype(BF16), dn, preferred_element_type=F32)


def _silu(x):
    return x * jax.nn.sigmoid(x)


def _softplus(x):
    return jnp.maximum(x, 0.0) + jnp.log1p(jnp.exp(-jnp.abs(x)))


def _rms(x, g):
    return x * lax.rsqrt(jnp.mean(x * x, axis=-1, keepdims=True) + RMS_EPS) * g


def _adaln(x, g, m):
    return _rms(x, g) * (1.0 + m[1:2]) + m[0:1]


def _mod_kernel(c_ref, w_ref, b_ref, o_ref):
    c = c_ref[...]
    o_ref[0] = _dot(_silu(c), w_ref[0]) + b_ref[0]


def _modulation(c, w_mod, b_mod):
    depth, d, n = w_mod.shape
    bsz = c.shape[0]
    rows = 8
    cp = jnp.pad(c, ((0, rows - bsz), (0, 0)))
    tn = 1024
    out = pl.pallas_call(
        _mod_kernel,
        grid=(depth, n // tn),
        in_specs=[pl.BlockSpec((rows, d), lambda l, j: (0, 0)),
                  pl.BlockSpec((1, d, tn), lambda l, j: (l, 0, j)),
                  pl.BlockSpec((1, 1, tn), lambda l, j: (l, 0, j))],
        out_specs=pl.BlockSpec((1, rows, tn), lambda l, j: (l, 0, j)),
        out_shape=jax.ShapeDtypeStruct((depth, rows, n), F32),
        compiler_params=_params(("arbitrary", "arbitrary")),
        name="modulation",
    )(cp, w_mod, b_mod.reshape(depth, 1, n))
    return out[:, :bsz]


def _ffn_kernel(x_ref, m_ref, g_ref, w1_hbm, w3_hbm, w2_hbm, *rest, tail, layer, which, tf):
    if tail == "final":
        fg_ref, o_ref, h_scr, w1_buf, w3_buf, w2_buf, sem = rest
    elif tail == "next":
        ng_ref, nm_ref, o_ref, hn_ref, h_scr, w1_buf, w3_buf, w2_buf, sem = rest
    else:
        o_ref, h_scr, w1_buf, w3_buf, w2_buf, sem = rest
    tm = x_ref.shape[0]
    rc = min(256, tm)
    n_tiles = w1_hbm.shape[-1] // tf

    def tile_copies(j, slot):
        cols = pl.multiple_of(j * tf, tf)
        d = w1_hbm.shape[-2]
        rk, rf = d // FFN_DMA_SPLIT, tf // FFN_DMA_SPLIT
        out = []
        for s in range(FFN_DMA_SPLIT):
            rows = pl.ds(s * rk, rk)
            out.append(pltpu.make_async_copy(w1_hbm.at[layer, which, rows, pl.ds(cols, tf)],
                                             w1_buf.at[slot, rows, :], sem.at[0, s, slot]))
            out.append(pltpu.make_async_copy(w3_hbm.at[layer, which, rows, pl.ds(cols, tf)],
                                             w3_buf.at[slot, rows, :], sem.at[1, s, slot]))
            out.append(pltpu.make_async_copy(w2_hbm.at[layer, which, pl.ds(cols + s * rf, rf), :],
                                             w2_buf.at[slot, pl.ds(s * rf, rf), :], sem.at[2, s, slot]))
        return out

    for cp in tile_copies(0, 0):
        cp.start()
    for r0 in range(0, tm, rc):
        h_scr[r0:r0 + rc, :] = _adaln(x_ref[r0:r0 + rc, :], g_ref[...], m_ref[0]).astype(BF16)
    o_ref[...] = jnp.zeros_like(o_ref)

    def f_tile(j, slot):
        for cp in tile_copies(j, slot):
            cp.wait()

        @pl.when(j + 1 < n_tiles)
        def _():
            for cp in tile_copies(j + 1, 1 - slot):
                cp.start()

        h = h_scr[...]
        a = _dot(h, w1_buf[slot])
        b = _dot(h, w3_buf[slot])
        o_ref[...] += _dot(_silu(a) * b, w2_buf[slot])

    def two_tiles(jj, carry):
        f_tile(2 * jj, 0)
        f_tile(2 * jj + 1, 1)
        return carry

    lax.fori_loop(0, n_tiles // 2, two_tiles, 0)

    for r0 in range(0, tm, rc):
        o = x_ref[r0:r0 + rc, :] + (0.5 * m_ref[0][2:3]) * o_ref[r0:r0 + rc, :]
        if tail == "final":
            o = _rms(o, fg_ref[...])
        o_ref[r0:r0 + rc, :] = o
        if tail == "next":
            hn_ref[r0:r0 + rc, :] = _adaln(o, ng_ref[...], nm_ref[0]).astype(hn_ref.dtype)


def _ffn(x, m, g, w1, w3, w2, layer, which, seqlen, final_g=None, next_ln=None):
    t, d = x.shape
    f = w1.shape[-1]
    tm, tf = min(1024, seqlen), 256
    assert f % (2 * tf) == 0
    per = seqlen // tm
    tail = "final" if final_g is not None else "next" if next_ln is not None else "plain"
    vec = pl.BlockSpec((1, d), lambda i: (0, 0))
    mod = pl.BlockSpec((1, 3, d), lambda i: (i // per, 0, 0))
    tok = pl.BlockSpec((tm, d), lambda i: (i, 0))
    hbm = pl.BlockSpec(memory_space=pl.ANY)
    x_spec = pl.BlockSpec((tm, d), lambda i: (i, 0), pipeline_mode=pl.Buffered(1)) if tail == "next" else tok
    in_specs = [x_spec, mod, vec, hbm, hbm, hbm]
    args = [x, m, g.reshape(1, d), w1, w3, w2]
    out_specs, out_shape = tok, jax.ShapeDtypeStruct((t, d), F32)
    if tail == "final":
        in_specs.append(vec)
        args.append(final_g.reshape(1, d))
    elif tail == "next":
        in_specs += [vec, mod]
        args += [next_ln[0].reshape(1, d), next_ln[1]]
        out_specs, out_shape = [tok, tok], [out_shape, jax.ShapeDtypeStruct((t, d), BF16)]
    return pl.pallas_call(
        functools.partial(_ffn_kernel, tail=tail, layer=layer, which=which, tf=tf),
        grid=(t // tm,),
        in_specs=in_specs,
        out_specs=out_specs,
        out_shape=out_shape,
        scratch_shapes=[pltpu.VMEM((tm, d), BF16),
                        pltpu.VMEM((2, d, tf), F32), pltpu.VMEM((2, d, tf), F32), pltpu.VMEM((2, tf, d), F32),
                        pltpu.SemaphoreType.DMA((3, FFN_DMA_SPLIT, 2))],
        compiler_params=_params(("arbitrary",), FFN_VMEM_LIMIT),
        name="ffn",
    )(*args)


def _matmul_kernel(a_ref, w_ref, o_ref, w_scr, *, valid, out_major):
    @pl.when(pl.program_id(1) == 0)
    def _():
        w = w_ref[...]
        n_axis = 0 if out_major else 1
        if valid < w.shape[n_axis]:
            w = jnp.where(lax.broadcasted_iota(jnp.int32, w.shape, n_axis) < valid, w, 0.0)
        w_scr[...] = w.astype(BF16)

    dot = _dot_nt if out_major else _dot
    o_ref[...] = dot(a_ref[...], w_scr[...]).astype(o_ref.dtype)


def _matmul(a, w, lead, col0=0, n=None, out_dtype=F32, out_major=False):
    m, k = a.shape
    n_total = w.shape[1] if out_major else w.shape[2]
    n = n_total - col0 if n is None else n
    n_pad = -(-n // LANES) * LANES
    tm = min(1024, m)
    tn = min(1024, n_pad)
    assert col0 % tn == 0 and n_pad % tn == 0
    cb = col0 // tn
    if out_major:
        w_spec = pl.BlockSpec((None, tn, k), lambda j, i: (lead, cb + j, 0))
        w_tile = (tn, k)
    else:
        w_spec = pl.BlockSpec((None, k, tn), lambda j, i: (lead, 0, cb + j))
        w_tile = (k, tn)
    return pl.pallas_call(
        functools.partial(_matmul_kernel, valid=min(tn, n), out_major=out_major),
        grid=(n_pad // tn, m // tm),
        in_specs=[pl.BlockSpec((tm, k), lambda j, i: (i, 0)), w_spec],
        out_specs=pl.BlockSpec((tm, tn), lambda j, i: (i, j)),
        out_shape=jax.ShapeDtypeStruct((m, n_pad), out_dtype),
        scratch_shapes=[pltpu.VMEM(w_tile, BF16)],
        compiler_params=_params(("parallel", "arbitrary")),
        name="matmul",
    )(a, w)


def _matmul_resid_kernel(*refs, n_in):
    a_refs = refs[:n_in]
    w_refs = refs[n_in:2 * n_in]
    x_ref, m_ref, o_ref = refs[2 * n_in:2 * n_in + 3]
    w_scrs = refs[2 * n_in + 3:]

    @pl.when(pl.program_id(1) == 0)
    def _():
        for w_ref, w_scr in zip(w_refs, w_scrs):
            w_scr[...] = w_ref[...].astype(BF16)

    acc = _dot(a_refs[0][...], w_scrs[0][...])
    for a_ref, w_scr in zip(a_refs[1:], w_scrs[1:]):
        acc = acc + _dot(a_ref[...], w_scr[...])
    o_ref[...] = x_ref[...] + m_ref[0][2:3] * acc


def _matmul_resid(a_list, w, x, m, seqlen):
    t, d = x.shape
    tm, tn = min(1024 if len(a_list) == 1 else 512, seqlen), 1024
    per = seqlen // tm
    n_in = len(a_list)
    kb = w.shape[0] // n_in
    arrays, in_specs = [], []
    for a in a_list:
        arr, blk = a if isinstance(a, tuple) else (a, 0)
        arrays.append(arr)
        in_specs.append(pl.BlockSpec((tm, kb), lambda j, i, blk=blk: (i, blk)))
    in_specs += [pl.BlockSpec((kb, tn), lambda j, i, r=r: (r, j)) for r in range(n_in)]
    in_specs += [pl.BlockSpec((tm, tn), lambda j, i: (i, j)),
                 pl.BlockSpec((1, 3, tn), lambda j, i: (i // per, 0, j))]
    return pl.pallas_call(
        functools.partial(_matmul_resid_kernel, n_in=n_in),
        grid=(d // tn, t // tm),
        in_specs=in_specs,
        out_specs=pl.BlockSpec((tm, tn), lambda j, i: (i, j)),
        out_shape=jax.ShapeDtypeStruct((t, d), F32),
        scratch_shapes=[pltpu.VMEM((kb, tn), BF16)] * n_in,
        compiler_params=_params(("parallel", "arbitrary")),
        name="matmul_resid",
    )(*arrays, *([w] * n_in), x, m)


def _s5_kernel(u_ref, lrg_ref, lig_ref, ldg_ref, lrl_ref, lil_ref, ldl_ref, btr_ref, bti_ref, cr_ref, ci_ref,
               o_ref, *, nc, levels):
    q, gc, p = S5_CHUNK, S5_GROUP_CH, S5_STATE
    ng = LANES // gc
    gp = ng * p
    rows = u_ref.shape[0] // q
    half = q // 2

    def discretize(lre, lim, ldt):
        lr = jnp.minimum(lre, -1e-4)
        dt = jnp.exp(ldt)
        mag = jnp.exp(lr * dt)
        lb_re, lb_im = mag * jnp.cos(lim * dt), mag * jnp.sin(lim * dt)
        den = lr * lr + lim * lim
        nr, ni = lb_re - 1.0, lb_im
        return lb_re, lb_im, (nr * lr + ni * lim) / den, (ni * lr - nr * lim) / den

    def powers(lb_re, lb_im, count):
        out = [(jnp.ones_like(lb_re), jnp.zeros_like(lb_im))]
        for _ in range(count - 1):
            pr, pi = out[-1]
            out.append((pr * lb_re - pi * lb_im, pr * lb_im + pi * lb_re))
        return out

    def split3(x):
        p1 = x.astype(BF16)
        r1 = x - p1.astype(F32)
        p2 = r1.astype(BF16)
        return p1, p2, (r1 - p2.astype(F32)).astype(BF16)

    ri = lax.broadcasted_iota(jnp.int32, (LANES, ng), 0)
    ci_ = lax.broadcasted_iota(jnp.int32, (LANES, ng), 1)
    to_rows = (ri // gc == ci_).astype(F32)
    expand = lambda a: _dot(to_rows, a, hi=True)
    pi_ = lax.broadcasted_iota(jnp.int32, (3 * p, gp), 0)
    pj_ = lax.broadcasted_iota(jnp.int32, (3 * p, gp), 1)
    to_lanes = (pi_ % p == pj_ % p).astype(BF16)
    tile = lambda a: jnp.dot(jnp.concatenate(split3(a), axis=1), to_lanes, preferred_element_type=F32)
    same_rc = (lax.broadcasted_iota(jnp.int32, (LANES, LANES), 0) // gc
               == lax.broadcasted_iota(jnp.int32, (LANES, LANES), 1) // gc)
    same_rl = (lax.broadcasted_iota(jnp.int32, (LANES, gp), 0) // gc
               == lax.broadcasted_iota(jnp.int32, (LANES, gp), 1) // p)

    lb_re_r, lb_im_r, f_re_r, f_im_r = [expand(a) for a in discretize(lrg_ref[...], lig_ref[...], ldg_ref[...])]
    lb_re_l, lb_im_l, f_re_l, f_im_l = discretize(lrl_ref[0], lil_ref[0], ldl_ref[0])
    pow_r = powers(lb_re_r, lb_im_r, q)
    pow_l = powers(lb_re_l, lb_im_l, q + 1)

    btr, bti = btr_ref[...], bti_ref[...]
    cr, ci = cr_ref[...], ci_ref[...]
    bb_re = f_re_r * btr - f_im_r * bti
    bb_im = f_re_r * bti + f_im_r * btr

    b_hi, b_lo, _ = split3(jnp.concatenate([bb_re, -bb_im], axis=1))
    b_cat = jnp.concatenate([b_hi, b_lo, b_hi], axis=1)
    w = []
    for pr, pi in pow_r:
        c_hi, c_lo, _ = split3(jnp.concatenate([cr * pr - ci * pi, cr * pi + ci * pr], axis=1))
        k_t = _dot_nt(b_cat, jnp.concatenate([c_hi, c_hi, c_lo], axis=1))
        w.append(jnp.where(same_rc, k_t, 0.0).astype(BF16))
    zero = jnp.zeros((LANES, LANES), BF16)

    def w_pair(delta):
        lo = w[2 * delta - 1] if delta > 0 else zero
        return jnp.concatenate([jnp.concatenate([w[2 * delta], w[2 * delta + 1]], axis=1),
                                jnp.concatenate([lo, w[2 * delta]], axis=1)], axis=0)

    u_pair = []
    for jp in range(half):
        u_pair.append(jnp.concatenate(
            [u_ref[pl.ds(2 * jp + jj, rows, stride=q), :].astype(BF16) for jj in range(2)], axis=1))

    bt_re, bt_im = tile(btr), tile(bti)
    bbl_re = jnp.where(same_rl, f_re_l * bt_re - f_im_l * bt_im, 0.0)
    bbl_im = jnp.where(same_rl, f_re_l * bt_im + f_im_l * bt_re, 0.0)
    st = None
    for jp in range(half):
        blocks = []
        for j in (2 * jp, 2 * jp + 1):
            pr, pi = pow_l[q - 1 - j]
            blocks.append(jnp.concatenate([bbl_re * pr - bbl_im * pi, bbl_re * pi + bbl_im * pr], axis=1))
        term = _dot(u_pair[jp], jnp.concatenate(blocks, axis=0))
        st = term if st is None else st + term
    sr, si = st[:, :gp], st[:, gp:]

    cidx = lax.broadcasted_iota(jnp.int32, (rows, 1), 0) % nc
    ar, ai = pow_l[q]
    for k in range(levels):
        sh = 1 << k
        if k > 0:
            ar, ai = ar * ar - ai * ai, 2.0 * ar * ai
        xr = pltpu.roll(sr, sh, axis=0)
        xi = pltpu.roll(si, sh, axis=0)
        ok = cidx >= sh
        sr, si = (sr + jnp.where(ok, ar * xr - ai * xi, 0.0),
                  si + jnp.where(ok, ar * xi + ai * xr, 0.0))
    ok = cidx >= 1
    s_prev = jnp.concatenate([jnp.where(ok, pltpu.roll(sr, 1, axis=0), 0.0),
                              jnp.where(ok, pltpu.roll(si, 1, axis=0), 0.0)], axis=1).astype(BF16)

    ct_re, ct_im = tile(cr), tile(ci)
    ct_re = jnp.where(same_rl, ct_re, 0.0)
    ct_im = jnp.where(same_rl, ct_im, 0.0)
    for ip in range(half):
        acc = None
        for jp in range(ip + 1):
            term = _dot(u_pair[jp], w_pair(ip - jp))
            acc = term if acc is None else acc + term
        blocks = []
        for i in (2 * ip, 2 * ip + 1):
            pr, pi = pow_l[i + 1]
            blocks.append(jnp.concatenate([ct_re * pr - ct_im * pi, -(ct_re * pi + ct_im * pr)], axis=1))
        acc = acc + _dot_nt(s_prev, jnp.concatenate(blocks, axis=0))
        o_ref[pl.ds(2 * ip, rows, stride=q), :] = acc[:, :LANES]
        o_ref[pl.ds(2 * ip + 1, rows, stride=q), :] = acc[:, LANES:]


def _s5_post_kernel(y_ref, u_ref, d_ref, w_ref, b_ref, o_ref):
    y = y_ref[...] + d_ref[...] * u_ref[...]
    y = jax.nn.gelu(y)
    o_ref[...] = (y * jax.nn.sigmoid(_dot(y, w_ref[...]) + b_ref[...])).astype(o_ref.dtype)


def _s5_branch(u, bsz, seqlen, lam_re, lam_im, log_dt, b_re, b_im, c_re, c_im, d_skip, glu_w, glu_b):
    t, width = u.shape
    q, gc, p = S5_CHUNK, S5_GROUP_CH, S5_STATE
    groups = width // gc
    nc = seqlen // q
    levels = int(math.log2(nc))
    assert (1 << levels) == nc
    ng = LANES // gc
    nblk = width // LANES
    gp = ng * p
    ldt = jnp.broadcast_to(log_dt[:, None], (groups, p))
    by_group = pl.BlockSpec((ng, p), lambda k: (k, 0))
    by_lane = pl.BlockSpec((1, 1, gp), lambda k: (k, 0, 0))
    by_chan = pl.BlockSpec((LANES, p), lambda k: (k, 0))
    tok = pl.BlockSpec((t, LANES), lambda k: (0, k))
    lanes3 = lambda a: a.reshape(nblk, 1, gp)
    chan2 = lambda a: a.reshape(width, p)
    y = pl.pallas_call(
        functools.partial(_s5_kernel, nc=nc, levels=levels),
        grid=(nblk,),
        in_specs=[tok, by_group, by_group, by_group, by_lane, by_lane, by_lane,
                  by_chan, by_chan, by_chan, by_chan],
        out_specs=tok,
        out_shape=jax.ShapeDtypeStruct((t, width), F32),
        compiler_params=_params(("parallel",)),
        name="s5_conv",
    )(u, lam_re, lam_im, ldt, lanes3(lam_re), lanes3(lam_im), lanes3(ldt),
      chan2(jnp.swapaxes(b_re, 1, 2)), chan2(jnp.swapaxes(b_im, 1, 2)), chan2(c_re), chan2(c_im))

    tm = min(512, t)
    return pl.pallas_call(
        _s5_post_kernel,
        grid=(t // tm,),
        in_specs=[pl.BlockSpec((tm, width), lambda i: (i, 0)),
                  pl.BlockSpec((tm, width), lambda i: (i, 0)),
                  pl.BlockSpec((1, width), lambda i: (0, 0)),
                  pl.BlockSpec((width, width), lambda i: (0, 0)),
                  pl.BlockSpec((1, width), lambda i: (0, 0))],
        out_specs=pl.BlockSpec((tm, width), lambda i: (i, 0)),
        out_shape=jax.ShapeDtypeStruct((t, width), BF16),
        compiler_params=_params(("parallel",)),
        name="s5_post",
    )(y, u, d_skip.reshape(1, width), glu_w.astype(BF16), glu_b.reshape(1, width))


def _ssd_kernel(z_ref, xbc_ref, dt_ref, cw_ref, cb_ref, dtb_ref, alog_ref, dsk_ref, ng_ref,
                o_ref, ext_scr, st_scr, *, heads, inner):
    lc = SSD_CHUNK
    hd, ns = SSD_HEAD_DIM, SSD_STATE
    rpg = heads // SSD_GROUPS
    gw = SSD_GROUPS * ns
    assert SSD_CONV - 1 <= 8

    @pl.when(pl.program_id(1) == 0)
    def _():
        ext_scr[0:8, :] = jnp.zeros((8, ext_scr.shape[1]), F32)
        st_scr[...] = jnp.zeros_like(st_scr)

    ext_scr[8:8 + lc, :] = xbc_ref[...]
    cw = cw_ref[...]
    ext = ext_scr[0:8 + lc, :]
    acc = cw[0:1] * ext
    for k in range(1, SSD_CONV):
        acc = cw[k:k + 1] * ext + pltpu.roll(acc, 1, axis=0)
    conv = cb_ref[...] + acc[8:8 + lc]
    ext_scr[0:8, :] = ext_scr[lc:lc + 8, :]
    act = _silu(conv)
    xs = act[:, :inner]
    bs = act[:, inner:inner + gw]
    cs = act[:, inner + gw:]

    dt = _softplus(dt_ref[...] + dtb_ref[...])
    adt = dt * (-jnp.exp(alog_ref[...]))
    ri = lax.broadcasted_iota(jnp.int32, (lc, lc), 0)
    ci = lax.broadcasted_iota(jnp.int32, (lc, lc), 1)
    causal = ri >= ci
    a_cum = _dot(causal.astype(F32), adt, hi=True)
    a_cum_t = a_cum.T
    dt_t = dt.T

    y_parts = []
    for g in range(SSD_GROUPS):
        cs_g = cs[:, g * ns:(g + 1) * ns]
        bs_g = bs[:, g * ns:(g + 1) * ns]
        cb = _dot_nt(cs_g, bs_g)
        bs_gt = bs_g.T
        for r in range(rpg):
            h = g * rpg + r
            col = a_cum[:, h:h + 1]
            row = a_cum_t[h:h + 1, :]
            dt_row = dt_t[h:h + 1, :]
            decay = jnp.exp(jnp.where(causal, col - row, -jnp.inf))
            xs_h = xs[:, h * hd:(h + 1) * hd]
            st = st_scr[h]
            lhs = jnp.concatenate([cb * decay * dt_row, cs_g * jnp.exp(col)], axis=1)
            y_parts.append(_dot(lhs, jnp.concatenate([xs_h, st], axis=0)))
            last = row[:, lc - 1:lc]
            to_end = jnp.exp(last - row)
            st_scr[h] = jnp.exp(last) * st + _dot(bs_gt * (to_end * dt_row), xs_h)
    y = jnp.concatenate(y_parts, axis=1) + dsk_ref[...] * xs
    y = y * _silu(z_ref[...])
    o_ref[...] = _rms(y, ng_ref[...]).astype(o_ref.dtype)


def _ssd_branch(z, xbc, dt_raw, bsz, seqlen, conv_w, conv_b, dt_bias, a_log, d_skip, norm_g):
    t, inner = z.shape
    cd = xbc.shape[1]
    heads = inner // SSD_HEAD_DIM
    lc = SSD_CHUNK
    nc = seqlen // lc
    pad = LANES - heads
    padv = lambda v: jnp.pad(v.reshape(1, heads), ((0, 0), (0, pad)))
    row = lambda n: pl.BlockSpec((1, n), lambda b, c: (0, 0))
    tok = lambda n: pl.BlockSpec((lc, n), lambda b, c: (b * nc + c, 0))
    return pl.pallas_call(
        functools.partial(_ssd_kernel, heads=heads, inner=inner),
        grid=(bsz, nc),
        in_specs=[tok(inner), tok(cd), tok(LANES),
                  pl.BlockSpec((SSD_CONV, cd), lambda b, c: (0, 0)), row(cd),
                  row(LANES), row(LANES), row(inner), row(inner)],
        out_specs=tok(inner),
        out_shape=jax.ShapeDtypeStruct((t, inner), BF16),
        scratch_shapes=[pltpu.VMEM((8 + lc + 8, cd), F32),
                        pltpu.VMEM((heads, SSD_STATE, SSD_HEAD_DIM), F32)],
        compiler_params=_params(("parallel", "arbitrary")),
        name="ssd",
    )(z, xbc, dt_raw, conv_w, conv_b.reshape(1, cd), padv(dt_bias), padv(a_log),
      jnp.repeat(d_skip, SSD_HEAD_DIM).reshape(1, inner), norm_g.reshape(1, inner))


def _hybrid_mixer(x, h, m, seqlen, idx, w_in, w_out, lam_re, lam_im, log_dt, b_re, b_im, c_re, c_im, s5_d,
                  glu_w, glu_b, conv_w, conv_b, dt_bias, a_log, ssd_d, ssd_norm_g):
    t, d = x.shape
    bsz = t // seqlen
    s5w = s5_d.shape[0]
    inner = ssd_norm_g.shape[0]
    cd = conv_w.shape[1]
    heads = dt_bias.shape[0]
    o1, o2, o3 = s5w, s5w + inner, s5w + inner + cd
    w_t = jnp.swapaxes(w_in, 1, 2)
    u = _matmul(h, w_t, idx, 0, o1, out_major=True)
    z = _matmul(h, w_t, idx, o1, inner, out_major=True)
    xbc = _matmul(h, w_t, idx, o2, cd, out_major=True)
    dt_raw = _matmul(h, w_t, idx, o3, heads, out_major=True)
    y_s5 = _s5_branch(u, bsz, seqlen, lam_re, lam_im, log_dt, b_re, b_im, c_re, c_im, s5_d, glu_w, glu_b)
    y_ssd = _ssd_branch(z, xbc, dt_raw, bsz, seqlen, conv_w, conv_b, dt_bias, a_log, ssd_d, ssd_norm_g)
    assert inner % s5w == 0
    return _matmul_resid([y_s5] + [(y_ssd, i) for i in range(inner // s5w)], w_out, x, m, seqlen)


def _rwkv_mix_kernel(x_ref, m_ref, g_ref, mu_ref, w1_ref, w2_ref, w0_ref, a1_ref, a2_ref, a0_ref, g1_ref, g2_ref,
                     xr_ref, xk_ref, xv_ref, lw_ref, a_ref, gate_ref, prev_scr):
    h = _adaln(x_ref[...], g_ref[...], m_ref[0])
    tm = h.shape[0]

    @pl.when(pl.program_id(1) == 0)
    def _():
        prev_scr[...] = jnp.zeros_like(prev_scr)

    first = lax.broadcasted_iota(jnp.int32, (tm, 1), 0) == 0
    shifted = jnp.where(first, prev_scr[7:8, :], pltpu.roll(h, 1, axis=0))
    prev_scr[...] = h[tm - 8:tm]
    xx = shifted - h
    mu = mu_ref[...]
    mix = lambda i: h + xx * mu[i:i + 1]
    xr_ref[...] = mix(0).astype(xr_ref.dtype)
    xk_ref[...] = mix(2).astype(xk_ref.dtype)
    xv_ref[...] = mix(3).astype(xv_ref.dtype)
    w = -_softplus(-(w0_ref[...] + _dot(jnp.tanh(_dot(mix(1), w1_ref[...])), w2_ref[...]))) - 0.5
    lw_ref[...] = -jnp.exp(w)
    a_ref[...] = jax.nn.sigmoid(a0_ref[...] + _dot(_dot(mix(4), a1_ref[...]), a2_ref[...]))
    gate_ref[...] = _dot(jax.nn.sigmoid(_dot(mix(5), g1_ref[...])), g2_ref[...])


def _rwkv_mix(x, m, g, mu, w1, w2, w0, a1, a2, a0, g1, g2, bsz, seqlen):
    t, d = x.shape
    tm = min(256, seqlen)
    per = seqlen // tm
    tok = pl.BlockSpec((tm, d), lambda b, i: (b * per + i, 0))
    vec = pl.BlockSpec((1, d), lambda b, i: (0, 0))
    whole = lambda a: pl.BlockSpec(a.shape, lambda b, i: (0, 0))

    def lora(a, b):
        rank = a.shape[1]
        rp = -(-rank // LANES) * LANES
        return (jnp.pad(a, ((0, 0), (0, rp - rank))).astype(BF16),
                jnp.pad(b, ((0, rp - rank), (0, 0))).astype(BF16))

    (w1p, w2p), (a1p, a2p), (g1p, g2p) = lora(w1, w2), lora(a1, a2), lora(g1, g2)
    return pl.pallas_call(
        _rwkv_mix_kernel,
        grid=(bsz, per),
        in_specs=[tok, pl.BlockSpec((1, 3, d), lambda b, i: (b, 0, 0)), vec,
                  pl.BlockSpec((8, d), lambda b, i: (0, 0)),
                  whole(w1p), whole(w2p), vec, whole(a1p), whole(a2p), vec, whole(g1p), whole(g2p)],
        out_specs=[tok] * 6,
        out_shape=[jax.ShapeDtypeStruct((t, d), BF16)] * 3 + [jax.ShapeDtypeStruct((t, d), F32)] * 3,
        scratch_shapes=[pltpu.VMEM((8, d), F32)],
        compiler_params=_params(("parallel", "arbitrary")),
        name="rwkv_mix",
    )(x, m, g.reshape(1, d), jnp.pad(mu, ((0, 2), (0, 0))),
      w1p, w2p, w0.reshape(1, d), a1p, a2p, a0.reshape(1, d), g1p, g2p)


def _rwkv_scan_kernel(r_ref, k_ref, v_ref, lw_ref, a_ref, g_ref, kk_ref, ka_ref, rk_ref, lg_ref, lb_ref,
                      o_ref, s_scr):
    c = RWKV_CHUNK
    hd = RWKV_HEAD_DIM
    c2 = 2 * c
    tokens = r_ref.shape[0]
    pairs = r_ref.shape[1] // LANES
    tb = pairs * tokens
    per_pair = tokens // c
    n_chunks = tb // c

    @pl.when(pl.program_id(2) == 0)
    def _():
        s_scr[...] = jnp.zeros_like(s_scr)

    def rows_of(ref):
        return jnp.concatenate([ref[:, p * LANES:(p + 1) * LANES] for p in range(pairs)], axis=0)

    def param_rows(ref):
        return jnp.concatenate([jnp.broadcast_to(ref[:, p * LANES:(p + 1) * LANES], (tokens, LANES))
                                for p in range(pairs)], axis=0)

    lane_head = lax.broadcasted_iota(jnp.int32, (1, LANES), 1) // hd
    head0 = lane_head == 0
    same_head = ((lax.broadcasted_iota(jnp.int32, (LANES, LANES), 0) // hd)
                 == (lax.broadcasted_iota(jnp.int32, (LANES, LANES), 1) // hd))
    r2 = lax.broadcasted_iota(jnp.int32, (1, c2, c2), 1)
    q2 = lax.broadcasted_iota(jnp.int32, (1, c2, c2), 2)
    same_blk = (r2 // c) == (q2 // c)
    strict = same_blk & ((r2 % c) > (q2 % c))
    incl = same_blk & ((r2 % c) >= (q2 % c))
    eye2 = (r2 == q2).astype(F32)
    k_k, k_a, r_k = param_rows(kk_ref), param_rows(ka_ref), param_rows(rk_ref)

    def stack(x):
        x = x.reshape(n_chunks, c, LANES)
        return jnp.concatenate([jnp.where(head0, x, 0.0), jnp.where(head0, 0.0, x)], axis=1)

    def unstack(x):
        return (x[:, :c] + x[:, c:]).reshape(tb, LANES)

    def bdot(p, q, dims):
        return lax.dot_general(p.astype(BF16), q.astype(BF16), (dims, ((0,), (0,))), preferred_element_type=F32)

    bmm = lambda p, q: bdot(p, q, ((2,), (1,)))
    bmm_nt = lambda p, q: bdot(p, q, ((2,), (2,)))
    bmm_tn = lambda p, q: bdot(p, q, ((1,), (1,)))

    def split(x):
        hi_part = x.astype(BF16)
        return hi_part, (x - hi_part.astype(F32)).astype(BF16)

    def head_sum(x):
        s0 = jnp.sum(jnp.where(head0, x, 0.0), axis=1, keepdims=True)
        s1 = jnp.sum(jnp.where(head0, 0.0, x), axis=1, keepdims=True)
        return jnp.where(head0, s0, s1)

    def cumsum_rows(x):
        row_in_chunk = lax.broadcasted_iota(jnp.int32, (tb, 1), 0) % c
        shift = 1
        while shift < c:
            x = x + jnp.where(row_in_chunk >= shift, pltpu.roll(x, shift, axis=0), 0.0)
            shift *= 2
        return x

    def lhs3(hi_part, lo_part):
        return jnp.concatenate([hi_part, lo_part], axis=2)

    def rhs3(hi_part, lo_part):
        top = jnp.concatenate([hi_part, lo_part], axis=2)
        bot = jnp.concatenate([hi_part, jnp.zeros_like(lo_part)], axis=2)
        return jnp.concatenate([top, bot], axis=1)

    def fold(x):
        return x[:, :, :c2] + x[:, :, c2:]

    r, k, v, lw, a = rows_of(r_ref), rows_of(k_ref), rows_of(v_ref), rows_of(lw_ref), rows_of(a_ref)
    kk = k * k_k
    k2 = k * (1.0 + (a - 1.0) * k_a)
    rk_sum = head_sum(r * k2 * r_k)
    kk = kk / jnp.maximum(jnp.sqrt(head_sum(kk * kk)), 1e-12)
    bv = kk * a
    cum = cumsum_rows(lw)
    cum3 = cum.reshape(n_chunks, c, LANES)
    tot = jnp.broadcast_to(cum3[:, c - 1:c, :], cum3.shape).reshape(tb, LANES)
    g_inv = jnp.exp(-cum)
    rt_u = r * jnp.exp(cum)
    at = stack(-kk * jnp.exp(cum - lw))
    rt = stack(rt_u)
    bt = stack(bv * g_inv)
    kt = stack(k2 * g_inv)
    to_end = jnp.exp(tot - cum)
    b_end = (bv * to_end).reshape(n_chunks, c, LANES)
    k_end = (k2 * to_end).reshape(n_chunks, c, LANES)
    vs = stack(v)

    gram = bmm_nt(jnp.concatenate([at, rt], axis=1), jnp.concatenate([bt, kt], axis=1))
    n_ab = jnp.where(strict, gram[:, :c2, :c2], 0.0)
    a_ak = jnp.where(strict, gram[:, :c2, c2:], 0.0)
    a_rb = jnp.where(incl, gram[:, c2:, :c2], 0.0)
    a_rk = jnp.where(incl, gram[:, c2:, c2:], 0.0)
    pw_h, pw_l = split(n_ab)
    pw = fold(bmm(lhs3(pw_h, pw_l), rhs3(pw_h, pw_l)))
    tinv = eye2 + n_ab
    for _ in range(int(math.log2(c)) - 2):
        pw_h, pw_l = split(pw)
        t_h, t_l = split(tinv)
        both = fold(bmm(jnp.concatenate([lhs3(pw_h, pw_l), lhs3(t_h, t_l)], axis=1), rhs3(pw_h, pw_l)))
        pw = both[:, :c2]
        tinv = tinv + both[:, c2:]
    pw_h, pw_l = split(pw)
    t_h, t_l = split(tinv)
    tinv = tinv + fold(bmm(lhs3(t_h, t_l), rhs3(pw_h, pw_l)))
    pq = bmm(tinv, jnp.concatenate([at, bmm(a_ak, vs)], axis=2))
    ry = bmm(jnp.concatenate([a_rb, a_rk], axis=2),
             jnp.concatenate([pq, jnp.concatenate([jnp.zeros_like(vs), vs], axis=2)], axis=1))
    rq = rt_u + unstack(ry[:, :, :LANES])
    y0 = unstack(ry[:, :, LANES:])
    p_u = unstack(pq[:, :, :LANES]).reshape(n_chunks, c, LANES)
    q_u = unstack(pq[:, :, LANES:]).reshape(n_chunks, c, LANES)
    gm = jnp.where(same_head, bmm_tn(p_u, b_end), 0.0)
    dm = jnp.where(same_head, bmm_tn(jnp.concatenate([q_u, v.reshape(n_chunks, c, LANES)], axis=1),
                                     jnp.concatenate([b_end, k_end], axis=1)), 0.0)
    g_tot = jnp.exp(cum3[:, c - 1:c, :])
    bonus = rk_sum * v

    s = [s_scr[p] for p in range(pairs)]
    ys = [[] for _ in range(pairs)]
    for i in range(per_pair):
        for p in range(pairs):
            n = p * per_pair + i
            ys[p].append(_dot_nt(rq[n * c:(n + 1) * c], s[p]))
            s[p] = s[p] * g_tot[n] + _dot(s[p], gm[n]) + dm[n]
    for p in range(pairs):
        s_scr[p] = s[p]
    y = jnp.concatenate([y_c for y_p in ys for y_c in y_p], axis=0) + y0
    mean = head_sum(y) * (1.0 / hd)
    dy = y - mean
    var = head_sum(dy * dy) * (1.0 / hd)
    yn = dy * lax.rsqrt(var + RWKV_GN_EPS) * param_rows(lg_ref) + param_rows(lb_ref)
    out = ((yn + bonus) * rows_of(g_ref)).astype(o_ref.dtype)
    for p in range(pairs):
        o_ref[:, p * LANES:(p + 1) * LANES] = out[p * tokens:(p + 1) * tokens]


def _rwkv_scan(r, k, v, lw, a, g, k_k, k_a, r_k, ln_g, ln_b, bsz, seqlen):
    t, d = r.shape
    tb = min(RWKV_TOKEN_BLOCK, seqlen)
    per = seqlen // tb
    pairs = RWKV_PAIRS_PER_BLOCK
    bw = pairs * LANES
    tok = pl.BlockSpec((tb, bw), lambda b, h, i: (b * per + i, h))
    row = pl.BlockSpec((1, bw), lambda b, h, i: (0, h))
    vec = lambda p: p.reshape(1, d)
    return pl.pallas_call(
        _rwkv_scan_kernel,
        grid=(bsz, d // bw, per),
        in_specs=[tok] * 6 + [row] * 5,
        out_specs=tok,
        out_shape=jax.ShapeDtypeStruct((t, d), BF16),
        scratch_shapes=[pltpu.VMEM((pairs, LANES, LANES), F32)],
        compiler_params=_params(("parallel", "parallel", "arbitrary")),
        name="rwkv_scan",
    )(r, k, v, lw, a, g, vec(k_k), vec(k_a), vec(r_k), vec(ln_g), vec(ln_b))


def _rwkv_mixer(x, m, g, seqlen, idx, mu, w_r, w_k, w_v, w_o, w0, w1, w2, a0, a1, a2, g1, g2,
                k_k, k_a, r_k, ln_g, ln_b):
    t, d = x.shape
    bsz = t // seqlen
    xr, xk, xv, lw, a, gate = _rwkv_mix(x, m, g, mu, w1, w2, w0, a1, a2, a0, g1, g2, bsz, seqlen)
    r = _matmul(xr, w_r, idx)
    k = _matmul(xk, w_k, idx)
    v = _matmul(xv, w_v, idx)
    y = _rwkv_scan(r, k, v, lw, a, gate, k_k, k_a, r_k.reshape(-1), ln_g, ln_b, bsz, seqlen)
    return _matmul_resid([y], w_o, x, m, seqlen)


def kernel(x, c, w_mod, b_mod, norm_g, ffn_w1, ffn_w3, ffn_w2, hyb_w_in, hyb_w_out, s5_lambda_re, s5_lambda_im, s5_log_dt, s5_b_re, s5_b_im, s5_c_re, s5_c_im, s5_d, s5_glu_w, s5_glu_b, ssd_conv_w, ssd_conv_b, ssd_dt_bias, ssd_a_log, ssd_d, ssd_norm_g, rwkv_mu, rwkv_w_r, rwkv_w_k, rwkv_w_v, rwkv_w_o, rwkv_w0, rwkv_w1, rwkv_w2, rwkv_a0, rwkv_a1, rwkv_a2, rwkv_g1, rwkv_g2, rwkv_k_k, rwkv_k_a, rwkv_r_k, rwkv_ln_g, rwkv_ln_b, final_g):
    bsz, seqlen, d = x.shape
    depth = w_mod.shape[0]
    xf = x.reshape(bsz * seqlen, d)
    mod = _modulation(c, w_mod, b_mod).reshape(depth, bsz, 3, 3, d)
    for layer in range(depth):
        i = layer // 2
        m0, m1, m2 = mod[layer, :, 0], mod[layer, :, 1], mod[layer, :, 2]
        if layer % 2 == 0:
            xf, h = _ffn(xf, m0, norm_g[layer, 0], ffn_w1, ffn_w3, ffn_w2, layer, 0, seqlen,
                         next_ln=(norm_g[layer, 1], m1))
            xf = _hybrid_mixer(xf, h, m1, seqlen, i, hyb_w_in, hyb_w_out[i], s5_lambda_re[i],
                               s5_lambda_im[i], s5_log_dt[i], s5_b_re[i], s5_b_im[i], s5_c_re[i], s5_c_im[i],
                               s5_d[i], s5_glu_w[i], s5_glu_b[i], ssd_conv_w[i], ssd_conv_b[i], ssd_dt_bias[i],
                               ssd_a_log[i], ssd_d[i], ssd_norm_g[i])
        else:
            xf = _ffn(xf, m0, norm_g[layer, 0], ffn_w1, ffn_w3, ffn_w2, layer, 0, seqlen)
            xf = _rwkv_mixer(xf, m1, norm_g[layer, 1], seqlen, i, rwkv_mu[i], rwkv_w_r, rwkv_w_k, rwkv_w_v,
                             rwkv_w_o[i], rwkv_w0[i], rwkv_w1[i], rwkv_w2[i], rwkv_a0[i], rwkv_a1[i], rwkv_a2[i],
                             rwkv_g1[i], rwkv_g2[i], rwkv_k_k[i], rwkv_k_a[i], rwkv_r_k[i], rwkv_ln_g[i],
                             rwkv_ln_b[i])
        xf = _ffn(xf, m2, norm_g[layer, 2], ffn_w1, ffn_w3, ffn_w2, layer, 1, seqlen,
                  final_g=final_g if layer == depth - 1 else None)
    return xf.reshape(bsz, seqlen, d)
```
